```python
import math
import jax, jax.numpy as jnp
from jax import lax
import numpy as np

D_MODEL = 1024
BATCH = 8
SEQ = 2048
DEPTH = 2
DEC_BATCH = 128
DEC_SEQ = 1
PAST_LEN = 16384
PAGE_SIZE = 128

N_BR = 5
W_BR = D_MODEL // 2
N_HEADS = 4
HEAD_DIM = W_BR // N_HEADS
CONV_W = 4
N_MEM = 256
D_FF = -(-8 * D_MODEL // (3 * 256)) * 256
LRU_C = 8.0
GDN_CHUNK = 64
HGRN_CHUNK = 16
RET_CHUNK = 64
ROPE_BASE = 10000.0
LN_EPS = 1e-5
NORM_EPS = 1e-6
DEEPNORM_ALPHA = (2 * DEPTH) ** 0.25
DEEPNORM_BETA = (8 * DEPTH) ** -0.25
IN_SPLITS = (W_BR, 3 * W_BR, W_BR, N_HEADS, N_HEADS, W_BR, W_BR, W_BR, W_BR, W_BR, W_BR, W_BR, W_BR, W_BR)
N_IN = sum(IN_SPLITS)

kernel_name = "hybrid_gated_parallel_recurrent_decoder_step"


def _split(a, sizes):
    idx = np.cumsum(sizes)[:-1].tolist()
    return jnp.split(a, idx, axis=-1)


def _heads(a):
    return a.reshape(a.shape[:-1] + (N_HEADS, HEAD_DIM))


def _layernorm(x, g, b):
    xf = x.astype(jnp.float32)
    mu = jnp.mean(xf, -1, keepdims=True)
    var = jnp.mean(jnp.square(xf - mu), -1, keepdims=True)
    return ((xf - mu) * lax.rsqrt(var + LN_EPS) * g + b).astype(x.dtype)


def _head_rmsnorm(o, w):
    of = o.astype(jnp.float32)
    of = of * lax.rsqrt(jnp.mean(of * of, -1, keepdims=True) + NORM_EPS)
    return of.reshape(o.shape[:2] + (-1,)) * w


def _head_groupnorm(o, w, b):
    of = o.astype(jnp.float32)
    mu = jnp.mean(of, -1, keepdims=True)
    var = jnp.mean(jnp.square(of - mu), -1, keepdims=True)
    return ((of - mu) * lax.rsqrt(var + NORM_EPS)).reshape(o.shape[:2] + (-1,)) * w + b


def _l2norm(a):
    af = a.astype(jnp.float32)
    return af * lax.rsqrt(jnp.sum(af * af, -1, keepdims=True) + NORM_EPS)


def _causal_conv(x, buf, w, b=None):
    T = x.shape[1]
    xp = jnp.concatenate([buf.astype(x.dtype), x], axis=1)
    out = w[0] * xp[:, 0:T]
    for j in range(1, CONV_W):
        out = out + w[j] * xp[:, j:j + T]
    if b is not None:
        out = out + b
    return out, xp[:, T:]


def _rotary(x, pos):
    half = HEAD_DIM // 2
    inv = ROPE_BASE ** (-jnp.arange(half, dtype=jnp.float32) / half)
    ang = pos.astype(jnp.float32)[:, None] * inv
    cos, sin = jnp.cos(ang)[None, :, None, :], jnp.sin(ang)[None, :, None, :]
    xf = x.astype(jnp.float32)
    x1, x2 = xf[..., :half], xf[..., half:]
    return jnp.concatenate([x1 * cos - x2 * sin, x2 * cos + x1 * sin], axis=-1)


def _lin_comb(e1, e2):
    a1, b1 = e1
    a2, b2 = e2
    return a1 * a2, a2 * b1 + b2


def _rglru(xc, h0, wa, ba, wi, bi, lam):
    f32 = jnp.float32
    xh = _heads(xc)
    r = jax.nn.sigmoid((jnp.einsum("bthi,hij->bthj", xh, wa).reshape(xc.shape) + ba).astype(f32))
    ig = jax.nn.sigmoid((jnp.einsum("bthi,hij->bthj", xh, wi).reshape(xc.shape) + bi).astype(f32))
    log_a = -LRU_C * r * jax.nn.softplus(-lam.astype(f32))
    a = jnp.exp(log_a)
    b = jnp.sqrt(-jnp.expm1(2.0 * log_a)) * (ig * xc.astype(f32))
    b = b.at[:, 0].add(a[:, 0] * h0.astype(f32))
    _, h = lax.associative_scan(_lin_comb, (a, b), axis=1)
    return h, h[:, -1]


def _chunk_len(T, chunk):
    return chunk if T % chunk == 0 else T


def _to_chunks(a, N, C):
    B_, T, H, X = a.shape
    return a.reshape(B_, N, C, H, X).transpose(1, 0, 3, 2, 4)


def _from_chunks(o):
    N, B_, H, C, X = o.shape
    return o.transpose(1, 0, 3, 2, 4).reshape(B_, N * C, H, X)


def _chunked_gla(q, k, v, log_f, s0, chunk):
    f32 = jnp.float32
    T = q.shape[1]
    C = _chunk_len(T, chunk)
    N = T // C
    qc, kc, vc, gc = (_to_chunks(a.astype(f32), N, C) for a in (q, k, v, log_f))
    G = jnp.cumsum(gc, axis=3)
    q_g = qc * jnp.exp(G)
    k_g = kc * jnp.exp(-G)
    incl = jnp.tril(jnp.ones((C, C), bool))
    a_qk = jnp.where(incl, jnp.einsum("nbhik,nbhjk->nbhij", q_g, k_g), 0.0)
    o_intra = jnp.einsum("nbhij,nbhjv->nbhiv", a_qk, vc)
    g_last = G[:, :, :, -1]
    k_end = kc * jnp.exp(g_last[:, :, :, None] - G)

    def step(S, xs):
        qg, ke, vv, gl = xs
        o = jnp.einsum("bhck,bhkv->bhcv", qg, S)
        S = jnp.exp(gl)[..., None] * S + jnp.einsum("bhck,bhcv->bhkv", ke, vv)
        return S, o

    s_fin, o_inter = lax.scan(step, s0.astype(f32), (q_g, k_end, vc, g_last))
    return _from_chunks(o_intra + o_inter), s_fin


def _chunked_gdn(q, k, v, g, beta, s0, chunk):
    f32 = jnp.float32
    T = q.shape[1]
    V = v.shape[-1]
    C = _chunk_len(T, chunk)
    N = T // C
    qc, kc, vc = (_to_chunks(a.astype(f32), N, C) for a in (q, k, v))
    gc, bc = (_to_chunks(a.astype(f32)[..., None], N, C)[..., 0] for a in (g, beta))
    G = jnp.cumsum(gc, axis=-1)
    incl = jnp.tril(jnp.ones((C, C), bool))
    strict = jnp.tril(jnp.ones((C, C), bool), -1)
    decay = jnp.exp(jnp.where(incl, G[..., :, None] - G[..., None, :], -jnp.inf))
    kb = kc * bc[..., None]
    a_kk = jnp.where(strict, jnp.einsum("nbhik,nbhjk->nbhij", kb, kc) * decay, 0.0)
    rhs = jnp.concatenate([vc * bc[..., None], kb * jnp.exp(G)[..., None]], axis=-1)
    sol = lax.linalg.triangular_solve(a_kk + jnp.eye(C, dtype=f32), rhs, left_side=True, lower=True, unit_diagonal=True)
    u, w = sol[..., :V], sol[..., V:]
    a_qk = jnp.einsum("nbhik,nbhjk->nbhij", qc, kc) * decay
    q_g = qc * jnp.exp(G)[..., None]
    g_last = G[..., -1]
    k_end = kc * jnp.exp(g_last[..., None] - G)[..., None]

    def step(S, xs):
        qg, ke, uu, ww, aqk, gl = xs
        v_new = uu - jnp.einsum("bhck,bhkv->bhcv", ww, S)
        o = jnp.einsum("bhck,bhkv->bhcv", qg, S) + jnp.einsum("bhij,bhjv->bhiv", aqk, v_new)
        S = jnp.exp(gl)[..., None, None] * S + jnp.einsum("bhck,bhcv->bhkv", ke, v_new)
        return S, o

    s_fin, o = lax.scan(step, s0.astype(f32), (q_g, k_end, u, w, a_qk, g_last))
    return _from_chunks(o), s_fin


def _cross_attn(q, mem_k, mem_v):
    s = jnp.einsum("bthd,bmhd->bhtm", q, mem_k).astype(jnp.float32) * HEAD_DIM ** -0.5
    p = jax.nn.softmax(s, axis=-1)
    o = jnp.einsum("bhtm,bmhd->bthd", p.astype(mem_v.dtype), mem_v)
    return o.reshape(o.shape[:2] + (-1,))


def _layer(x, mem_k, mem_v, pos, lb, state, p):
    f32 = jnp.float32
    dt = x.dtype
    h_lru, conv_lru, conv_gdn, s_gdn, s_hgrn, s_ret = state
    (xa, gdn_qkv, gdn_z, gdn_b, gdn_a, hg_q, hg_f, hg_i, hg_g,
     rt_q, rt_k, rt_v, rt_g, xq) = _split(x @ p["w_in"], IN_SPLITS)

    xa_c, conv_lru_new = _causal_conv(xa, conv_lru, p["lru_conv_w"], p["lru_conv_b"])
    y_a, h_lru_new = _rglru(xa_c, h_lru, p["lru_wa"], p["lru_ba"], p["lru_wi"], p["lru_bi"], p["lru_lambda"])

    qkv_c, conv_gdn_new = _causal_conv(gdn_qkv, conv_gdn, p["gdn_conv_w"])
    gq, gk, gv = (_heads(t) for t in jnp.split(jax.nn.silu(qkv_c), 3, axis=-1))
    gq = _l2norm(gq) * HEAD_DIM ** -0.5
    gk = _l2norm(gk)
    beta = jax.nn.sigmoid(gdn_b.astype(f32))
    g_log = -jnp.exp(p["gdn_a_log"].astype(f32)) * jax.nn.softplus(gdn_a.astype(f32) + p["gdn_dt_bias"].astype(f32))
    o_b, s_gdn_new = _chunked_gdn(gq, gk, gv, g_log, beta, s_gdn, GDN_CHUNK)
    y_b = _head_rmsnorm(o_b, p["gdn_norm_w"]) * jax.nn.silu(gdn_z.astype(f32))

    fz = hg_f.astype(f32)
    log_f = jnp.logaddexp(jnp.log(lb), jnp.log1p(-lb) + jax.nn.log_sigmoid(fz))
    k_c = (1.0 - lb) * jax.nn.sigmoid(-fz)
    o_c, s_hgrn_new = _chunked_gla(_heads(jax.nn.silu(hg_q.astype(f32))), _heads(k_c), _heads(hg_i),
                                   _heads(log_f), s_hgrn, HGRN_CHUNK)
    y_c = _head_rmsnorm(o_c, p["hgrn_norm_w"]) * jax.nn.sigmoid(hg_g.astype(f32))

    rq = _rotary(_heads(rt_q), pos)
    rk = _rotary(_heads(rt_k), pos) * HEAD_DIM ** -0.5
    log_gamma = jnp.log1p(-jnp.exp2(-5.0 - jnp.arange(N_HEADS, dtype=f32)))
    log_f_d = jnp.broadcast_to(log_gamma[:, None], rq.shape)
    o_d, s_ret_new = _chunked_gla(rq, rk, _heads(rt_v), log_f_d, s_ret, RET_CHUNK)
    y_d = _head_groupnorm(o_d, p["ret_gn_w"], p["ret_gn_b"]) * jax.nn.silu(rt_g.astype(f32))

    y_e = _cross_attn(_heads(xq), mem_k, mem_v)

    merged = jnp.zeros(x.shape, f32)
    for n, y in enumerate((y_a, y_b, y_c, y_d, y_e)):
        gate = jax.nn.sigmoid((x @ p["w_merge_gate"][n] + p["b_merge_gate"][n]).astype(f32))
        merged = merged + gate * (y.astype(dt) @ p["w_branch"][n]).astype(f32)
    mix = merged.astype(dt) @ p["w_out"]
    x = _layernorm(DEEPNORM_ALPHA * x + mix, p["ln1_g"], p["ln1_b"])

    hg, hv = jnp.split(x @ p["w_ffn_up"], 2, axis=-1)
    ffn = (jax.nn.silu(hg) * hv) @ p["w_ffn_down"]
    x = _layernorm(DEEPNORM_ALPHA * x + ffn, p["ln2_g"], p["ln2_b"])
    return x, (h_lru_new, conv_lru_new, conv_gdn_new, s_gdn_new, s_hgrn_new, s_ret_new)


def setup_inputs(seed: int = 0) -> dict:
    key = jax.random.key(seed)
    ks = iter(jax.random.split(key, 48))
    f32 = jnp.float32

    def nrm(shape, scale):
        return jax.random.normal(next(ks), shape, f32) * scale

    def unif(shape, lo, hi):
        return jax.random.uniform(next(ks), shape, f32, lo, hi)

    L, D, W, H, HD = DEPTH, D_MODEL, W_BR, N_HEADS, HEAD_DIM
    log_s = jnp.log(unif((L, W), 0.9, 0.999)) / LRU_C
    lru_lambda = log_s - jnp.log(-jnp.expm1(log_s))
    dt = jnp.exp(unif((L, H), math.log(1e-3), math.log(1e-1)))
    gdn_dt_bias = dt + jnp.log(-jnp.expm1(-dt))
    gdn_a_log = jnp.log(unif((L, H), 1.0, 16.0))
    return {
        "x_prompt": nrm((BATCH, SEQ, D), 1.0),
        "x_sample": nrm((DEC_BATCH, DEC_SEQ, D), 1.0),
        "mem_prompt": nrm((BATCH, N_MEM, D), 1.0),
        "state_lru_h": nrm((L, DEC_BATCH, W), 0.5),
        "state_lru_conv": nrm((L, DEC_BATCH, CONV_W - 1, W), 1.0),
        "state_gdn_conv": nrm((L, DEC_BATCH, CONV_W - 1, 3 * W), 1.0),
        "state_gdn_s": nrm((L, DEC_BATCH, H, HD, HD), 0.1),
        "state_hgrn_s": nrm((L, DEC_BATCH, H, HD, HD), 0.3),
        "state_ret_s": nrm((L, DEC_BATCH, H, HD, HD), 1.0),
        "cache_mem_k": nrm((L, DEC_BATCH, N_MEM, H, HD), 1.0),
        "cache_mem_v": nrm((L, DEC_BATCH, N_MEM, H, HD), 1.0),
        "w_in": nrm((L, D, N_IN), D ** -0.5),
        "lru_conv_w": nrm((L, CONV_W, W), CONV_W ** -0.5),
        "lru_conv_b": nrm((L, W), 0.01),
        "lru_wa": nrm((L, H, HD, HD), HD ** -0.5),
        "lru_ba": nrm((L, W), 0.01),
        "lru_wi": nrm((L, H, HD, HD), HD ** -0.5),
        "lru_bi": nrm((L, W), 0.01),
        "lru_lambda": lru_lambda,
        "gdn_conv_w": nrm((L, CONV_W, 3 * W), CONV_W ** -0.5),
        "gdn_a_log": gdn_a_log,
        "gdn_dt_bias": gdn_dt_bias,
        "gdn_norm_w": 1.0 + nrm((L, W), 0.02),
        "hgrn_lb_raw": nrm((L, W), 0.1),
        "hgrn_norm_w": 1.0 + nrm((L, W), 0.02),
        "ret_gn_w": 1.0 + nrm((L, W), 0.02),
        "ret_gn_b": nrm((L, W), 0.01),
        "w_mem_k": nrm((L, D, W), D ** -0.5),
        "w_mem_v": nrm((L, D, W), D ** -0.5),
        "w_merge_gate": nrm((L, N_BR, D, D), D ** -0.5),
        "b_merge_gate": nrm((L, N_BR, D), 0.01),
        "w_branch": nrm((L, N_BR, W, D), W ** -0.5 * DEEPNORM_BETA),
        "w_out": nrm((L, D, D), D ** -0.5 * DEEPNORM_BETA),
        "ln1_g": 1.0 + nrm((L, D), 0.02),
        "ln1_b": nrm((L, D), 0.01),
        "w_ffn_up": nrm((L, D, 2 * D_FF), D ** -0.5),
        "w_ffn_down": nrm((L, D_FF, D), D_FF ** -0.5 * DEEPNORM_BETA),
        "ln2_g": 1.0 + nrm((L, D), 0.02),
        "ln2_b": nrm((L, D), 0.01),
    }


def reference(x_prompt, x_sample, mem_prompt,
              state_lru_h, state_lru_conv, state_gdn_conv, state_gdn_s, state_hgrn_s, state_ret_s,
              cache_mem_k, cache_mem_v,
              w_in, lru_conv_w, lru_conv_b, lru_wa, lru_ba, lru_wi, lru_bi, lru_lambda,
              gdn_conv_w, gdn_a_log, gdn_dt_bias, gdn_norm_w,
              hgrn_lb_raw, hgrn_norm_w, ret_gn_w, ret_gn_b,
              w_mem_k, w_mem_v, w_merge_gate, b_merge_gate, w_branch, w_out,
              ln1_g, ln1_b, w_ffn_up, w_ffn_down, ln2_g, ln2_b):
    f32 = jnp.float32
    B_p, T_p = x_prompt.shape[:2]
    T_s = x_sample.shape[1]
    n_mem = mem_prompt.shape[1]
    pos_p = jnp.arange(T_p)
    pos_s = PAST_LEN + jnp.arange(T_s)
    lb_cum = jnp.cumsum(jax.nn.softmax(hgrn_lb_raw.astype(f32), axis=0), axis=0)
    lb_all = lb_cum - lb_cum[0]
    zero_state = (jnp.zeros((B_p, W_BR), f32),
                  jnp.zeros((B_p, CONV_W - 1, W_BR), x_prompt.dtype),
                  jnp.zeros((B_p, CONV_W - 1, 3 * W_BR), x_prompt.dtype),
                  jnp.zeros((B_p, N_HEADS, HEAD_DIM, HEAD_DIM), f32),
                  jnp.zeros((B_p, N_HEADS, HEAD_DIM, HEAD_DIM), f32),
                  jnp.zeros((B_p, N_HEADS, HEAD_DIM, HEAD_DIM), f32))
    xp, xs = x_prompt, x_sample
    new_p, new_s, mem_k_p, mem_v_p = [], [], [], []
    for l in range(DEPTH):
        p = dict(w_in=w_in[l], lru_conv_w=lru_conv_w[l], lru_conv_b=lru_conv_b[l], lru_wa=lru_wa[l],
                 lru_ba=lru_ba[l], lru_wi=lru_wi[l], lru_bi=lru_bi[l], lru_lambda=lru_lambda[l],
                 gdn_conv_w=gdn_conv_w[l], gdn_a_log=gdn_a_log[l], gdn_dt_bias=gdn_dt_bias[l],
                 gdn_norm_w=gdn_norm_w[l], hgrn_norm_w=hgrn_norm_w[l], ret_gn_w=ret_gn_w[l],
                 ret_gn_b=ret_gn_b[l], w_merge_gate=w_merge_gate[l], b_merge_gate=b_merge_gate[l],
                 w_branch=w_branch[l], w_out=w_out[l], ln1_g=ln1_g[l], ln1_b=ln1_b[l],
                 w_ffn_up=w_ffn_up[l], w_ffn_down=w_ffn_down[l], ln2_g=ln2_g[l], ln2_b=ln2_b[l])
        mk = (mem_prompt @ w_mem_k[l]).reshape(B_p, n_mem, N_HEADS, HEAD_DIM)
        mv = (mem_prompt @ w_mem_v[l]).reshape(B_p, n_mem, N_HEADS, HEAD_DIM)
        xp, sp = _layer(xp, mk, mv, pos_p, lb_all[l], zero_state, p)
        st_s = (state_lru_h[l], state_lru_conv[l], state_gdn_conv[l], state_gdn_s[l], state_hgrn_s[l], state_ret_s[l])
        xs, ss = _layer(xs, cache_mem_k[l], cache_mem_v[l], pos_s, lb_all[l], st_s, p)
        new_p.append(sp)
        new_s.append(ss)
        mem_k_p.append(mk)
        mem_v_p.append(mv)
    return (xp, xs,
            jnp.stack([s[0] for s in new_p]), jnp.stack([s[1] for s in new_p]), jnp.stack([s[2] for s in new_p]),
            jnp.stack([s[3] for s in new_p]), jnp.stack([s[4] for s in new_p]), jnp.stack([s[5] for s in new_p]),
            jnp.stack(mem_k_p), jnp.stack(mem_v_p),
            jnp.stack([s[0] for s in new_s]), jnp.stack([s[1] for s in new_s]), jnp.stack([s[2] for s in new_s]),
            jnp.stack([s[3] for s in new_s]), jnp.stack([s[4] for s in new_s]), jnp.stack([s[5] for s in new_s]))
```

```python
import functools
import math

import numpy as np
import jax
import jax.numpy as jnp
from jax import lax
from jax.experimental import pallas as pl
from jax.experimental.pallas import tpu as pltpu

f32 = jnp.float32
bf16 = jnp.bfloat16

N_HEADS = 4
HEAD_DIM = 128
W_BR = N_HEADS * HEAD_DIM
CONV_W = 4
N_BR = 5
LRU_C = 8.0
GDN_CHUNK = 64
HGRN_CHUNK = 16
ROPE_BASE = 10000.0
LN_EPS = 1e-5
NORM_EPS = 1e-6
PAST_LEN = 16384

T_TILE = 256
ROW_TILE = 512
SAMPLE_BLOCK = 8
CARRY_ROWS = 8
VMEM_LIMIT = 52 * 1024 * 1024


def _mm(a, b):
    return jnp.dot(a.astype(bf16), b.astype(bf16), preferred_element_type=f32)


def _mm_nt(a, b):
    return lax.dot_general(a.astype(bf16), b.astype(bf16), (((1,), (1,)), ((), ())), preferred_element_type=f32)


def _mm_tn(a, b):
    return lax.dot_general(a.astype(bf16), b.astype(bf16), (((0,), (0,)), ((), ())), preferred_element_type=f32)


def _mask_mm_exact(mask_bf16, x):
    hi = x.astype(bf16)
    r1 = x - hi.astype(f32)
    mid = r1.astype(bf16)
    lo = (r1 - mid.astype(f32)).astype(bf16)
    dot = functools.partial(jnp.dot, preferred_element_type=f32)
    return dot(mask_bf16, hi) + dot(mask_bf16, mid) + dot(mask_bf16, lo)


def _chunk_masks(n, chunk):
    shift = int(math.log2(chunk))
    row = lax.broadcasted_iota(jnp.int32, (n, n), 0)
    col = lax.broadcasted_iota(jnp.int32, (n, n), 1)
    same = (row >> shift) == (col >> shift)
    return jnp.logical_and(same, col <= row), same


def _as_bf16(mask):
    return jnp.where(mask, 1.0, 0.0).astype(bf16)


def _sigmoid(x):
    return jax.nn.sigmoid(x)


def _silu(x):
    return x * jax.nn.sigmoid(x)


def _layernorm(x, g, b):
    mu = jnp.mean(x, -1, keepdims=True)
    xc = x - mu
    var = jnp.mean(xc * xc, -1, keepdims=True)
    return xc * lax.rsqrt(var + LN_EPS) * g + b


def _head_slices(a):
    return [a[:, h * HEAD_DIM:(h + 1) * HEAD_DIM] for h in range(N_HEADS)]


def _conv_from_buffer(buf_ref, x, cw, n):
    buf_ref[CARRY_ROWS:CARRY_ROWS + n, :] = x
    out = cw[CONV_W - 1:CONV_W, :] * x
    for j in range(CONV_W - 1):
        off = CARRY_ROWS - (CONV_W - 1) + j
        out = out + cw[j:j + 1, :] * buf_ref[off:off + n, :]
    return out


def _lru_gates(xc, wa_ref, ba, wi_ref, bi, lam):
    xh = _head_slices(xc)
    r = _sigmoid(jnp.concatenate([_mm(xh[h], wa_ref[h]) for h in range(N_HEADS)], -1) + ba)
    ig = _sigmoid(jnp.concatenate([_mm(xh[h], wi_ref[h]) for h in range(N_HEADS)], -1) + bi)
    log_a = -LRU_C * r * jax.nn.softplus(-lam)
    a = jnp.exp(log_a)
    b = jnp.sqrt(jnp.tanh(-log_a) * (1.0 + a * a)) * (ig * xc)
    return a, b


def _gdn_gates(ba, alog_row, dtb_row):
    beta = _sigmoid(ba)
    g = -jnp.exp(alog_row) * jax.nn.softplus(ba + dtb_row)
    return beta, g


def _l2norm(a):
    return a * lax.rsqrt(jnp.sum(a * a, -1, keepdims=True) + NORM_EPS)


def _rmsnorm_heads(o_heads, w):
    parts = [o * lax.rsqrt(jnp.mean(o * o, -1, keepdims=True) + NORM_EPS) for o in o_heads]
    return jnp.concatenate(parts, -1) * w


def _groupnorm_heads(o_heads, w, b):
    parts = []
    for o in o_heads:
        mu = jnp.mean(o, -1, keepdims=True)
        oc = o - mu
        var = jnp.mean(oc * oc, -1, keepdims=True)
        parts.append(oc * lax.rsqrt(var + NORM_EPS))
    return jnp.concatenate(parts, -1) * w + b


def _hgrn_lower_bound(lbraw, layer):
    n = lbraw.shape[0]
    rows = [lbraw[i:i + 1, :] for i in range(n)]
    m = rows[0]
    for r in rows[1:]:
        m = jnp.maximum(m, r)
    es = [jnp.exp(r - m) for r in rows]
    tot = es[0]
    for e in es[1:]:
        tot = tot + e
    acc = jnp.zeros_like(m)
    for i in range(1, layer + 1):
        acc = acc + es[i] / tot
    return acc


def _hgrn_gates(fz, lb):
    log_f = jnp.logaddexp(jnp.log(lb), jnp.log1p(-lb) + jax.nn.log_sigmoid(fz))
    k = (1.0 - lb) * _sigmoid(-fz)
    return log_f, k


def _rotary(x, cos, sin_signed):
    return x * cos + pltpu.roll(x, HEAD_DIM // 2, 1) * sin_signed


def _ret_log_gamma(h):
    return math.log1p(-(2.0 ** (-5.0 - h)))


def _rope_kernel(ang_ref, cos_ref, sin_ref):
    ang = ang_ref[...]
    lane = lax.broadcasted_iota(jnp.int32, ang.shape, 1)
    s = jnp.sin(ang)
    cos_ref[...] = jnp.cos(ang)
    sin_ref[...] = jnp.where(lane < HEAD_DIM // 2, -s, s)


def _rope_tables(positions):
    half = HEAD_DIM // 2
    inv = ROPE_BASE ** (-np.arange(half, dtype=np.float64) / half)
    ang = np.mod(np.asarray(positions, np.float64)[:, None] * inv[None, :], 2.0 * np.pi)
    ang = np.concatenate([ang, ang], axis=1).astype(np.float32)
    n = ang.shape[0]
    pad = (-n) % 8
    ang = np.pad(ang, ((0, pad), (0, 0)))
    shp = jax.ShapeDtypeStruct(ang.shape, f32)
    cos, sin = pl.pallas_call(_rope_kernel, out_shape=(shp, shp), name="rope_tables")(jnp.asarray(ang))
    return cos[:n], sin[:n]


def _memkv_kernel(m_ref, wk_ref, wv_ref, k_ref, v_ref, kb_ref, vb_ref):
    m = m_ref[...]
    k = _mm(m, wk_ref[...])
    v = _mm(m, wv_ref[...])
    k_ref[...] = k
    v_ref[...] = v
    kb_ref[...] = k.astype(bf16)
    vb_ref[...] = v.astype(bf16)


def _memkv(mem2d, wk, wv):
    rows, d = mem2d.shape
    tile = 256
    row_spec = lambda w: pl.BlockSpec((tile, w), lambda i: (i, 0))
    full = lambda a: pl.BlockSpec(a.shape, lambda i: (0,) * a.ndim)
    return pl.pallas_call(
        _memkv_kernel,
        grid=(rows // tile,),
        in_specs=[row_spec(d), full(wk), full(wv)],
        out_specs=[row_spec(W_BR)] * 4,
        out_shape=[jax.ShapeDtypeStruct((rows, W_BR), f32)] * 2 + [jax.ShapeDtypeStruct((rows, W_BR), bf16)] * 2,
        compiler_params=pltpu.CompilerParams(dimension_semantics=("arbitrary",), vmem_limit_bytes=VMEM_LIMIT),
        name="mem_kv",
    )(mem2d, wk, wv)


def _prompt_call(kernel, name, x, consts, per_batch, per_time, outs_tile, outs_batch, scratch):
    B, T, D = x.shape
    nt = T // T_TILE
    in_specs = [pl.BlockSpec((1, T_TILE, D), lambda b, t: (b, t, 0))]
    for a in consts:
        in_specs.append(pl.BlockSpec(a.shape, lambda b, t, n=a.ndim: (0,) * n))
    for a in per_batch:
        in_specs.append(pl.BlockSpec((1,) + a.shape[1:], lambda b, t, n=a.ndim: (b,) + (0,) * (n - 1)))
    for a in per_time:
        in_specs.append(pl.BlockSpec((T_TILE,) + a.shape[1:], lambda b, t, n=a.ndim: (t,) + (0,) * (n - 1)))
    out_specs, out_shape = [], []
    for width, dt in outs_tile:
        out_specs.append(pl.BlockSpec((1, T_TILE, width), lambda b, t: (b, t, 0)))
        out_shape.append(jax.ShapeDtypeStruct((B, T, width), dt))
    for shp, dt in outs_batch:
        out_specs.append(pl.BlockSpec((1,) + shp, lambda b, t, n=len(shp): (b,) + (0,) * n))
        out_shape.append(jax.ShapeDtypeStruct((B,) + shp, dt))
    return pl.pallas_call(
        kernel,
        grid=(B, nt),
        in_specs=in_specs,
        out_specs=out_specs,
        out_shape=out_shape,
        scratch_shapes=scratch,
        compiler_params=pltpu.CompilerParams(dimension_semantics=("arbitrary", "arbitrary"),
                                             vmem_limit_bytes=VMEM_LIMIT),
        name=name,
    )(x, *consts, *per_batch, *per_time)


def _lru_prompt_kernel(x_ref, w_ref, cw_ref, cb_ref, wa_ref, ba_ref, wi_ref, bi_ref, lam_ref,
                       y_ref, h_ref, conv_ref, buf_ref, a_ref, b_ref, hcar_ref):
    n = T_TILE

    @pl.when(pl.program_id(1) == 0)
    def _():
        buf_ref[0:CARRY_ROWS, :] = jnp.zeros((CARRY_ROWS, W_BR), f32)
        hcar_ref[...] = jnp.zeros_like(hcar_ref)

    xa = _mm(x_ref[0], w_ref[...])
    xc = _conv_from_buffer(buf_ref, xa, cw_ref[...], n) + cb_ref[...]
    a, b = _lru_gates(xc, wa_ref, ba_ref[...], wi_ref, bi_ref[...], lam_ref[...])

    row8 = lax.broadcasted_iota(jnp.int32, (n, W_BR), 0) & 7
    for d in (1, 2, 4):
        keep = row8 >= d
        a_prev = jnp.where(keep, pltpu.roll(a, d, 0), 1.0)
        b_prev = jnp.where(keep, pltpu.roll(b, d, 0), 0.0)
        b = b + a * b_prev
        a = a * a_prev
    a_ref[...] = a
    b_ref[...] = b

    def group(gi, h_prev):
        i = pl.multiple_of(gi * 8, 8)
        blk = b_ref[pl.ds(i, 8), :] + a_ref[pl.ds(i, 8), :] * h_prev
        b_ref[pl.ds(i, 8), :] = blk
        return jnp.broadcast_to(blk[7:8, :], (8, W_BR))

    h_last = lax.fori_loop(0, n // 8, group, hcar_ref[...])
    hcar_ref[...] = h_last
    y_ref[0] = b_ref[...].astype(y_ref.dtype)
    h_ref[0] = h_last[0:1, :]
    conv_ref[0] = buf_ref[CARRY_ROWS + n - (CONV_W - 1):CARRY_ROWS + n, :]
    buf_ref[0:CARRY_ROWS, :] = buf_ref[n:n + CARRY_ROWS, :]


def _gdn_prompt_kernel(x_ref, wqkv_ref, wz_ref, wba_ref, cw_ref, alog_ref, dtb_ref, nw_ref,
                       y_ref, conv_ref, s_ref, buf_ref, st_ref):
    n, C = T_TILE, GDN_CHUNK

    @pl.when(pl.program_id(1) == 0)
    def _():
        buf_ref[0:CARRY_ROWS, :] = jnp.zeros((CARRY_ROWS, 3 * W_BR), f32)
        st_ref[...] = jnp.zeros_like(st_ref)

    x = x_ref[0].astype(bf16)
    qkv = _mm(x, wqkv_ref[...])
    c = _silu(_conv_from_buffer(buf_ref, qkv, cw_ref[...], n))
    z = _mm(x, wz_ref[...])
    beta, g = _gdn_gates(_mm(x, wba_ref[...]), alog_ref[...], dtb_ref[...])

    incl_t, _ = _chunk_masks(n, C)
    G = _mask_mm_exact(_as_bf16(incl_t), g)
    GT = G.T
    expG = jnp.exp(G)
    ci = lax.broadcasted_iota(jnp.int32, (C, C), 0)
    cj = lax.broadcasted_iota(jnp.int32, (C, C), 1)
    incl, strict = cj <= ci, cj < ci

    o_heads = []
    for h in range(N_HEADS):
        q = _l2norm(c[:, h * HEAD_DIM:(h + 1) * HEAD_DIM]) * HEAD_DIM ** -0.5
        k = _l2norm(c[:, W_BR + h * HEAD_DIM:W_BR + (h + 1) * HEAD_DIM])
        v = c[:, 2 * W_BR + h * HEAD_DIM:2 * W_BR + (h + 1) * HEAD_DIM]
        bcol = beta[:, h:h + 1]
        gcol = G[:, N_HEADS + h:N_HEADS + h + 1]
        egcol = expG[:, N_HEADS + h:N_HEADS + h + 1]
        S = st_ref[h]
        o_chunks = []
        for ck in range(n // C):
            sl = slice(ck * C, (ck + 1) * C)
            qc, kc, vc = q[sl], k[sl], v[sl]
            gc = gcol[sl]
            decay = jnp.where(incl, jnp.exp(gc - GT[N_HEADS + h:N_HEADS + h + 1, sl]), 0.0)
            kb = kc * bcol[sl]
            a_kk = jnp.where(strict, _mm_nt(kb, kc) * decay, 0.0)
            m = -a_kk
            t_off = m
            for _ in range(int(math.log2(C)) - 1):
                m = _mm(m, m)
                t_off = t_off + m + _mm(t_off, m)
            rhs = jnp.concatenate([vc * bcol[sl], kb * egcol[sl]], -1)
            sol = rhs + _mm(t_off, rhs)
            u, w = sol[:, :HEAD_DIM], sol[:, HEAD_DIM:]
            a_qk = _mm_nt(qc, kc) * decay
            v_new = u - _mm(w, S)
            o_chunks.append(_mm(qc * egcol[sl], S) + _mm(a_qk, v_new))
            g_last = gc[C - 1:C, :]
            S = jnp.exp(g_last) * S + _mm_tn(kc * jnp.exp(g_last - gc), v_new)
        st_ref[h] = S
        s_ref[0, h] = S
        o_heads.append(jnp.concatenate(o_chunks, 0))

    y = _rmsnorm_heads(o_heads, nw_ref[...]) * _silu(z)
    y_ref[0] = y.astype(y_ref.dtype)
    conv_ref[0] = buf_ref[CARRY_ROWS + n - (CONV_W - 1):CARRY_ROWS + n, :]
    buf_ref[0:CARRY_ROWS, :] = buf_ref[n:n + CARRY_ROWS, :]


def _hgrn_prompt_kernel(layer, x_ref, w_ref, lbraw_ref, nw_ref, y_ref, s_ref, st_ref):
    n, C = T_TILE, HGRN_CHUNK

    @pl.when(pl.program_id(1) == 0)
    def _():
        st_ref[...] = jnp.zeros_like(st_ref)

    u = _mm(x_ref[0], w_ref[...])
    hq, fz, hi, hg = (u[:, i * W_BR:(i + 1) * W_BR] for i in range(4))
    lb = _hgrn_lower_bound(lbraw_ref[...], layer)
    log_f, k = _hgrn_gates(fz, lb)
    q = _silu(hq)

    incl, same = _chunk_masks(n, C)
    G = _mask_mm_exact(_as_bf16(incl), log_f)
    Gtot = _mask_mm_exact(_as_bf16(same), log_f)
    q_g = q * jnp.exp(G)
    k_g = k * jnp.exp(-G)
    k_end = k * jnp.exp(Gtot - G)
    dT = jnp.exp(Gtot).T

    o_heads = []
    for h in range(N_HEADS):
        hs = slice(h * HEAD_DIM, (h + 1) * HEAD_DIM)
        a_qk = jnp.where(incl, _mm_nt(q_g[:, hs], k_g[:, hs]), 0.0)
        o_intra = _mm(a_qk, hi[:, hs])
        S = st_ref[h]
        o_inter = []
        for ck in range(n // C):
            sl = slice(ck * C, (ck + 1) * C)
            o_inter.append(_mm(q_g[sl, hs], S))
            S = dT[hs, ck * C:ck * C + 1] * S + _mm_tn(k_end[sl, hs], hi[sl, hs])
        st_ref[h] = S
        s_ref[0, h] = S
        o_heads.append(o_intra + jnp.concatenate(o_inter, 0))

    y = _rmsnorm_heads(o_heads, nw_ref[...]) * _sigmoid(hg)
    y_ref[0] = y.astype(y_ref.dtype)


def _ret_prompt_kernel(x_ref, w_ref, gw_ref, gb_ref, cos_ref, sin_ref, y_ref, s_ref, st_ref):
    n = T_TILE

    @pl.when(pl.program_id(1) == 0)
    def _():
        st_ref[...] = jnp.zeros_like(st_ref)

    u = _mm(x_ref[0], w_ref[...])
    rq, rk, rv, rg = (u[:, i * W_BR:(i + 1) * W_BR] for i in range(4))
    cos, sin = cos_ref[...], sin_ref[...]
    ri = lax.broadcasted_iota(jnp.int32, (n, n), 0)
    rj = lax.broadcasted_iota(jnp.int32, (n, n), 1)
    dist = (ri - rj).astype(f32)
    pos = lax.broadcasted_iota(jnp.int32, (n, 1), 0).astype(f32)

    o_heads = []
    for h in range(N_HEADS):
        hs = slice(h * HEAD_DIM, (h + 1) * HEAD_DIM)
        lg = _ret_log_gamma(h)
        q = _rotary(rq[:, hs], cos, sin)
        k = _rotary(rk[:, hs], cos, sin) * HEAD_DIM ** -0.5
        v = rv[:, hs]
        decay = jnp.where(rj <= ri, jnp.exp(dist * lg), 0.0)
        S = st_ref[h]
        o = _mm(_mm_nt(q, k) * decay, v) + _mm(q * jnp.exp((pos + 1.0) * lg), S)
        S = math.exp(n * lg) * S + _mm_tn(k * jnp.exp((n - 1.0 - pos) * lg), v)
        st_ref[h] = S
        s_ref[0, h] = S
        o_heads.append(o)

    y = _groupnorm_heads(o_heads, gw_ref[...], gb_ref[...]) * _silu(rg)
    y_ref[0] = y.astype(y_ref.dtype)


def _xattn_prompt_kernel(x_ref, w_ref, mk_ref, mv_ref, y_ref):
    q = _mm(x_ref[0], w_ref[...])
    mk, mv = mk_ref[0], mv_ref[0]
    outs = []
    for h in range(N_HEADS):
        hs = slice(h * HEAD_DIM, (h + 1) * HEAD_DIM)
        s = _mm_nt(q[:, hs], mk[:, hs]) * HEAD_DIM ** -0.5
        e = jnp.exp(s - jnp.max(s, -1, keepdims=True))
        p = e / jnp.sum(e, -1, keepdims=True)
        outs.append(_mm(p, mv[:, hs]))
    y_ref[0] = jnp.concatenate(outs, -1).astype(y_ref.dtype)


def _merge_kernel(alpha, x_ref, ya_ref, yb_ref, yc_ref, yd_ref, ye_ref, wg_ref, bg_ref, wb_ref, wo_ref,
                  g_ref, b_ref, o_ref):
    x = x_ref[...]
    xb = x.astype(bf16)
    merged = None
    for i, y_ref in enumerate((ya_ref, yb_ref, yc_ref, yd_ref, ye_ref)):
        gate = _sigmoid(_mm(xb, wg_ref[i]) + bg_ref[i])
        term = gate * _mm(y_ref[...], wb_ref[i])
        merged = term if merged is None else merged + term
    mix = _mm(merged, wo_ref[...])
    o_ref[...] = _layernorm(alpha * x + mix, g_ref[...], b_ref[...])


def _ffn_kernel(alpha, n_split, x_ref, wg_ref, wv_ref, wd_ref, g_ref, b_ref, o_ref):
    x = x_ref[...]
    xb = x.astype(bf16)
    d_ff = wg_ref.shape[1]
    step = d_ff // n_split
    ffn = None
    for i in range(n_split):
        cs = slice(i * step, (i + 1) * step)
        hidden = _silu(_mm(xb, wg_ref[:, cs])) * _mm(xb, wv_ref[:, cs])
        part = _mm(hidden, wd_ref[cs, :])
        ffn = part if ffn is None else ffn + part
    o_ref[...] = _layernorm(alpha * x + ffn, g_ref[...], b_ref[...])


def _row_call(kernel, name, row_ops, consts, d_out):
    rows = row_ops[0].shape[0]
    tile = min(ROW_TILE, rows)
    in_specs = [pl.BlockSpec((tile, a.shape[1]), lambda i: (i, 0)) for a in row_ops]
    in_specs += [pl.BlockSpec(a.shape, lambda i, n=a.ndim: (0,) * n, pipeline_mode=pl.Buffered(1)) for a in consts]
    return pl.pallas_call(
        kernel,
        grid=(rows // tile,),
        in_specs=in_specs,
        out_specs=pl.BlockSpec((tile, d_out), lambda i: (i, 0)),
        out_shape=jax.ShapeDtypeStruct((rows, d_out), f32),
        compiler_params=pltpu.CompilerParams(dimension_semantics=("arbitrary",), vmem_limit_bytes=VMEM_LIMIT),
        name=name,
    )(*row_ops, *consts)


def _proj_kernel(x_ref, w_ref, o_ref):
    o_ref[...] = _mm(x_ref[...], w_ref[...])


def _sample_proj(x, w, n_tile):
    rows, d = x.shape
    n = w.shape[1]
    return pl.pallas_call(
        _proj_kernel,
        grid=(n // n_tile,),
        in_specs=[pl.BlockSpec((rows, d), lambda j: (0, 0)), pl.BlockSpec((d, n_tile), lambda j: (0, j))],
        out_specs=pl.BlockSpec((rows, n_tile), lambda j: (0, j)),
        out_shape=jax.ShapeDtypeStruct((rows, n), f32),
        compiler_params=pltpu.CompilerParams(dimension_semantics=("arbitrary",), vmem_limit_bytes=VMEM_LIMIT),
        name="sample_proj",
    )(x, w)


def _columns(rows_list):
    nb = rows_list[0].shape[0]
    out = []
    for b in range(nb):
        tile = jnp.concatenate([r[b:b + 1, :] for r in rows_list]
                               + [jnp.zeros((8 - len(rows_list), HEAD_DIM), f32)], 0)
        out.append(tile.T)
    return out


def _sample_kernel(layer,
                   u_ref, h0_ref, lconv_ref, gconv_ref, sg_ref, sh_ref, sr_ref, ck_ref, cv_ref,
                   lcw_ref, lcb_ref, lwa_ref, lba_ref, lwi_ref, lbi_ref, lam_ref,
                   gcw_ref, alog_ref, dtb_ref, gnw_ref, lbraw_ref, hnw_ref, rgw_ref, rgb_ref, cos_ref, sin_ref,
                   ya_ref, yb_ref, yc_ref, yd_ref, ye_ref, h1_ref, lconv_o_ref, gconv_o_ref,
                   sg_o_ref, sh_o_ref, sr_o_ref):
    nb = SAMPLE_BLOCK
    W = W_BR
    u = u_ref[...]
    off = 0

    def take(width):
        nonlocal off
        part = u[:, off:off + width]
        off += width
        return part

    xa, qkv, z = take(W), take(3 * W), take(W)
    hq, fz, hi, hg = take(W), take(W), take(W), take(W)
    rq, rk, rv, rg = take(W), take(W), take(W), take(W)
    xq, ba = take(W), take(HEAD_DIM)

    lcw = lcw_ref[...]
    xc = lcw[CONV_W - 1:CONV_W, :] * xa + lcb_ref[...]
    for j in range(CONV_W - 1):
        xc = xc + lcw[j:j + 1, :] * lconv_ref[j]
    for j in range(CONV_W - 2):
        lconv_o_ref[j] = lconv_ref[j + 1]
    lconv_o_ref[CONV_W - 2] = xa
    a, b = _lru_gates(xc, lwa_ref, lba_ref[...], lwi_ref, lbi_ref[...], lam_ref[...])
    h1 = a * h0_ref[...] + b
    h1_ref[...] = h1
    ya_ref[...] = h1

    gcw = gcw_ref[...]
    c = gcw[CONV_W - 1:CONV_W, :] * qkv
    for j in range(CONV_W - 1):
        c = c + gcw[j:j + 1, :] * gconv_ref[j]
    for j in range(CONV_W - 2):
        gconv_o_ref[j] = gconv_ref[j + 1]
    gconv_o_ref[CONV_W - 2] = qkv
    c = _silu(c)
    beta, g = _gdn_gates(ba, alog_ref[...], dtb_ref[...])
    eg = jnp.exp(g)
    o_rows = [[None] * N_HEADS for _ in range(nb)]
    for h in range(N_HEADS):
        q = _l2norm(c[:, h * HEAD_DIM:(h + 1) * HEAD_DIM]) * HEAD_DIM ** -0.5
        k = _l2norm(c[:, W + h * HEAD_DIM:W + (h + 1) * HEAD_DIM])
        v = c[:, 2 * W + h * HEAD_DIM:2 * W + (h + 1) * HEAD_DIM]
        cols = _columns([q, k])
        for bi in range(nb):
            qcol, kcol = cols[bi][:, 0:1], cols[bi][:, 1:2]
            beta_s = beta[bi:bi + 1, h:h + 1]
            eg_s = eg[bi:bi + 1, N_HEADS + h:N_HEADS + h + 1]
            S0 = sg_ref[bi, h]
            kS = jnp.sum(kcol * S0, axis=0, keepdims=True)
            v_new = beta_s * (v[bi:bi + 1, :] - eg_s * kS)
            S1 = eg_s * S0 + kcol * v_new
            sg_o_ref[bi, h] = S1
            o_rows[bi][h] = jnp.sum(qcol * S1, axis=0, keepdims=True)
    o = [jnp.concatenate([o_rows[bi][h] for bi in range(nb)], 0) for h in range(N_HEADS)]
    yb_ref[...] = _rmsnorm_heads(o, gnw_ref[...]) * _silu(z)

    lb = _hgrn_lower_bound(lbraw_ref[...], layer)
    log_f, kk = _hgrn_gates(fz, lb)
    f = jnp.exp(log_f)
    qs = _silu(hq)
    for h in range(N_HEADS):
        hs = slice(h * HEAD_DIM, (h + 1) * HEAD_DIM)
        cols = _columns([qs[:, hs], kk[:, hs], f[:, hs]])
        for bi in range(nb):
            qcol, kcol, fcol = cols[bi][:, 0:1], cols[bi][:, 1:2], cols[bi][:, 2:3]
            S1 = fcol * sh_ref[bi, h] + kcol * hi[bi:bi + 1, hs]
            sh_o_ref[bi, h] = S1
            o_rows[bi][h] = jnp.sum(qcol * S1, axis=0, keepdims=True)
    o = [jnp.concatenate([o_rows[bi][h] for bi in range(nb)], 0) for h in range(N_HEADS)]
    yc_ref[...] = _rmsnorm_heads(o, hnw_ref[...]) * _sigmoid(hg)

    cos, sin = cos_ref[...], sin_ref[...]
    for h in range(N_HEADS):
        hs = slice(h * HEAD_DIM, (h + 1) * HEAD_DIM)
        gamma = math.exp(_ret_log_gamma(h))
        q = _rotary(rq[:, hs], cos, sin)
        k = _rotary(rk[:, hs], cos, sin) * HEAD_DIM ** -0.5
        cols = _columns([q, k])
        for bi in range(nb):
            qcol, kcol = cols[bi][:, 0:1], cols[bi][:, 1:2]
            S1 = gamma * sr_ref[bi, h] + kcol * rv[bi:bi + 1, hs]
            sr_o_ref[bi, h] = S1
            o_rows[bi][h] = jnp.sum(qcol * S1, axis=0, keepdims=True)
    o = [jnp.concatenate([o_rows[bi][h] for bi in range(nb)], 0) for h in range(N_HEADS)]
    yd_ref[...] = _groupnorm_heads(o, rgw_ref[...], rgb_ref[...]) * _silu(rg)

    hrow = lax.broadcasted_iota(jnp.int32, (8, W), 0)
    hcol = lax.broadcasted_iota(jnp.int32, (8, W), 1) >> int(math.log2(HEAD_DIM))
    head_mask = hrow == hcol
    e_rows = []
    for bi in range(nb):
        qblk = jnp.where(head_mask, jnp.broadcast_to(xq[bi:bi + 1, :], (8, W)), 0.0)
        s = _mm_nt(qblk, ck_ref[bi]) * HEAD_DIM ** -0.5
        e = jnp.exp(s - jnp.max(s, -1, keepdims=True))
        p = e / jnp.sum(e, -1, keepdims=True)
        of = _mm(p, cv_ref[bi])
        e_rows.append(jnp.sum(jnp.where(head_mask, of, 0.0), axis=0, keepdims=True))
    ye_ref[...] = jnp.concatenate(e_rows, 0)


def _sample_branches(layer, u, h0, lconv_t, gconv_t, sg, sh, sr, ck, cv, params):
    nb = SAMPLE_BLOCK
    rows = u.shape[0]
    blk2 = lambda a: pl.BlockSpec((nb, a.shape[1]), lambda i: (i, 0))
    blk_t = lambda a: pl.BlockSpec((a.shape[0], nb, a.shape[2]), lambda i: (0, i, 0))
    blk_lead = lambda a: pl.BlockSpec((nb,) + a.shape[1:], lambda i, n=a.ndim: (i,) + (0,) * (n - 1))
    full = lambda a: pl.BlockSpec(a.shape, lambda i, n=a.ndim: (0,) * n)
    sds = lambda a: jax.ShapeDtypeStruct(a.shape, f32)
    y_shape = jax.ShapeDtypeStruct((rows, W_BR), f32)
    y_spec = pl.BlockSpec((nb, W_BR), lambda i: (i, 0))
    return pl.pallas_call(
        functools.partial(_sample_kernel, layer),
        grid=(rows // nb,),
        in_specs=[blk2(u), blk2(h0), blk_t(lconv_t), blk_t(gconv_t), blk_lead(sg), blk_lead(sh), blk_lead(sr),
                  blk_lead(ck), blk_lead(cv)] + [full(p) for p in params],
        out_specs=[y_spec] * 5 + [blk2(h0), blk_t(lconv_t), blk_t(gconv_t), blk_lead(sg), blk_lead(sh), blk_lead(sr)],
        out_shape=[y_shape] * 5 + [sds(h0), sds(lconv_t), sds(gconv_t), sds(sg), sds(sh), sds(sr)],
        compiler_params=pltpu.CompilerParams(dimension_semantics=("arbitrary",), vmem_limit_bytes=VMEM_LIMIT),
        name="sample_branches",
    )(u, h0, lconv_t, gconv_t, sg, sh, sr, ck, cv, *params)


def kernel(x_prompt, x_sample, mem_prompt, state_lru_h, state_lru_conv, state_gdn_conv, state_gdn_s, state_hgrn_s, state_ret_s, cache_mem_k, cache_mem_v, w_in, lru_conv_w, lru_conv_b, lru_wa, lru_ba, lru_wi, lru_bi, lru_lambda, gdn_conv_w, gdn_a_log, gdn_dt_bias, gdn_norm_w, hgrn_lb_raw, hgrn_norm_w, ret_gn_w, ret_gn_b, w_mem_k, w_mem_v, w_merge_gate, b_merge_gate, w_branch, w_out, ln1_g, ln1_b, w_ffn_up, w_ffn_down, ln2_g, ln2_b):
    B, T, D = x_prompt.shape
    Bs = x_sample.shape[0]
    depth = w_in.shape[0]
    n_mem = mem_prompt.shape[1]
    W, H, HD = W_BR, N_HEADS, HEAD_DIM
    d_ff = w_ffn_down.shape[1]
    alpha = (2 * depth) ** 0.25
    assert T % T_TILE == 0 and Bs % SAMPLE_BLOCK == 0 and (B * T) % ROW_TILE == 0

    cos_all, sin_all = _rope_tables(list(range(T)) + [PAST_LEN])
    cos_p, sin_p = cos_all[:T], sin_all[:T]
    cos_s, sin_s = cos_all[T:T + 1], sin_all[T:T + 1]

    row = lambda a: a.reshape(1, -1)
    state_spec = ((H, HD, HD), f32)
    state_scratch = pltpu.VMEM((H, HD, HD), f32)

    xp = x_prompt
    xs = x_sample.reshape(Bs, D)
    mem2d = mem_prompt.reshape(B * n_mem, D)
    outs_p = [[] for _ in range(8)]
    outs_s = [[] for _ in range(6)]

    for l in range(depth):
        wl = w_in[l]
        o = 0
        pieces = []
        for width in (W, 3 * W, W, H, H, W, W, W, W, W, W, W, W, W):
            pieces.append(wl[:, o:o + width])
            o += width
        (w_xa, w_qkv, w_z, w_gb, w_ga, w_hq, w_hf, w_hi, w_hg, w_rq, w_rk, w_rv, w_rg, w_xq) = pieces
        w_ba = jnp.concatenate([w_gb, w_ga, jnp.zeros((D, HD - 2 * H), f32)], 1).astype(bf16)
        w_xa, w_qkv, w_z, w_xq = (a.astype(bf16) for a in (w_xa, w_qkv, w_z, w_xq))
        w_hgrn = jnp.concatenate([w_hq, w_hf, w_hi, w_hg], 1).astype(bf16)
        w_ret = jnp.concatenate([w_rq, w_rk, w_rv, w_rg], 1).astype(bf16)
        w_cat = jnp.concatenate([w_xa, w_qkv, w_z, w_hgrn, w_ret, w_xq, w_ba], 1)

        lru_params = (lru_conv_w[l], row(lru_conv_b[l]), lru_wa[l].astype(bf16), row(lru_ba[l]),
                      lru_wi[l].astype(bf16), row(lru_bi[l]), row(lru_lambda[l]))
        pad_lanes = lambda a, start: jnp.zeros((1, HD), f32).at[0, start:start + H].set(a)
        alog_row, dtb_row = pad_lanes(gdn_a_log[l], H), pad_lanes(gdn_dt_bias[l], H)
        gdn_params = (gdn_conv_w[l], alog_row, dtb_row, row(gdn_norm_w[l]))
        hgrn_params = (hgrn_lb_raw, row(hgrn_norm_w[l]))
        ret_params = (row(ret_gn_w[l]), row(ret_gn_b[l]))
        merge_consts = (w_merge_gate[l].astype(bf16), b_merge_gate[l].reshape(N_BR, 1, D), w_branch[l].astype(bf16),
                        w_out[l].astype(bf16), row(ln1_g[l]), row(ln1_b[l]))
        ffn_consts = (w_ffn_up[l][:, :d_ff].astype(bf16), w_ffn_up[l][:, d_ff:].astype(bf16),
                      w_ffn_down[l].astype(bf16), row(ln2_g[l]), row(ln2_b[l]))
        merge_k = functools.partial(_merge_kernel, alpha)
        ffn_k = functools.partial(_ffn_kernel, alpha, 2)

        mk, mv, mkb, mvb = _memkv(mem2d, w_mem_k[l].astype(bf16), w_mem_v[l].astype(bf16))
        ya, h_last, lconv = _prompt_call(
            _lru_prompt_kernel, "lru_prompt", xp, (w_xa,) + lru_params, (), (),
            [(W, bf16)], [((1, W), f32), ((CONV_W - 1, W), f32)],
            [pltpu.VMEM((T_TILE + CARRY_ROWS, W), f32), pltpu.VMEM((T_TILE, W), f32), pltpu.VMEM((T_TILE, W), f32),
             pltpu.VMEM((8, W), f32)])
        yb, gconv, s_gdn = _prompt_call(
            _gdn_prompt_kernel, "gdn_prompt", xp, (w_qkv, w_z, w_ba) + gdn_params, (), (),
            [(W, bf16)], [((CONV_W - 1, 3 * W), f32), state_spec],
            [pltpu.VMEM((T_TILE + CARRY_ROWS, 3 * W), f32), state_scratch])
        yc, s_hgrn = _prompt_call(
            functools.partial(_hgrn_prompt_kernel, l), "hgrn_prompt", xp, (w_hgrn,) + hgrn_params, (), (),
            [(W, bf16)], [state_spec], [state_scratch])
        yd, s_ret = _prompt_call(
            _ret_prompt_kernel, "ret_prompt", xp, (w_ret,) + ret_params, (), (cos_p, sin_p),
            [(W, bf16)], [state_spec], [state_scratch])
        (ye,) = _prompt_call(
            _xattn_prompt_kernel, "xattn_prompt", xp, (w_xq,),
            (mkb.reshape(B, n_mem, W), mvb.reshape(B, n_mem, W)), (), [(W, bf16)], [], [])
        x2d = xp.reshape(B * T, D)
        ys = [y.reshape(B * T, W) for y in (ya, yb, yc, yd, ye)]
        x2d = _row_call(merge_k, "merge_prompt", [x2d] + ys, merge_consts, D)
        x2d = _row_call(ffn_k, "ffn_prompt", [x2d], ffn_consts, D)
        xp = x2d.reshape(B, T, D)
        for lst, val in zip(outs_p, (h_last.reshape(B, W), lconv, gconv, s_gdn, s_hgrn, s_ret,
                                     mk.reshape(B, n_mem, H, HD), mv.reshape(B, n_mem, H, HD))):
            lst.append(val)

        u_s = _sample_proj(xs, w_cat, w_cat.shape[1] // 3)
        sample_params = lru_params + gdn_params + hgrn_params + ret_params + (cos_s, sin_s)
        res = _sample_branches(
            l, u_s, state_lru_h[l], state_lru_conv[l].transpose(1, 0, 2), state_gdn_conv[l].transpose(1, 0, 2),
            state_gdn_s[l], state_hgrn_s[l], state_ret_s[l],
            cache_mem_k[l].reshape(Bs, n_mem, W), cache_mem_v[l].reshape(Bs, n_mem, W), sample_params)
        ys_s, (h1, lconv_s, gconv_s, sg1, sh1, sr1) = res[:5], res[5:]
        xs = _row_call(merge_k, "merge_sample", [xs] + list(ys_s), merge_consts, D)
        xs = _row_call(ffn_k, "ffn_sample", [xs], ffn_consts, D)
        for lst, val in zip(outs_s, (h1, lconv_s.transpose(1, 0, 2), gconv_s.transpose(1, 0, 2), sg1, sh1, sr1)):
            lst.append(val)

    return (xp, xs.reshape(Bs, 1, D), *(jnp.stack(v) for v in outs_p), *(jnp.stack(v) for v in outs_s))
```

```python
import functools
import math

import numpy as np
import jax
import jax.numpy as jnp
from jax import lax
from jax.experimental import pallas as pl
from jax.experimental.pallas import tpu as pltpu

f32 = jnp.float32
bf16 = jnp.bfloat16

N_HEADS = 4
HEAD_DIM = 128
W_BR = N_HEADS * HEAD_DIM
CONV_W = 4
N_BR = 5
LRU_C = 8.0
GDN_CHUNK = 64
HGRN_CHUNK = 16
ROPE_BASE = 10000.0
LN_EPS = 1e-5
NORM_EPS = 1e-6
PAST_LEN = 16384

T_TILE = 256
ROW_TILE = 512
SAMPLE_BLOCK = 8
CARRY_ROWS = 8
VMEM_LIMIT = 52 * 1024 * 1024


def _mm(a, b):
    return jnp.dot(a.astype(bf16), b.astype(bf16), preferred_element_type=f32)


def _mm_nt(a, b):
    return lax.dot_general(a.astype(bf16), b.astype(bf16), (((1,), (1,)), ((), ())), preferred_element_type=f32)


def _mm_tn(a, b):
    return lax.dot_general(a.astype(bf16), b.astype(bf16), (((0,), (0,)), ((), ())), preferred_element_type=f32)


def _mask_mm_exact(mask_bf16, x):
    hi = x.astype(bf16)
    r1 = x - hi.astype(f32)
    mid = r1.astype(bf16)
    lo = (r1 - mid.astype(f32)).astype(bf16)
    dot = functools.partial(jnp.dot, preferred_element_type=f32)
    return dot(mask_bf16, hi) + dot(mask_bf16, mid) + dot(mask_bf16, lo)


def _chunk_masks(n, chunk):
    shift = int(math.log2(chunk))
    row = lax.broadcasted_iota(jnp.int32, (n, n), 0)
    col = lax.broadcasted_iota(jnp.int32, (n, n), 1)
    same = (row >> shift) == (col >> shift)
    return jnp.logical_and(same, col <= row), same


def _as_bf16(mask):
    return jnp.where(mask, 1.0, 0.0).astype(bf16)


def _sigmoid(x):
    return jax.nn.sigmoid(x)


def _silu(x):
    return x * jax.nn.sigmoid(x)


def _layernorm(x, g, b):
    mu = jnp.mean(x, -1, keepdims=True)
    xc = x - mu
    var = jnp.mean(xc * xc, -1, keepdims=True)
    return xc * lax.rsqrt(var + LN_EPS) * g + b


def _head_slices(a):
    return [a[:, h * HEAD_DIM:(h + 1) * HEAD_DIM] for h in range(N_HEADS)]


def _conv_from_buffer(buf_ref, x, cw, n):
    buf_ref[CARRY_ROWS:CARRY_ROWS + n, :] = x
    out = cw[CONV_W - 1:CONV_W, :] * x
    for j in range(CONV_W - 1):
        off = CARRY_ROWS - (CONV_W - 1) + j
        out = out + cw[j:j + 1, :] * buf_ref[off:off + n, :]
    return out


def _lru_gates(xc, wa_ref, ba, wi_ref, bi, lam):
    xh = _head_slices(xc)
    r = _sigmoid(jnp.concatenate([_mm(xh[h], wa_ref[h]) for h in range(N_HEADS)], -1) + ba)
    ig = _sigmoid(jnp.concatenate([_mm(xh[h], wi_ref[h]) for h in range(N_HEADS)], -1) + bi)
    log_a = -LRU_C * r * jax.nn.softplus(-lam)
    a = jnp.exp(log_a)
    b = jnp.sqrt(jnp.tanh(-log_a) * (1.0 + a * a)) * (ig * xc)
    return a, b


def _gdn_gates(ba, alog_row, dtb_row):
    beta = _sigmoid(ba)
    g = -jnp.exp(alog_row) * jax.nn.softplus(ba + dtb_row)
    return beta, g


def _l2norm(a):
    return a * lax.rsqrt(jnp.sum(a * a, -1, keepdims=True) + NORM_EPS)


def _rmsnorm_heads(o_heads, w):
    parts = [o * lax.rsqrt(jnp.mean(o * o, -1, keepdims=True) + NORM_EPS) for o in o_heads]
    return jnp.concatenate(parts, -1) * w


def _groupnorm_heads(o_heads, w, b):
    parts = []
    for o in o_heads:
        mu = jnp.mean(o, -1, keepdims=True)
        oc = o - mu
        var = jnp.mean(oc * oc, -1, keepdims=True)
        parts.append(oc * lax.rsqrt(var + NORM_EPS))
    return jnp.concatenate(parts, -1) * w + b


def _hgrn_lower_bound(lbraw, layer):
    n = lbraw.shape[0]
    rows = [lbraw[i:i + 1, :] for i in range(n)]
    m = rows[0]
    for r in rows[1:]:
        m = jnp.maximum(m, r)
    es = [jnp.exp(r - m) for r in rows]
    tot = es[0]
    for e in es[1:]:
        tot = tot + e
    acc = jnp.zeros_like(m)
    for i in range(1, layer + 1):
        acc = acc + es[i] / tot
    return acc


def _hgrn_gates(fz, lb):
    log_f = jnp.logaddexp(jnp.log(lb), jnp.log1p(-lb) + jax.nn.log_sigmoid(fz))
    k = (1.0 - lb) * _sigmoid(-fz)
    return log_f, k


def _rotary(x, cos, sin_signed):
    return x * cos + pltpu.roll(x, HEAD_DIM // 2, 1) * sin_signed


def _ret_log_gamma(h):
    return math.log1p(-(2.0 ** (-5.0 - h)))


def _rope_kernel(ang_ref, cos_ref, sin_ref):
    ang = ang_ref[...]
    lane = lax.broadcasted_iota(jnp.int32, ang.shape, 1)
    s = jnp.sin(ang)
    cos_ref[...] = jnp.cos(ang)
    sin_ref[...] = jnp.where(lane < HEAD_DIM // 2, -s, s)


def _rope_tables(positions):
    half = HEAD_DIM // 2
    inv = ROPE_BASE ** (-np.arange(half, dtype=np.float64) / half)
    ang = np.mod(np.asarray(positions, np.float64)[:, None] * inv[None, :], 2.0 * np.pi)
    ang = np.concatenate([ang, ang], axis=1).astype(np.float32)
    n = ang.shape[0]
    pad = (-n) % 8
    ang = np.pad(ang, ((0, pad), (0, 0)))
    shp = jax.ShapeDtypeStruct(ang.shape, f32)
    cos, sin = pl.pallas_call(_rope_kernel, out_shape=(shp, shp), name="rope_tables")(jnp.asarray(ang))
    return cos[:n], sin[:n]


def _memkv_kernel(m_ref, wk_ref, wv_ref, k_ref, v_ref, kb_ref, vb_ref):
    m = m_ref[...]
    k = _mm(m, wk_ref[...])
    v = _mm(m, wv_ref[...])
    k_ref[...] = k
    v_ref[...] = v
    kb_ref[...] = k.astype(bf16)
    vb_ref[...] = v.astype(bf16)


def _memkv(mem2d, wk, wv):
    rows, d = mem2d.shape
    depth = wk.shape[0]
    tile = 256
    w_spec = pl.BlockSpec((None, d, W_BR), lambda l, i: (l, 0, 0))
    o_spec = pl.BlockSpec((None, tile, W_BR), lambda l, i: (l, i, 0))
    return pl.pallas_call(
        _memkv_kernel,
        grid=(depth, rows // tile),
        in_specs=[pl.BlockSpec((tile, d), lambda l, i: (i, 0)), w_spec, w_spec],
        out_specs=[o_spec] * 4,
        out_shape=[jax.ShapeDtypeStruct((depth, rows, W_BR), f32)] * 2
        + [jax.ShapeDtypeStruct((depth, rows, W_BR), bf16)] * 2,
        compiler_params=pltpu.CompilerParams(dimension_semantics=("arbitrary", "arbitrary"),
                                             vmem_limit_bytes=VMEM_LIMIT),
        name="mem_kv",
    )(mem2d, wk, wv)


def _prompt_call(kernel, name, x, consts, per_batch, per_time, outs_tile, outs_batch, scratch):
    B, T, D = x.shape
    nt = T // T_TILE
    in_specs = [pl.BlockSpec((1, T_TILE, D), lambda b, t: (b, t, 0))]
    for a in consts:
        in_specs.append(pl.BlockSpec(a.shape, lambda b, t, n=a.ndim: (0,) * n))
    for a in per_batch:
        in_specs.append(pl.BlockSpec((1,) + a.shape[1:], lambda b, t, n=a.ndim: (b,) + (0,) * (n - 1)))
    for a in per_time:
        in_specs.append(pl.BlockSpec((T_TILE,) + a.shape[1:], lambda b, t, n=a.ndim: (t,) + (0,) * (n - 1)))
    out_specs, out_shape = [], []
    for width, dt in outs_tile:
        out_specs.append(pl.BlockSpec((1, T_TILE, width), lambda b, t: (b, t, 0)))
        out_shape.append(jax.ShapeDtypeStruct((B, T, width), dt))
    for shp, dt in outs_batch:
        out_specs.append(pl.BlockSpec((1,) + shp, lambda b, t, n=len(shp): (b,) + (0,) * n))
        out_shape.append(jax.ShapeDtypeStruct((B,) + shp, dt))
    return pl.pallas_call(
        kernel,
        grid=(B, nt),
        in_specs=in_specs,
        out_specs=out_specs,
        out_shape=out_shape,
        scratch_shapes=scratch,
        compiler_params=pltpu.CompilerParams(dimension_semantics=("arbitrary", "arbitrary"),
                                             vmem_limit_bytes=VMEM_LIMIT),
        name=name,
    )(x, *consts, *per_batch, *per_time)


def _lru_prompt_kernel(x_ref, w_ref, cw_ref, cb_ref, wa_ref, ba_ref, wi_ref, bi_ref, lam_ref,
                       y_ref, h_ref, conv_ref, buf_ref, a_ref, b_ref, hcar_ref):
    n = T_TILE

    @pl.when(pl.program_id(1) == 0)
    def _():
        buf_ref[0:CARRY_ROWS, :] = jnp.zeros((CARRY_ROWS, W_BR), f32)
        hcar_ref[...] = jnp.zeros_like(hcar_ref)

    xa = _mm(x_ref[0], w_ref[...])
    xc = _conv_from_buffer(buf_ref, xa, cw_ref[...], n) + cb_ref[...]
    a, b = _lru_gates(xc, wa_ref, ba_ref[...], wi_ref, bi_ref[...], lam_ref[...])

    row8 = lax.broadcasted_iota(jnp.int32, (n, W_BR), 0) & 7
    for d in (1, 2, 4):
        keep = row8 >= d
        a_prev = jnp.where(keep, pltpu.roll(a, d, 0), 1.0)
        b_prev = jnp.where(keep, pltpu.roll(b, d, 0), 0.0)
        b = b + a * b_prev
        a = a * a_prev
    a_ref[...] = a
    b_ref[...] = b

    def group(gi, h_prev):
        i = pl.multiple_of(gi * 8, 8)
        blk = b_ref[pl.ds(i, 8), :] + a_ref[pl.ds(i, 8), :] * h_prev
        b_ref[pl.ds(i, 8), :] = blk
        return jnp.broadcast_to(blk[7:8, :], (8, W_BR))

    h_last = lax.fori_loop(0, n // 8, group, hcar_ref[...])
    hcar_ref[...] = h_last
    y_ref[0] = b_ref[...].astype(y_ref.dtype)
    h_ref[0] = h_last[0:1, :]
    conv_ref[0] = buf_ref[CARRY_ROWS + n - (CONV_W - 1):CARRY_ROWS + n, :]
    buf_ref[0:CARRY_ROWS, :] = buf_ref[n:n + CARRY_ROWS, :]


def _gdn_prompt_kernel(x_ref, wqkv_ref, wz_ref, wba_ref, cw_ref, alog_ref, dtb_ref, nw_ref,
                       y_ref, conv_ref, s_ref, buf_ref, st_ref):
    n, C = T_TILE, GDN_CHUNK

    @pl.when(pl.program_id(1) == 0)
    def _():
        buf_ref[0:CARRY_ROWS, :] = jnp.zeros((CARRY_ROWS, 3 * W_BR), f32)
        st_ref[...] = jnp.zeros_like(st_ref)

    x = x_ref[0].astype(bf16)
    qkv = _mm(x, wqkv_ref[...])
    c = _silu(_conv_from_buffer(buf_ref, qkv, cw_ref[...], n))
    z = _mm(x, wz_ref[...])
    beta, g = _gdn_gates(_mm(x, wba_ref[...]), alog_ref[...], dtb_ref[...])

    incl_t, _ = _chunk_masks(n, C)
    G = _mask_mm_exact(_as_bf16(incl_t), g)
    GT = G.T
    expG = jnp.exp(G)
    ci = lax.broadcasted_iota(jnp.int32, (C, C), 0)
    cj = lax.broadcasted_iota(jnp.int32, (C, C), 1)
    incl, strict = cj <= ci, cj < ci

    pairs = [(ck, h) for ck in range(n // C) for h in range(N_HEADS)]
    qg, kend, glast, a_qk, rhs, neg_a = {}, {}, {}, {}, {}, {}
    for h in range(N_HEADS):
        q = _l2norm(c[:, h * HEAD_DIM:(h + 1) * HEAD_DIM]) * HEAD_DIM ** -0.5
        k = _l2norm(c[:, W_BR + h * HEAD_DIM:W_BR + (h + 1) * HEAD_DIM])
        v = c[:, 2 * W_BR + h * HEAD_DIM:2 * W_BR + (h + 1) * HEAD_DIM]
        bcol = beta[:, h:h + 1]
        gcol = G[:, N_HEADS + h:N_HEADS + h + 1]
        egcol = expG[:, N_HEADS + h:N_HEADS + h + 1]
        kb = k * bcol
        vb = v * bcol
        kbe = kb * egcol
        qe = q * egcol
        for ck in range(n // C):
            sl = slice(ck * C, (ck + 1) * C)
            gc = gcol[sl]
            decay = jnp.where(incl, jnp.exp(gc - GT[N_HEADS + h:N_HEADS + h + 1, sl]), 0.0)
            neg_a[ck, h] = jnp.where(strict, -(_mm_nt(kb[sl], k[sl]) * decay), 0.0)
            a_qk[ck, h] = _mm_nt(q[sl], k[sl]) * decay
            rhs[ck, h] = jnp.concatenate([vb[sl], kbe[sl]], -1)
            qg[ck, h] = qe[sl]
            glast[ck, h] = gc[C - 1:C, :]
            kend[ck, h] = k[sl] * jnp.exp(gc[C - 1:C, :] - gc)

    m = dict(neg_a)
    t_off = dict(neg_a)
    for _ in range(int(math.log2(C)) - 1):
        m = {p: _mm(m[p], m[p]) for p in pairs}
        t_off = {p: t_off[p] + m[p] + _mm(t_off[p], m[p]) for p in pairs}
    sol = {p: rhs[p] + _mm(t_off[p], rhs[p]) for p in pairs}

    S = [st_ref[h] for h in range(N_HEADS)]
    o_parts = {}
    for ck in range(n // C):
        v_new = {h: sol[ck, h][:, :HEAD_DIM] - _mm(sol[ck, h][:, HEAD_DIM:], S[h]) for h in range(N_HEADS)}
        for h in range(N_HEADS):
            o_parts[ck, h] = _mm(qg[ck, h], S[h]) + _mm(a_qk[ck, h], v_new[h])
        S = [jnp.exp(glast[ck, h]) * S[h] + _mm_tn(kend[ck, h], v_new[h]) for h in range(N_HEADS)]
    for h in range(N_HEADS):
        st_ref[h] = S[h]
        s_ref[0, h] = S[h]
    o_heads = [jnp.concatenate([o_parts[ck, h] for ck in range(n // C)], 0) for h in range(N_HEADS)]

    y = _rmsnorm_heads(o_heads, nw_ref[...]) * _silu(z)
    y_ref[0] = y.astype(y_ref.dtype)
    conv_ref[0] = buf_ref[CARRY_ROWS + n - (CONV_W - 1):CARRY_ROWS + n, :]
    buf_ref[0:CARRY_ROWS, :] = buf_ref[n:n + CARRY_ROWS, :]


def _hgrn_prompt_kernel(layer, x_ref, w_ref, lbraw_ref, nw_ref, y_ref, s_ref, st_ref):
    n, C = T_TILE, HGRN_CHUNK

    @pl.when(pl.program_id(1) == 0)
    def _():
        st_ref[...] = jnp.zeros_like(st_ref)

    u = _mm(x_ref[0], w_ref[...])
    hq, fz, hi, hg = (u[:, i * W_BR:(i + 1) * W_BR] for i in range(4))
    lb = _hgrn_lower_bound(lbraw_ref[...], layer)
    log_f, k = _hgrn_gates(fz, lb)
    q = _silu(hq)

    incl, same = _chunk_masks(n, C)
    G = _mask_mm_exact(_as_bf16(incl), log_f)
    Gtot = _mask_mm_exact(_as_bf16(same), log_f)
    q_g = q * jnp.exp(G)
    k_g = k * jnp.exp(-G)
    k_end = k * jnp.exp(Gtot - G)
    dT = jnp.exp(Gtot).T

    o_heads = []
    for h in range(N_HEADS):
        hs = slice(h * HEAD_DIM, (h + 1) * HEAD_DIM)
        a_qk = jnp.where(incl, _mm_nt(q_g[:, hs], k_g[:, hs]), 0.0)
        o_intra = _mm(a_qk, hi[:, hs])
        S = st_ref[h]
        o_inter = []
        for ck in range(n // C):
            sl = slice(ck * C, (ck + 1) * C)
            o_inter.append(_mm(q_g[sl, hs], S))
            S = dT[hs, ck * C:ck * C + 1] * S + _mm_tn(k_end[sl, hs], hi[sl, hs])
        st_ref[h] = S
        s_ref[0, h] = S
        o_heads.append(o_intra + jnp.concatenate(o_inter, 0))

    y = _rmsnorm_heads(o_heads, nw_ref[...]) * _sigmoid(hg)
    y_ref[0] = y.astype(y_ref.dtype)


def _ret_prompt_kernel(x_ref, w_ref, gw_ref, gb_ref, cos_ref, sin_ref, y_ref, s_ref, st_ref):
    n = T_TILE

    @pl.when(pl.program_id(1) == 0)
    def _():
        st_ref[...] = jnp.zeros_like(st_ref)

    u = _mm(x_ref[0], w_ref[...])
    rq, rk, rv, rg = (u[:, i * W_BR:(i + 1) * W_BR] for i in range(4))
    cos, sin = cos_ref[...], sin_ref[...]
    ri = lax.broadcasted_iota(jnp.int32, (n, n), 0)
    rj = lax.broadcasted_iota(jnp.int32, (n, n), 1)
    dist = (ri - rj).astype(f32)
    pos = lax.broadcasted_iota(jnp.int32, (n, 1), 0).astype(f32)

    o_heads = []
    for h in range(N_HEADS):
        hs = slice(h * HEAD_DIM, (h + 1) * HEAD_DIM)
        lg = _ret_log_gamma(h)
        q = _rotary(rq[:, hs], cos, sin)
        k = _rotary(rk[:, hs], cos, sin) * HEAD_DIM ** -0.5
        v = rv[:, hs]
        decay = jnp.where(rj <= ri, jnp.exp(dist * lg), 0.0)
        S = st_ref[h]
        o = _mm(_mm_nt(q, k) * decay, v) + _mm(q * jnp.exp((pos + 1.0) * lg), S)
        S = math.exp(n * lg) * S + _mm_tn(k * jnp.exp((n - 1.0 - pos) * lg), v)
        st_ref[h] = S
        s_ref[0, h] = S
        o_heads.append(o)

    y = _groupnorm_heads(o_heads, gw_ref[...], gb_ref[...]) * _silu(rg)
    y_ref[0] = y.astype(y_ref.dtype)


def _xattn_prompt_kernel(x_ref, w_ref, mk_ref, mv_ref, y_ref):
    q = _mm(x_ref[0], w_ref[...])
    mk, mv = mk_ref[0], mv_ref[0]
    outs = []
    for h in range(N_HEADS):
        hs = slice(h * HEAD_DIM, (h + 1) * HEAD_DIM)
        s = _mm_nt(q[:, hs], mk[:, hs]) * HEAD_DIM ** -0.5
        e = jnp.exp(s - jnp.max(s, -1, keepdims=True))
        p = e / jnp.sum(e, -1, keepdims=True)
        outs.append(_mm(p, mv[:, hs]))
    y_ref[0] = jnp.concatenate(outs, -1).astype(y_ref.dtype)


def _merge_kernel(alpha, x_ref, ya_ref, yb_ref, yc_ref, yd_ref, ye_ref, wg_ref, bg_ref, wb_ref, wo_ref,
                  g_ref, b_ref, o_ref):
    x = x_ref[...]
    xb = x.astype(bf16)
    merged = None
    for i, y_ref in enumerate((ya_ref, yb_ref, yc_ref, yd_ref, ye_ref)):
        gate = _sigmoid(_mm(xb, wg_ref[i]) + bg_ref[i])
        term = gate * _mm(y_ref[...], wb_ref[i])
        merged = term if merged is None else merged + term
    mix = _mm(merged, wo_ref[...])
    o_ref[...] = _layernorm(alpha * x + mix, g_ref[...], b_ref[...])


def _ffn_kernel(alpha, n_split, x_ref, wg_ref, wv_ref, wd_ref, g_ref, b_ref, o_ref):
    x = x_ref[...]
    xb = x.astype(bf16)
    d_ff = wg_ref.shape[1]
    step = d_ff // n_split
    ffn = None
    for i in range(n_split):
        cs = slice(i * step, (i + 1) * step)
        hidden = _silu(_mm(xb, wg_ref[:, cs])) * _mm(xb, wv_ref[:, cs])
        part = _mm(hidden, wd_ref[cs, :])
        ffn = part if ffn is None else ffn + part
    o_ref[...] = _layernorm(alpha * x + ffn, g_ref[...], b_ref[...])


def _row_call(kernel, name, row_ops, consts, d_out):
    rows = row_ops[0].shape[0]
    tile = min(ROW_TILE, rows)
    in_specs = [pl.BlockSpec((tile, a.shape[1]), lambda i: (i, 0)) for a in row_ops]
    in_specs += [pl.BlockSpec(a.shape, lambda i, n=a.ndim: (0,) * n, pipeline_mode=pl.Buffered(1)) for a in consts]
    return pl.pallas_call(
        kernel,
        grid=(rows // tile,),
        in_specs=in_specs,
        out_specs=pl.BlockSpec((tile, d_out), lambda i: (i, 0)),
        out_shape=jax.ShapeDtypeStruct((rows, d_out), f32),
        compiler_params=pltpu.CompilerParams(dimension_semantics=("arbitrary",), vmem_limit_bytes=VMEM_LIMIT),
        name=name,
    )(*row_ops, *consts)


def _proj_kernel(x_ref, *refs):
    w_refs, o_ref = refs[:-1], refs[-1]
    x = x_ref[...].astype(bf16)
    off = 0
    for w_ref in w_refs:
        width = w_ref.shape[1]
        o_ref[:, off:off + width] = _mm(x, w_ref[...])
        off += width


def _sample_proj(x, weights):
    rows = x.shape[0]
    n = sum(w.shape[1] for w in weights)
    return pl.pallas_call(
        _proj_kernel,
        out_shape=jax.ShapeDtypeStruct((rows, n), f32),
        compiler_params=pltpu.CompilerParams(vmem_limit_bytes=VMEM_LIMIT),
        name="sample_proj",
    )(x, *weights)


def _columns(rows_list):
    nb = rows_list[0].shape[0]
    out = []
    for b in range(nb):
        tile = jnp.concatenate([r[b:b + 1, :] for r in rows_list]
                               + [jnp.zeros((8 - len(rows_list), HEAD_DIM), f32)], 0)
        out.append(tile.T)
    return out


N_SAMPLE_INPUTS = 26


def _sample_kernel(layer, n_alias, *refs):
    (u_ref, h0_ref, lconv_ref, gconv_ref, sg_ref, sh_ref, sr_ref, ck_ref, cv_ref,
     lcw_ref, lcb_ref, lwa_ref, lba_ref, lwi_ref, lbi_ref, lam_ref,
     gcw_ref, alog_ref, dtb_ref, gnw_ref, lbraw_ref, hnw_ref, rgw_ref, rgb_ref, cos_ref, sin_ref) = refs[:N_SAMPLE_INPUTS]
    (ya_ref, yb_ref, yc_ref, yd_ref, ye_ref, h1_ref, lconv_o_ref, gconv_o_ref,
     sg_o_ref, sh_o_ref, sr_o_ref) = refs[N_SAMPLE_INPUTS + n_alias:]
    nb = SAMPLE_BLOCK
    W = W_BR
    u = u_ref[...]
    off = 0

    def take(width):
        nonlocal off
        part = u[:, off:off + width]
        off += width
        return part

    xa, qkv, z = take(W), take(3 * W), take(W)
    hq, fz, hi, hg = take(W), take(W), take(W), take(W)
    rq, rk, rv, rg = take(W), take(W), take(W), take(W)
    xq, ba = take(W), take(HEAD_DIM)

    lcw = lcw_ref[...]
    xc = lcw[CONV_W - 1:CONV_W, :] * xa + lcb_ref[...]
    for j in range(CONV_W - 1):
        xc = xc + lcw[j:j + 1, :] * lconv_ref[j]
    for j in range(CONV_W - 2):
        lconv_o_ref[j] = lconv_ref[j + 1]
    lconv_o_ref[CONV_W - 2] = xa
    a, b = _lru_gates(xc, lwa_ref, lba_ref[...], lwi_ref, lbi_ref[...], lam_ref[...])
    h1 = a * h0_ref[...] + b
    h1_ref[...] = h1
    ya_ref[...] = h1

    gcw = gcw_ref[...]
    c = gcw[CONV_W - 1:CONV_W, :] * qkv
    for j in range(CONV_W - 1):
        c = c + gcw[j:j + 1, :] * gconv_ref[j]
    for j in range(CONV_W - 2):
        gconv_o_ref[j] = gconv_ref[j + 1]
    gconv_o_ref[CONV_W - 2] = qkv
    c = _silu(c)
    beta, g = _gdn_gates(ba, alog_ref[...], dtb_ref[...])
    eg = jnp.exp(g)
    o_rows = [[None] * N_HEADS for _ in range(nb)]
    for h in range(N_HEADS):
        q = _l2norm(c[:, h * HEAD_DIM:(h + 1) * HEAD_DIM]) * HEAD_DIM ** -0.5
        k = _l2norm(c[:, W + h * HEAD_DIM:W + (h + 1) * HEAD_DIM])
        v = c[:, 2 * W + h * HEAD_DIM:2 * W + (h + 1) * HEAD_DIM]
        cols = _columns([q, k])
        for bi in range(nb):
            qcol, kcol = cols[bi][:, 0:1], cols[bi][:, 1:2]
            beta_s = beta[bi:bi + 1, h:h + 1]
            eg_s = eg[bi:bi + 1, N_HEADS + h:N_HEADS + h + 1]
            S0 = sg_ref[bi, h]
            kS = jnp.sum(kcol * S0, axis=0, keepdims=True)
            v_new = beta_s * (v[bi:bi + 1, :] - eg_s * kS)
            S1 = eg_s * S0 + kcol * v_new
            sg_o_ref[bi, h] = S1
            o_rows[bi][h] = jnp.sum(qcol * S1, axis=0, keepdims=True)
    o = [jnp.concatenate([o_rows[bi][h] for bi in range(nb)], 0) for h in range(N_HEADS)]
    yb_ref[...] = _rmsnorm_heads(o, gnw_ref[...]) * _silu(z)

    lb = _hgrn_lower_bound(lbraw_ref[...], layer)
    log_f, kk = _hgrn_gates(fz, lb)
    f = jnp.exp(log_f)
    qs = _silu(hq)
    for h in range(N_HEADS):
        hs = slice(h * HEAD_DIM, (h + 1) * HEAD_DIM)
        cols = _columns([qs[:, hs], kk[:, hs], f[:, hs]])
        for bi in range(nb):
            qcol, kcol, fcol = cols[bi][:, 0:1], cols[bi][:, 1:2], cols[bi][:, 2:3]
            S1 = fcol * sh_ref[bi, h] + kcol * hi[bi:bi + 1, hs]
            sh_o_ref[bi, h] = S1
            o_rows[bi][h] = jnp.sum(qcol * S1, axis=0, keepdims=True)
    o = [jnp.concatenate([o_rows[bi][h] for bi in range(nb)], 0) for h in range(N_HEADS)]
    yc_ref[...] = _rmsnorm_heads(o, hnw_ref[...]) * _sigmoid(hg)

    cos, sin = cos_ref[...], sin_ref[...]
    for h in range(N_HEADS):
        hs = slice(h * HEAD_DIM, (h + 1) * HEAD_DIM)
        gamma = math.exp(_ret_log_gamma(h))
        q = _rotary(rq[:, hs], cos, sin)
        k = _rotary(rk[:, hs], cos, sin) * HEAD_DIM ** -0.5
        cols = _columns([q, k])
        for bi in range(nb):
            qcol, kcol = cols[bi][:, 0:1], cols[bi][:, 1:2]
            S1 = gamma * sr_ref[bi, h] + kcol * rv[bi:bi + 1, hs]
            sr_o_ref[bi, h] = S1
            o_rows[bi][h] = jnp.sum(qcol * S1, axis=0, keepdims=True)
    o = [jnp.concatenate([o_rows[bi][h] for bi in range(nb)], 0) for h in range(N_HEADS)]
    yd_ref[...] = _groupnorm_heads(o, rgw_ref[...], rgb_ref[...]) * _silu(rg)

    hrow = lax.broadcasted_iota(jnp.int32, (8, W), 0)
    hcol = lax.broadcasted_iota(jnp.int32, (8, W), 1) >> int(math.log2(HEAD_DIM))
    head_mask = hrow == hcol
    e_rows = []
    for bi in range(nb):
        qblk = jnp.where(head_mask, jnp.broadcast_to(xq[bi:bi + 1, :], (8, W)), 0.0)
        s = _mm_nt(qblk, ck_ref[bi]) * HEAD_DIM ** -0.5
        e = jnp.exp(s - jnp.max(s, -1, keepdims=True))
        p = e / jnp.sum(e, -1, keepdims=True)
        of = _mm(p, cv_ref[bi])
        e_rows.append(jnp.sum(jnp.where(head_mask, of, 0.0), axis=0, keepdims=True))
    ye_ref[...] = jnp.concatenate(e_rows, 0)


def _sample_branches(layer, u, h0, lconv_t, gconv_t, sg, sh, sr, ck, cv, params, prev_states):
    nb = SAMPLE_BLOCK
    rows = u.shape[0]
    blk2 = lambda a: pl.BlockSpec((nb, a.shape[1]), lambda i: (i, 0))
    blk_t = lambda a: pl.BlockSpec((a.shape[0], nb, a.shape[2]), lambda i: (0, i, 0))
    blk_layer = lambda a: pl.BlockSpec((None, nb) + a.shape[2:],
                                       lambda i, n=a.ndim: (layer, i) + (0,) * (n - 2))
    full = lambda a: pl.BlockSpec(a.shape, lambda i, n=a.ndim: (0,) * n)
    sds = lambda a: jax.ShapeDtypeStruct(a.shape, f32)
    y_shape = jax.ShapeDtypeStruct((rows, W_BR), f32)
    y_spec = pl.BlockSpec((nb, W_BR), lambda i: (i, 0))
    n_in = N_SAMPLE_INPUTS
    assert 9 + len(params) == n_in
    n_small_outs = 8
    return pl.pallas_call(
        functools.partial(_sample_kernel, layer, len(prev_states)),
        grid=(rows // nb,),
        in_specs=[blk2(u), blk2(h0), blk_t(lconv_t), blk_t(gconv_t), blk_layer(sg), blk_layer(sh), blk_layer(sr),
                  blk_layer(ck), blk_layer(cv)] + [full(p) for p in params]
        + [pl.BlockSpec(memory_space=pl.ANY)] * len(prev_states),
        out_specs=[y_spec] * 5 + [blk2(h0), blk_t(lconv_t), blk_t(gconv_t), blk_layer(sg), blk_layer(sh), blk_layer(sr)],
        out_shape=[y_shape] * 5 + [sds(h0), sds(lconv_t), sds(gconv_t), sds(sg), sds(sh), sds(sr)],
        input_output_aliases={n_in + j: n_small_outs + j for j in range(len(prev_states))},
        compiler_params=pltpu.CompilerParams(dimension_semantics=("arbitrary",), vmem_limit_bytes=VMEM_LIMIT),
        name="sample_branches",
    )(u, h0, lconv_t, gconv_t, sg, sh, sr, ck, cv, *params, *prev_states)


def kernel(x_prompt, x_sample, mem_prompt, state_lru_h, state_lru_conv, state_gdn_conv, state_gdn_s, state_hgrn_s, state_ret_s, cache_mem_k, cache_mem_v, w_in, lru_conv_w, lru_conv_b, lru_wa, lru_ba, lru_wi, lru_bi, lru_lambda, gdn_conv_w, gdn_a_log, gdn_dt_bias, gdn_norm_w, hgrn_lb_raw, hgrn_norm_w, ret_gn_w, ret_gn_b, w_mem_k, w_mem_v, w_merge_gate, b_merge_gate, w_branch, w_out, ln1_g, ln1_b, w_ffn_up, w_ffn_down, ln2_g, ln2_b):
    B, T, D = x_prompt.shape
    Bs = x_sample.shape[0]
    depth = w_in.shape[0]
    n_mem = mem_prompt.shape[1]
    W, H, HD = W_BR, N_HEADS, HEAD_DIM
    d_ff = w_ffn_down.shape[1]
    alpha = (2 * depth) ** 0.25
    assert T % T_TILE == 0 and Bs % SAMPLE_BLOCK == 0 and (B * T) % ROW_TILE == 0

    cos_all, sin_all = _rope_tables(list(range(T)) + [PAST_LEN])
    cos_p, sin_p = cos_all[:T], sin_all[:T]
    cos_s, sin_s = cos_all[T:T + 1], sin_all[T:T + 1]

    row = lambda a: a.reshape(1, -1)
    state_spec = ((H, HD, HD), f32)
    state_scratch = pltpu.VMEM((H, HD, HD), f32)

    xp = x_prompt
    xs = x_sample.reshape(Bs, D)
    mk_all, mv_all, mkb_all, mvb_all = _memkv(mem_prompt.reshape(B * n_mem, D), w_mem_k, w_mem_v)
    cache_k = cache_mem_k.reshape(depth, Bs, n_mem, W)
    cache_v = cache_mem_v.reshape(depth, Bs, n_mem, W)
    outs_p = [[] for _ in range(6)]
    outs_s = [[] for _ in range(3)]
    sample_states = ()

    for l in range(depth):
        wl = w_in[l]
        o = 0
        pieces = []
        for width in (W, 3 * W, W, H, H, W, W, W, W, W, W, W, W, W):
            pieces.append(wl[:, o:o + width])
            o += width
        (w_xa, w_qkv, w_z, w_gb, w_ga, w_hq, w_hf, w_hi, w_hg, w_rq, w_rk, w_rv, w_rg, w_xq) = pieces
        w_ba = jnp.concatenate([w_gb, w_ga, jnp.zeros((D, HD - 2 * H), f32)], 1).astype(bf16)
        w_xa, w_qkv, w_z, w_xq = (a.astype(bf16) for a in (w_xa, w_qkv, w_z, w_xq))
        w_hgrn = jnp.concatenate([w_hq, w_hf, w_hi, w_hg], 1).astype(bf16)
        w_ret = jnp.concatenate([w_rq, w_rk, w_rv, w_rg], 1).astype(bf16)

        lru_params = (lru_conv_w[l], row(lru_conv_b[l]), lru_wa[l].astype(bf16), row(lru_ba[l]),
                      lru_wi[l].astype(bf16), row(lru_bi[l]), row(lru_lambda[l]))
        pad_lanes = lambda a, start: jnp.zeros((1, HD), f32).at[0, start:start + H].set(a)
        alog_row, dtb_row = pad_lanes(gdn_a_log[l], H), pad_lanes(gdn_dt_bias[l], H)
        gdn_params = (gdn_conv_w[l], alog_row, dtb_row, row(gdn_norm_w[l]))
        hgrn_params = (hgrn_lb_raw, row(hgrn_norm_w[l]))
        ret_params = (row(ret_gn_w[l]), row(ret_gn_b[l]))
        merge_consts = (w_merge_gate[l].astype(bf16), b_merge_gate[l].reshape(N_BR, 1, D), w_branch[l].astype(bf16),
                        w_out[l].astype(bf16), row(ln1_g[l]), row(ln1_b[l]))
        ffn_consts = (w_ffn_up[l][:, :d_ff].astype(bf16), w_ffn_up[l][:, d_ff:].astype(bf16),
                      w_ffn_down[l].astype(bf16), row(ln2_g[l]), row(ln2_b[l]))
        merge_k = functools.partial(_merge_kernel, alpha)
        ffn_k = functools.partial(_ffn_kernel, alpha, 2)

        ya, h_last, lconv = _prompt_call(
            _lru_prompt_kernel, "lru_prompt", xp, (w_xa,) + lru_params, (), (),
            [(W, bf16)], [((1, W), f32), ((CONV_W - 1, W), f32)],
            [pltpu.VMEM((T_TILE + CARRY_ROWS, W), f32), pltpu.VMEM((T_TILE, W), f32), pltpu.VMEM((T_TILE, W), f32),
             pltpu.VMEM((8, W), f32)])
        yb, gconv, s_gdn = _prompt_call(
            _gdn_prompt_kernel, "gdn_prompt", xp, (w_qkv, w_z, w_ba) + gdn_params, (), (),
            [(W, bf16)], [((CONV_W - 1, 3 * W), f32), state_spec],
            [pltpu.VMEM((T_TILE + CARRY_ROWS, 3 * W), f32), state_scratch])
        yc, s_hgrn = _prompt_call(
            functools.partial(_hgrn_prompt_kernel, l), "hgrn_prompt", xp, (w_hgrn,) + hgrn_params, (), (),
            [(W, bf16)], [state_spec], [state_scratch])
        yd, s_ret = _prompt_call(
            _ret_prompt_kernel, "ret_prompt", xp, (w_ret,) + ret_params, (), (cos_p, sin_p),
            [(W, bf16)], [state_spec], [state_scratch])
        (ye,) = _prompt_call(
            _xattn_prompt_kernel, "xattn_prompt", xp, (w_xq,),
            (mkb_all[l].reshape(B, n_mem, W), mvb_all[l].reshape(B, n_mem, W)), (), [(W, bf16)], [], [])
        x2d = xp.reshape(B * T, D)
        ys = [y.reshape(B * T, W) for y in (ya, yb, yc, yd, ye)]
        x2d = _row_call(merge_k, "merge_prompt", [x2d] + ys, merge_consts, D)
        x2d = _row_call(ffn_k, "ffn_prompt", [x2d], ffn_consts, D)
        xp = x2d.reshape(B, T, D)
        for lst, val in zip(outs_p, (h_last.reshape(B, W), lconv, gconv, s_gdn, s_hgrn, s_ret)):
            lst.append(val)

        u_s = _sample_proj(xs, (w_xa, w_qkv, w_z, w_hgrn, w_ret, w_xq, w_ba))
        sample_params = lru_params + gdn_params + hgrn_params + ret_params + (cos_s, sin_s)
        res = _sample_branches(
            l, u_s, state_lru_h[l], state_lru_conv[l].transpose(1, 0, 2), state_gdn_conv[l].transpose(1, 0, 2),
            state_gdn_s, state_hgrn_s, state_ret_s, cache_k, cache_v, sample_params, sample_states)
        ys_s, (h1, lconv_s, gconv_s), sample_states = res[:5], res[5:8], tuple(res[8:])
        xs = _row_call(merge_k, "merge_sample", [xs] + list(ys_s), merge_consts, D)
        xs = _row_call(ffn_k, "ffn_sample", [xs], ffn_consts, D)
        for lst, val in zip(outs_s, (h1, lconv_s.transpose(1, 0, 2), gconv_s.transpose(1, 0, 2))):
            lst.append(val)

    mem_shape = (depth, B, n_mem, H, HD)
    return (xp, xs.reshape(Bs, 1, D), *(jnp.stack(v) for v in outs_p), mk_all.reshape(mem_shape),
            mv_all.reshape(mem_shape), *(jnp.stack(v) for v in outs_s), *sample_states)
```

```python
import functools
import math

import numpy as np
import jax
import jax.numpy as jnp
from jax import lax
from jax.experimental import pallas as pl
from jax.experimental.pallas import tpu as pltpu

f32 = jnp.float32
bf16 = jnp.bfloat16

N_HEADS = 4
HEAD_DIM = 128
W_BR = N_HEADS * HEAD_DIM
CONV_W = 4
N_BR = 5
LRU_C = 8.0
GDN_CHUNK = 64
HGRN_CHUNK = 16
ROPE_BASE = 10000.0
LN_EPS = 1e-5
NORM_EPS = 1e-6
PAST_LEN = 16384

T_TILE = 256
T_TILE_WIDE = 512
ROW_TILE = 512
SAMPLE_BLOCK = 8
CARRY_ROWS = 8
VMEM_LIMIT = 52 * 1024 * 1024


def _mm(a, b):
    return jnp.dot(a.astype(bf16), b.astype(bf16), preferred_element_type=f32)


def _mm_nt(a, b):
    return lax.dot_general(a.astype(bf16), b.astype(bf16), (((1,), (1,)), ((), ())), preferred_element_type=f32)


def _mm_tn(a, b):
    return lax.dot_general(a.astype(bf16), b.astype(bf16), (((0,), (0,)), ((), ())), preferred_element_type=f32)


def _mask_mm_exact(mask_bf16, x):
    hi = x.astype(bf16)
    r1 = x - hi.astype(f32)
    mid = r1.astype(bf16)
    lo = (r1 - mid.astype(f32)).astype(bf16)
    dot = functools.partial(jnp.dot, preferred_element_type=f32)
    return dot(mask_bf16, hi) + dot(mask_bf16, mid) + dot(mask_bf16, lo)


def _chunk_masks(n, chunk):
    shift = int(math.log2(chunk))
    row = lax.broadcasted_iota(jnp.int32, (n, n), 0)
    col = lax.broadcasted_iota(jnp.int32, (n, n), 1)
    same = (row >> shift) == (col >> shift)
    return jnp.logical_and(same, col <= row), same


def _as_bf16(mask):
    return jnp.where(mask, 1.0, 0.0).astype(bf16)


def _sigmoid(x):
    return jax.nn.sigmoid(x)


def _silu(x):
    return x * jax.nn.sigmoid(x)


def _layernorm(x, g, b):
    mu = jnp.mean(x, -1, keepdims=True)
    xc = x - mu
    var = jnp.mean(xc * xc, -1, keepdims=True)
    return xc * lax.rsqrt(var + LN_EPS) * g + b


def _head_slices(a):
    return [a[:, h * HEAD_DIM:(h + 1) * HEAD_DIM] for h in range(N_HEADS)]


def _causal_conv(buf_ref, x, cw):
    n = x.shape[0]
    buf_ref[CARRY_ROWS:CARRY_ROWS + n, :] = x
    out = cw[CONV_W - 1:CONV_W, :] * x
    for j in range(CONV_W - 1):
        off = CARRY_ROWS - (CONV_W - 1) + j
        out = out + cw[j:j + 1, :] * buf_ref[off:off + n, :]
    buf_ref[0:CARRY_ROWS, :] = buf_ref[n:n + CARRY_ROWS, :]
    return out


def _conv_state(buf_ref):
    return buf_ref[CARRY_ROWS - (CONV_W - 1):CARRY_ROWS, :]


def _lru_gates(xc, wa_ref, ba, wi_ref, bi, lam):
    xh = _head_slices(xc)
    r = _sigmoid(jnp.concatenate([_mm(xh[h], wa_ref[h]) for h in range(N_HEADS)], -1) + ba)
    ig = _sigmoid(jnp.concatenate([_mm(xh[h], wi_ref[h]) for h in range(N_HEADS)], -1) + bi)
    log_a = -LRU_C * r * jax.nn.softplus(-lam)
    a = jnp.exp(log_a)
    b = jnp.sqrt(jnp.tanh(-log_a) * (1.0 + a * a)) * (ig * xc)
    return a, b


def _gdn_gates(ba, alog_row, dtb_row):
    beta = _sigmoid(ba)
    g = -jnp.exp(alog_row) * jax.nn.softplus(ba + dtb_row)
    return beta, g


def _l2norm(a):
    return a * lax.rsqrt(jnp.sum(a * a, -1, keepdims=True) + NORM_EPS)


def _rmsnorm_heads(o_heads, w):
    parts = [o * lax.rsqrt(jnp.mean(o * o, -1, keepdims=True) + NORM_EPS) for o in o_heads]
    return jnp.concatenate(parts, -1) * w


def _groupnorm_heads(o_heads, w, b):
    parts = []
    for o in o_heads:
        mu = jnp.mean(o, -1, keepdims=True)
        oc = o - mu
        var = jnp.mean(oc * oc, -1, keepdims=True)
        parts.append(oc * lax.rsqrt(var + NORM_EPS))
    return jnp.concatenate(parts, -1) * w + b


def _hgrn_lower_bound(lbraw, layer):
    n = lbraw.shape[0]
    rows = [lbraw[i:i + 1, :] for i in range(n)]
    m = rows[0]
    for r in rows[1:]:
        m = jnp.maximum(m, r)
    es = [jnp.exp(r - m) for r in rows]
    tot = es[0]
    for e in es[1:]:
        tot = tot + e
    acc = jnp.zeros_like(m)
    for i in range(1, layer + 1):
        acc = acc + es[i] / tot
    return acc


def _hgrn_gates(fz, lb):
    log_f = jnp.logaddexp(jnp.log(lb), jnp.log1p(-lb) + jax.nn.log_sigmoid(fz))
    k = (1.0 - lb) * _sigmoid(-fz)
    return log_f, k


def _rotary(x, cos, sin_signed):
    return x * cos + pltpu.roll(x, HEAD_DIM // 2, 1) * sin_signed


def _ret_log_gamma(h):
    return math.log1p(-(2.0 ** (-5.0 - h)))


def _rope_kernel(ang_ref, cos_ref, sin_ref):
    ang = ang_ref[...]
    lane = lax.broadcasted_iota(jnp.int32, ang.shape, 1)
    s = jnp.sin(ang)
    cos_ref[...] = jnp.cos(ang)
    sin_ref[...] = jnp.where(lane < HEAD_DIM // 2, -s, s)


def _rope_tables(positions):
    half = HEAD_DIM // 2
    inv = ROPE_BASE ** (-np.arange(half, dtype=np.float64) / half)
    ang = np.mod(np.asarray(positions, np.float64)[:, None] * inv[None, :], 2.0 * np.pi)
    ang = np.concatenate([ang, ang], axis=1).astype(np.float32)
    n = ang.shape[0]
    pad = (-n) % 8
    ang = np.pad(ang, ((0, pad), (0, 0)))
    shp = jax.ShapeDtypeStruct(ang.shape, f32)
    cos, sin = pl.pallas_call(_rope_kernel, out_shape=(shp, shp), name="rope_tables")(jnp.asarray(ang))
    return cos[:n], sin[:n]


def _memkv_kernel(m_ref, wk_ref, wv_ref, k_ref, v_ref, kb_ref, vb_ref):
    m = m_ref[...]
    k = _mm(m, wk_ref[...])
    v = _mm(m, wv_ref[...])
    k_ref[...] = k
    v_ref[...] = v
    kb_ref[...] = k.astype(bf16)
    vb_ref[...] = v.astype(bf16)


def _memkv(mem2d, wk, wv):
    rows, d = mem2d.shape
    depth = wk.shape[0]
    tile = 256
    w_spec = pl.BlockSpec((None, d, W_BR), lambda l, i: (l, 0, 0))
    o_spec = pl.BlockSpec((None, tile, W_BR), lambda l, i: (l, i, 0))
    return pl.pallas_call(
        _memkv_kernel,
        grid=(depth, rows // tile),
        in_specs=[pl.BlockSpec((tile, d), lambda l, i: (i, 0)), w_spec, w_spec],
        out_specs=[o_spec] * 4,
        out_shape=[jax.ShapeDtypeStruct((depth, rows, W_BR), f32)] * 2
        + [jax.ShapeDtypeStruct((depth, rows, W_BR), bf16)] * 2,
        compiler_params=pltpu.CompilerParams(dimension_semantics=("arbitrary", "arbitrary"),
                                             vmem_limit_bytes=VMEM_LIMIT),
        name="mem_kv",
    )(mem2d, wk, wv)


def _prompt_call(kernel, name, x, tile, consts, per_batch, per_time, outs_tile, outs_batch, scratch):
    B, T, D = x.shape
    assert T % tile == 0
    nt = T // tile
    in_specs = [pl.BlockSpec((1, tile, D), lambda b, t: (b, t, 0))]
    for a in consts:
        in_specs.append(pl.BlockSpec(a.shape, lambda b, t, n=a.ndim: (0,) * n))
    for a in per_batch:
        in_specs.append(pl.BlockSpec((1,) + a.shape[1:], lambda b, t, n=a.ndim: (b,) + (0,) * (n - 1)))
    for a in per_time:
        in_specs.append(pl.BlockSpec((tile,) + a.shape[1:], lambda b, t, n=a.ndim: (t,) + (0,) * (n - 1)))
    out_specs, out_shape = [], []
    for width, dt in outs_tile:
        out_specs.append(pl.BlockSpec((1, tile, width), lambda b, t: (b, t, 0)))
        out_shape.append(jax.ShapeDtypeStruct((B, T, width), dt))
    for shp, dt in outs_batch:
        out_specs.append(pl.BlockSpec((1,) + shp, lambda b, t, n=len(shp): (b,) + (0,) * n))
        out_shape.append(jax.ShapeDtypeStruct((B,) + shp, dt))
    return pl.pallas_call(
        kernel,
        grid=(B, nt),
        in_specs=in_specs,
        out_specs=out_specs,
        out_shape=out_shape,
        scratch_shapes=scratch,
        compiler_params=pltpu.CompilerParams(dimension_semantics=("arbitrary", "arbitrary"),
                                             vmem_limit_bytes=VMEM_LIMIT),
        name=name,
    )(x, *consts, *per_batch, *per_time)


def _lru_prompt_kernel(x_ref, w_ref, cw_ref, cb_ref, wa_ref, ba_ref, wi_ref, bi_ref, lam_ref,
                       y_ref, h_ref, conv_ref, buf_ref, a_ref, b_ref, hcar_ref):
    n = x_ref.shape[1]

    @pl.when(pl.program_id(1) == 0)
    def _():
        buf_ref[0:CARRY_ROWS, :] = jnp.zeros((CARRY_ROWS, W_BR), f32)
        hcar_ref[...] = jnp.zeros_like(hcar_ref)

    xa = _mm(x_ref[0], w_ref[...])
    xc = _causal_conv(buf_ref, xa, cw_ref[...]) + cb_ref[...]
    a, b = _lru_gates(xc, wa_ref, ba_ref[...], wi_ref, bi_ref[...], lam_ref[...])

    row8 = lax.broadcasted_iota(jnp.int32, (n, W_BR), 0) & 7
    for d in (1, 2, 4):
        keep = row8 >= d
        a_prev = jnp.where(keep, pltpu.roll(a, d, 0), 1.0)
        b_prev = jnp.where(keep, pltpu.roll(b, d, 0), 0.0)
        b = b + a * b_prev
        a = a * a_prev
    a_ref[...] = a
    b_ref[...] = b

    def group(gi, h_prev):
        i = pl.multiple_of(gi * 8, 8)
        blk = b_ref[pl.ds(i, 8), :] + a_ref[pl.ds(i, 8), :] * h_prev
        b_ref[pl.ds(i, 8), :] = blk
        return jnp.broadcast_to(blk[7:8, :], (8, W_BR))

    h_last = lax.fori_loop(0, n // 8, group, hcar_ref[...])
    hcar_ref[...] = h_last
    y_ref[0] = b_ref[...].astype(y_ref.dtype)
    h_ref[0] = h_last[0:1, :]
    conv_ref[0] = _conv_state(buf_ref)


def _gdn_prompt_kernel(x_ref, wqkv_ref, wz_ref, wba_ref, cw_ref, alog_ref, dtb_ref, nw_ref,
                       y_ref, conv_ref, s_ref, buf_ref, st_ref):
    n, C = x_ref.shape[1], GDN_CHUNK

    @pl.when(pl.program_id(1) == 0)
    def _():
        buf_ref[0:CARRY_ROWS, :] = jnp.zeros((CARRY_ROWS, 3 * W_BR), f32)
        st_ref[...] = jnp.zeros_like(st_ref)

    x = x_ref[0].astype(bf16)
    qkv = _mm(x, wqkv_ref[...])
    c = _silu(_causal_conv(buf_ref, qkv, cw_ref[...]))
    z = _mm(x, wz_ref[...])
    beta, g = _gdn_gates(_mm(x, wba_ref[...]), alog_ref[...], dtb_ref[...])

    incl_t, _ = _chunk_masks(n, C)
    G = _mask_mm_exact(_as_bf16(incl_t), g)
    GT = G.T
    expG = jnp.exp(G)
    ci = lax.broadcasted_iota(jnp.int32, (C, C), 0)
    cj = lax.broadcasted_iota(jnp.int32, (C, C), 1)
    incl, strict = cj <= ci, cj < ci

    pairs = [(ck, h) for ck in range(n // C) for h in range(N_HEADS)]
    qg, kend, glast, a_qk, rhs, neg_a = {}, {}, {}, {}, {}, {}
    for h in range(N_HEADS):
        q = _l2norm(c[:, h * HEAD_DIM:(h + 1) * HEAD_DIM]) * HEAD_DIM ** -0.5
        k = _l2norm(c[:, W_BR + h * HEAD_DIM:W_BR + (h + 1) * HEAD_DIM])
        v = c[:, 2 * W_BR + h * HEAD_DIM:2 * W_BR + (h + 1) * HEAD_DIM]
        bcol = beta[:, h:h + 1]
        gcol = G[:, N_HEADS + h:N_HEADS + h + 1]
        egcol = expG[:, N_HEADS + h:N_HEADS + h + 1]
        kb = k * bcol
        vb = v * bcol
        kbe = kb * egcol
        qe = q * egcol
        for ck in range(n // C):
            sl = slice(ck * C, (ck + 1) * C)
            gc = gcol[sl]
            decay = jnp.where(incl, jnp.exp(gc - GT[N_HEADS + h:N_HEADS + h + 1, sl]), 0.0)
            neg_a[ck, h] = jnp.where(strict, -(_mm_nt(kb[sl], k[sl]) * decay), 0.0)
            a_qk[ck, h] = _mm_nt(q[sl], k[sl]) * decay
            rhs[ck, h] = jnp.concatenate([vb[sl], kbe[sl]], -1)
            qg[ck, h] = qe[sl]
            glast[ck, h] = gc[C - 1:C, :]
            kend[ck, h] = k[sl] * jnp.exp(gc[C - 1:C, :] - gc)

    m = dict(neg_a)
    t_off = dict(neg_a)
    for _ in range(int(math.log2(C)) - 1):
        m = {p: _mm(m[p], m[p]) for p in pairs}
        t_off = {p: t_off[p] + m[p] + _mm(t_off[p], m[p]) for p in pairs}
    sol = {p: rhs[p] + _mm(t_off[p], rhs[p]) for p in pairs}

    S = [st_ref[h] for h in range(N_HEADS)]
    o_parts = {}
    for ck in range(n // C):
        v_new = {h: sol[ck, h][:, :HEAD_DIM] - _mm(sol[ck, h][:, HEAD_DIM:], S[h]) for h in range(N_HEADS)}
        for h in range(N_HEADS):
            o_parts[ck, h] = _mm(qg[ck, h], S[h]) + _mm(a_qk[ck, h], v_new[h])
        S = [jnp.exp(glast[ck, h]) * S[h] + _mm_tn(kend[ck, h], v_new[h]) for h in range(N_HEADS)]
    for h in range(N_HEADS):
        st_ref[h] = S[h]
        s_ref[0, h] = S[h]
    o_heads = [jnp.concatenate([o_parts[ck, h] for ck in range(n // C)], 0) for h in range(N_HEADS)]

    y = _rmsnorm_heads(o_heads, nw_ref[...]) * _silu(z)
    y_ref[0] = y.astype(y_ref.dtype)
    conv_ref[0] = _conv_state(buf_ref)


def _hgrn_prompt_kernel(layer, x_ref, w_ref, lbraw_ref, nw_ref, y_ref, s_ref, st_ref):
    n, C = x_ref.shape[1], HGRN_CHUNK

    @pl.when(pl.program_id(1) == 0)
    def _():
        st_ref[...] = jnp.zeros_like(st_ref)

    u = _mm(x_ref[0], w_ref[...])
    hq, fz, hi, hg = (u[:, i * W_BR:(i + 1) * W_BR] for i in range(4))
    lb = _hgrn_lower_bound(lbraw_ref[...], layer)
    log_f, k = _hgrn_gates(fz, lb)
    q = _silu(hq)

    incl, _ = _chunk_masks(n, C)
    G = _mask_mm_exact(_as_bf16(incl), log_f)
    g_last = [G[(ck + 1) * C - 1:(ck + 1) * C, :] for ck in range(n // C)]
    Gtot = jnp.concatenate([jnp.broadcast_to(r, (C, W_BR)) for r in g_last], 0)
    q_g = q * jnp.exp(G)
    k_g = k * jnp.exp(-G)
    k_end = k * jnp.exp(Gtot - G)
    dT = jnp.exp(jnp.concatenate(g_last, 0)).T

    o_heads = []
    for h in range(N_HEADS):
        hs = slice(h * HEAD_DIM, (h + 1) * HEAD_DIM)
        a_qk = jnp.where(incl, _mm_nt(q_g[:, hs], k_g[:, hs]), 0.0)
        o_intra = _mm(a_qk, hi[:, hs])
        S = st_ref[h]
        o_inter = []
        for ck in range(n // C):
            sl = slice(ck * C, (ck + 1) * C)
            o_inter.append(_mm(q_g[sl, hs], S))
            S = dT[hs, ck:ck + 1] * S + _mm_tn(k_end[sl, hs], hi[sl, hs])
        st_ref[h] = S
        s_ref[0, h] = S
        o_heads.append(o_intra + jnp.concatenate(o_inter, 0))

    y = _rmsnorm_heads(o_heads, nw_ref[...]) * _sigmoid(hg)
    y_ref[0] = y.astype(y_ref.dtype)


def _ret_prompt_kernel(x_ref, w_ref, gw_ref, gb_ref, cos_ref, sin_ref, y_ref, s_ref, st_ref):
    n = x_ref.shape[1]

    @pl.when(pl.program_id(1) == 0)
    def _():
        st_ref[...] = jnp.zeros_like(st_ref)

    u = _mm(x_ref[0], w_ref[...])
    rq, rk, rv, rg = (u[:, i * W_BR:(i + 1) * W_BR] for i in range(4))
    cos, sin = cos_ref[...], sin_ref[...]
    ri = lax.broadcasted_iota(jnp.int32, (n, n), 0)
    rj = lax.broadcasted_iota(jnp.int32, (n, n), 1)
    dist = (ri - rj).astype(f32)
    pos = lax.broadcasted_iota(jnp.int32, (n, 1), 0).astype(f32)

    o_heads = []
    for h in range(N_HEADS):
        hs = slice(h * HEAD_DIM, (h + 1) * HEAD_DIM)
        lg = _ret_log_gamma(h)
        q = _rotary(rq[:, hs], cos, sin)
        k = _rotary(rk[:, hs], cos, sin) * HEAD_DIM ** -0.5
        v = rv[:, hs]
        decay = jnp.where(rj <= ri, jnp.exp(dist * lg), 0.0)
        S = st_ref[h]
        o = _mm(_mm_nt(q, k) * decay, v) + _mm(q * jnp.exp((pos + 1.0) * lg), S)
        S = math.exp(n * lg) * S + _mm_tn(k * jnp.exp((n - 1.0 - pos) * lg), v)
        st_ref[h] = S
        s_ref[0, h] = S
        o_heads.append(o)

    y = _groupnorm_heads(o_heads, gw_ref[...], gb_ref[...]) * _silu(rg)
    y_ref[0] = y.astype(y_ref.dtype)


def _xattn_prompt_kernel(x_ref, w_ref, mk_ref, mv_ref, y_ref):
    q = _mm(x_ref[0], w_ref[...])
    mk, mv = mk_ref[0], mv_ref[0]
    outs = []
    for h in range(N_HEADS):
        hs = slice(h * HEAD_DIM, (h + 1) * HEAD_DIM)
        s = _mm_nt(q[:, hs], mk[:, hs]) * HEAD_DIM ** -0.5
        e = jnp.exp(s - jnp.max(s, -1, keepdims=True))
        p = e / jnp.sum(e, -1, keepdims=True)
        outs.append(_mm(p, mv[:, hs]))
    y_ref[0] = jnp.concatenate(outs, -1).astype(y_ref.dtype)


def _merge_kernel(alpha, x_ref, ya_ref, yb_ref, yc_ref, yd_ref, ye_ref, wg_ref, bg_ref, wb_ref, wo_ref,
                  g_ref, b_ref, o_ref):
    x = x_ref[...]
    xb = x.astype(bf16)
    merged = None
    for i, y_ref in enumerate((ya_ref, yb_ref, yc_ref, yd_ref, ye_ref)):
        gate = _sigmoid(_mm(xb, wg_ref[i]) + bg_ref[i])
        term = gate * _mm(y_ref[...], wb_ref[i])
        merged = term if merged is None else merged + term
    mix = _mm(merged, wo_ref[...])
    o_ref[...] = _layernorm(alpha * x + mix, g_ref[...], b_ref[...])


def _ffn_kernel(alpha, n_split, x_ref, wg_ref, wv_ref, wd_ref, g_ref, b_ref, o_ref):
    x = x_ref[...]
    xb = x.astype(bf16)
    d_ff = wg_ref.shape[1]
    step = d_ff // n_split
    ffn = None
    for i in range(n_split):
        cs = slice(i * step, (i + 1) * step)
        hidden = _silu(_mm(xb, wg_ref[:, cs])) * _mm(xb, wv_ref[:, cs])
        part = _mm(hidden, wd_ref[cs, :])
        ffn = part if ffn is None else ffn + part
    o_ref[...] = _layernorm(alpha * x + ffn, g_ref[...], b_ref[...])


def _row_call(kernel, name, row_ops, consts, d_out):
    rows = row_ops[0].shape[0]
    tile = min(ROW_TILE, rows)
    in_specs = [pl.BlockSpec((tile, a.shape[1]), lambda i: (i, 0)) for a in row_ops]
    in_specs += [pl.BlockSpec(a.shape, lambda i, n=a.ndim: (0,) * n, pipeline_mode=pl.Buffered(1)) for a in consts]
    return pl.pallas_call(
        kernel,
        grid=(rows // tile,),
        in_specs=in_specs,
        out_specs=pl.BlockSpec((tile, d_out), lambda i: (i, 0)),
        out_shape=jax.ShapeDtypeStruct((rows, d_out), f32),
        compiler_params=pltpu.CompilerParams(dimension_semantics=("arbitrary",), vmem_limit_bytes=VMEM_LIMIT),
        name=name,
    )(*row_ops, *consts)


def _proj_kernel(x_ref, *refs):
    w_refs, o_ref = refs[:-1], refs[-1]
    x = x_ref[...].astype(bf16)
    off = 0
    for w_ref in w_refs:
        width = w_ref.shape[1]
        o_ref[:, off:off + width] = _mm(x, w_ref[...])
        off += width


def _sample_proj(x, weights):
    rows = x.shape[0]
    n = sum(w.shape[1] for w in weights)
    return pl.pallas_call(
        _proj_kernel,
        out_shape=jax.ShapeDtypeStruct((rows, n), f32),
        compiler_params=pltpu.CompilerParams(vmem_limit_bytes=VMEM_LIMIT),
        name="sample_proj",
    )(x, *weights)


def _columns(rows_list):
    nb = rows_list[0].shape[0]
    out = []
    for b in range(nb):
        tile = jnp.concatenate([r[b:b + 1, :] for r in rows_list]
                               + [jnp.zeros((8 - len(rows_list), HEAD_DIM), f32)], 0)
        out.append(tile.T)
    return out


N_SAMPLE_INPUTS = 24


def _rows_of(parts):
    nb = 1 + max(bi for bi, _ in parts)
    return [jnp.concatenate([parts[bi, h] for bi in range(nb)], 0) for h in range(N_HEADS)]


def _sample_kernel(layer, n_alias, *refs):
    (u_ref, h0_ref, lconv_ref, gconv_ref, sg_ref, sh_ref, sr_ref,
     lcw_ref, lcb_ref, lwa_ref, lba_ref, lwi_ref, lbi_ref, lam_ref,
     gcw_ref, alog_ref, dtb_ref, gnw_ref, lbraw_ref, hnw_ref, rgw_ref, rgb_ref, cos_ref, sin_ref) = refs[:N_SAMPLE_INPUTS]
    (ya_ref, yb_ref, yc_ref, yd_ref, h1_ref, lconv_o_ref, gconv_o_ref,
     sg_o_ref, sh_o_ref, sr_o_ref) = refs[N_SAMPLE_INPUTS + n_alias:]
    nb = SAMPLE_BLOCK
    W = W_BR
    pairs = [(bi, h) for bi in range(nb) for h in range(N_HEADS)]
    u = u_ref[...]
    off = 0

    def take(width):
        nonlocal off
        part = u[:, off:off + width]
        off += width
        return part

    xa, qkv, z = take(W), take(3 * W), take(W)
    hq, fz, hi, hg = take(W), take(W), take(W), take(W)
    rq, rk, rv, rg = take(W), take(W), take(W), take(W)
    xq, ba = take(W), take(HEAD_DIM)

    lcw = lcw_ref[...]
    xc = lcw[CONV_W - 1:CONV_W, :] * xa + lcb_ref[...]
    for j in range(CONV_W - 1):
        xc = xc + lcw[j:j + 1, :] * lconv_ref[j]
    for j in range(CONV_W - 2):
        lconv_o_ref[j] = lconv_ref[j + 1]
    lconv_o_ref[CONV_W - 2] = xa
    a, b = _lru_gates(xc, lwa_ref, lba_ref[...], lwi_ref, lbi_ref[...], lam_ref[...])
    h1 = a * h0_ref[...] + b
    h1_ref[...] = h1
    ya_ref[...] = h1

    gcw = gcw_ref[...]
    c = gcw[CONV_W - 1:CONV_W, :] * qkv
    for j in range(CONV_W - 1):
        c = c + gcw[j:j + 1, :] * gconv_ref[j]
    for j in range(CONV_W - 2):
        gconv_o_ref[j] = gconv_ref[j + 1]
    gconv_o_ref[CONV_W - 2] = qkv
    c = _silu(c)
    beta, g = _gdn_gates(ba, alog_ref[...], dtb_ref[...])
    eg = jnp.exp(g)
    qn, kn, vv, kcol = [], [], [], {}
    for h in range(N_HEADS):
        qn.append(_l2norm(c[:, h * HEAD_DIM:(h + 1) * HEAD_DIM]) * HEAD_DIM ** -0.5)
        kn.append(_l2norm(c[:, W + h * HEAD_DIM:W + (h + 1) * HEAD_DIM]))
        vv.append(c[:, 2 * W + h * HEAD_DIM:2 * W + (h + 1) * HEAD_DIM])
        cols = _columns([kn[h]])
        for bi in range(nb):
            kcol[bi, h] = cols[bi][:, 0:1]
    kS = _rows_of({(bi, h): _mm(kn[h][bi:bi + 1, :], sg_ref[bi, h]) for bi, h in pairs})
    v_new = [beta[:, h:h + 1] * (vv[h] - eg[:, N_HEADS + h:N_HEADS + h + 1] * kS[h]) for h in range(N_HEADS)]
    S1 = {}
    for bi, h in pairs:
        S1[bi, h] = (eg[bi:bi + 1, N_HEADS + h:N_HEADS + h + 1] * sg_ref[bi, h]
                     + kcol[bi, h] * v_new[h][bi:bi + 1, :])
        sg_o_ref[bi, h] = S1[bi, h]
    o = _rows_of({(bi, h): _mm(qn[h][bi:bi + 1, :], S1[bi, h]) for bi, h in pairs})
    yb_ref[...] = _rmsnorm_heads(o, gnw_ref[...]) * _silu(z)

    lb = _hgrn_lower_bound(lbraw_ref[...], layer)
    log_f, kk = _hgrn_gates(fz, lb)
    f = jnp.exp(log_f)
    qs = _silu(hq)
    kcol, fcol = {}, {}
    for h in range(N_HEADS):
        hs = slice(h * HEAD_DIM, (h + 1) * HEAD_DIM)
        cols = _columns([kk[:, hs], f[:, hs]])
        for bi in range(nb):
            kcol[bi, h], fcol[bi, h] = cols[bi][:, 0:1], cols[bi][:, 1:2]
    S1 = {}
    for bi, h in pairs:
        hs = slice(h * HEAD_DIM, (h + 1) * HEAD_DIM)
        S1[bi, h] = fcol[bi, h] * sh_ref[bi, h] + kcol[bi, h] * hi[bi:bi + 1, hs]
        sh_o_ref[bi, h] = S1[bi, h]
    o = _rows_of({(bi, h): _mm(qs[bi:bi + 1, h * HEAD_DIM:(h + 1) * HEAD_DIM], S1[bi, h]) for bi, h in pairs})
    yc_ref[...] = _rmsnorm_heads(o, hnw_ref[...]) * _sigmoid(hg)

    cos, sin = cos_ref[...], sin_ref[...]
    qr, kcol = [], {}
    for h in range(N_HEADS):
        hs = slice(h * HEAD_DIM, (h + 1) * HEAD_DIM)
        qr.append(_rotary(rq[:, hs], cos, sin))
        cols = _columns([_rotary(rk[:, hs], cos, sin) * HEAD_DIM ** -0.5])
        for bi in range(nb):
            kcol[bi, h] = cols[bi][:, 0:1]
    S1 = {}
    for bi, h in pairs:
        hs = slice(h * HEAD_DIM, (h + 1) * HEAD_DIM)
        S1[bi, h] = math.exp(_ret_log_gamma(h)) * sr_ref[bi, h] + kcol[bi, h] * rv[bi:bi + 1, hs]
        sr_o_ref[bi, h] = S1[bi, h]
    o = _rows_of({(bi, h): _mm(qr[h][bi:bi + 1, :], S1[bi, h]) for bi, h in pairs})
    yd_ref[...] = _groupnorm_heads(o, rgw_ref[...], rgb_ref[...]) * _silu(rg)


def _sample_xattn_kernel(q_ref, ck_ref, cv_ref, o_ref):
    for bi in range(q_ref.shape[0]):
        q = q_ref[bi]
        s = jnp.sum(ck_ref[bi] * q[None, :, :], axis=-1, keepdims=True) * HEAD_DIM ** -0.5
        e = jnp.exp(s - jnp.max(s, axis=0, keepdims=True))
        p = e / jnp.sum(e, axis=0, keepdims=True)
        o_ref[bi] = jnp.sum(p * cv_ref[bi], axis=0)


def _sample_xattn(layer, q, ck, cv):
    nb = SAMPLE_BLOCK
    cache_spec = pl.BlockSpec((None, nb) + ck.shape[2:], lambda i: (layer, i, 0, 0, 0))
    q_spec = pl.BlockSpec((nb,) + q.shape[1:], lambda i: (i, 0, 0))
    return pl.pallas_call(
        _sample_xattn_kernel,
        grid=(q.shape[0] // nb,),
        in_specs=[q_spec, cache_spec, cache_spec],
        out_specs=q_spec,
        out_shape=jax.ShapeDtypeStruct(q.shape, f32),
        compiler_params=pltpu.CompilerParams(dimension_semantics=("arbitrary",), vmem_limit_bytes=VMEM_LIMIT),
        name="sample_xattn",
    )(q, ck, cv)


def _sample_branches(layer, u, h0, lconv_t, gconv_t, sg, sh, sr, params, prev_states):
    nb = SAMPLE_BLOCK
    rows = u.shape[0]
    blk2 = lambda a: pl.BlockSpec((nb, a.shape[1]), lambda i: (i, 0))
    blk_t = lambda a: pl.BlockSpec((a.shape[0], nb, a.shape[2]), lambda i: (0, i, 0))
    blk_layer = lambda a: pl.BlockSpec((None, nb) + a.shape[2:],
                                       lambda i, n=a.ndim: (layer, i) + (0,) * (n - 2))
    full = lambda a: pl.BlockSpec(a.shape, lambda i, n=a.ndim: (0,) * n)
    sds = lambda a: jax.ShapeDtypeStruct(a.shape, f32)
    y_shape = jax.ShapeDtypeStruct((rows, W_BR), f32)
    y_spec = pl.BlockSpec((nb, W_BR), lambda i: (i, 0))
    n_in = N_SAMPLE_INPUTS
    assert 7 + len(params) == n_in
    n_small_outs = 7
    return pl.pallas_call(
        functools.partial(_sample_kernel, layer, len(prev_states)),
        grid=(rows // nb,),
        in_specs=[blk2(u), blk2(h0), blk_t(lconv_t), blk_t(gconv_t), blk_layer(sg), blk_layer(sh), blk_layer(sr)]
        + [full(p) for p in params] + [pl.BlockSpec(memory_space=pl.ANY)] * len(prev_states),
        out_specs=[y_spec] * 4 + [blk2(h0), blk_t(lconv_t), blk_t(gconv_t), blk_layer(sg), blk_layer(sh), blk_layer(sr)],
        out_shape=[y_shape] * 4 + [sds(h0), sds(lconv_t), sds(gconv_t), sds(sg), sds(sh), sds(sr)],
        input_output_aliases={n_in + j: n_small_outs + j for j in range(len(prev_states))},
        compiler_params=pltpu.CompilerParams(dimension_semantics=("arbitrary",), vmem_limit_bytes=VMEM_LIMIT),
        name="sample_branches",
    )(u, h0, lconv_t, gconv_t, sg, sh, sr, *params, *prev_states)


def kernel(x_prompt, x_sample, mem_prompt, state_lru_h, state_lru_conv, state_gdn_conv, state_gdn_s, state_hgrn_s, state_ret_s, cache_mem_k, cache_mem_v, w_in, lru_conv_w, lru_conv_b, lru_wa, lru_ba, lru_wi, lru_bi, lru_lambda, gdn_conv_w, gdn_a_log, gdn_dt_bias, gdn_norm_w, hgrn_lb_raw, hgrn_norm_w, ret_gn_w, ret_gn_b, w_mem_k, w_mem_v, w_merge_gate, b_merge_gate, w_branch, w_out, ln1_g, ln1_b, w_ffn_up, w_ffn_down, ln2_g, ln2_b):
    B, T, D = x_prompt.shape
    Bs = x_sample.shape[0]
    depth = w_in.shape[0]
    n_mem = mem_prompt.shape[1]
    W, H, HD = W_BR, N_HEADS, HEAD_DIM
    d_ff = w_ffn_down.shape[1]
    alpha = (2 * depth) ** 0.25
    assert Bs % SAMPLE_BLOCK == 0 and (B * T) % ROW_TILE == 0

    cos_all, sin_all = _rope_tables(list(range(T)) + [PAST_LEN])
    cos_p, sin_p = cos_all[:T], sin_all[:T]
    cos_s, sin_s = cos_all[T:T + 1], sin_all[T:T + 1]

    row = lambda a: a.reshape(1, -1)
    state_spec = ((H, HD, HD), f32)
    state_scratch = pltpu.VMEM((H, HD, HD), f32)

    xp = x_prompt
    xs = x_sample.reshape(Bs, D)
    mk_all, mv_all, mkb_all, mvb_all = _memkv(mem_prompt.reshape(B * n_mem, D), w_mem_k, w_mem_v)
    outs_p = [[] for _ in range(6)]
    outs_s = [[] for _ in range(3)]
    sample_states = ()

    for l in range(depth):
        wl = w_in[l]
        o = 0
        pieces = []
        for width in (W, 3 * W, W, H, H, W, W, W, W, W, W, W, W, W):
            pieces.append(wl[:, o:o + width])
            o += width
        (w_xa, w_qkv, w_z, w_gb, w_ga, w_hq, w_hf, w_hi, w_hg, w_rq, w_rk, w_rv, w_rg, w_xq) = pieces
        w_ba = jnp.concatenate([w_gb, w_ga, jnp.zeros((D, HD - 2 * H), f32)], 1).astype(bf16)
        w_xa, w_qkv, w_z, w_xq = (a.astype(bf16) for a in (w_xa, w_qkv, w_z, w_xq))
        w_hgrn = jnp.concatenate([w_hq, w_hf, w_hi, w_hg], 1).astype(bf16)
        w_ret = jnp.concatenate([w_rq, w_rk, w_rv, w_rg], 1).astype(bf16)

        lru_params = (lru_conv_w[l], row(lru_conv_b[l]), lru_wa[l].astype(bf16), row(lru_ba[l]),
                      lru_wi[l].astype(bf16), row(lru_bi[l]), row(lru_lambda[l]))
        pad_lanes = lambda a, start: jnp.zeros((1, HD), f32).at[0, start:start + H].set(a)
        alog_row, dtb_row = pad_lanes(gdn_a_log[l], H), pad_lanes(gdn_dt_bias[l], H)
        gdn_params = (gdn_conv_w[l], alog_row, dtb_row, row(gdn_norm_w[l]))
        hgrn_params = (hgrn_lb_raw, row(hgrn_norm_w[l]))
        ret_params = (row(ret_gn_w[l]), row(ret_gn_b[l]))
        merge_consts = (w_merge_gate[l].astype(bf16), b_merge_gate[l].reshape(N_BR, 1, D), w_branch[l].astype(bf16),
                        w_out[l].astype(bf16), row(ln1_g[l]), row(ln1_b[l]))
        ffn_consts = (w_ffn_up[l][:, :d_ff].astype(bf16), w_ffn_up[l][:, d_ff:].astype(bf16),
                      w_ffn_down[l].astype(bf16), row(ln2_g[l]), row(ln2_b[l]))
        merge_k = functools.partial(_merge_kernel, alpha)
        ffn_k = functools.partial(_ffn_kernel, alpha, 2)

        ya, h_last, lconv = _prompt_call(
            _lru_prompt_kernel, "lru_prompt", xp, T_TILE_WIDE, (w_xa,) + lru_params, (), (),
            [(W, bf16)], [((1, W), f32), ((CONV_W - 1, W), f32)],
            [pltpu.VMEM((CARRY_ROWS + T_TILE_WIDE, W), f32), pltpu.VMEM((T_TILE_WIDE, W), f32),
             pltpu.VMEM((T_TILE_WIDE, W), f32), pltpu.VMEM((8, W), f32)])
        yb, gconv, s_gdn = _prompt_call(
            _gdn_prompt_kernel, "gdn_prompt", xp, T_TILE, (w_qkv, w_z, w_ba) + gdn_params, (), (),
            [(W, bf16)], [((CONV_W - 1, 3 * W), f32), state_spec],
            [pltpu.VMEM((CARRY_ROWS + T_TILE, 3 * W), f32), state_scratch])
        yc, s_hgrn = _prompt_call(
            functools.partial(_hgrn_prompt_kernel, l), "hgrn_prompt", xp, T_TILE, (w_hgrn,) + hgrn_params, (), (),
            [(W, bf16)], [state_spec], [state_scratch])
        yd, s_ret = _prompt_call(
            _ret_prompt_kernel, "ret_prompt", xp, T_TILE, (w_ret,) + ret_params, (), (cos_p, sin_p),
            [(W, bf16)], [state_spec], [state_scratch])
        (ye,) = _prompt_call(
            _xattn_prompt_kernel, "xattn_prompt", xp, T_TILE_WIDE, (w_xq,),
            (mkb_all[l].reshape(B, n_mem, W), mvb_all[l].reshape(B, n_mem, W)), (), [(W, bf16)], [], [])
        x2d = xp.reshape(B * T, D)
        ys = [y.reshape(B * T, W) for y in (ya, yb, yc, yd, ye)]
        x2d = _row_call(merge_k, "merge_prompt", [x2d] + ys, merge_consts, D)
        x2d = _row_call(ffn_k, "ffn_prompt", [x2d], ffn_consts, D)
        xp = x2d.reshape(B, T, D)
        for lst, val in zip(outs_p, (h_last.reshape(B, W), lconv, gconv, s_gdn, s_hgrn, s_ret)):
            lst.append(val)

        u_s = _sample_proj(xs, (w_xa, w_qkv, w_z, w_hgrn, w_ret, w_xq, w_ba))
        sample_params = lru_params + gdn_params + hgrn_params + ret_params + (cos_s, sin_s)
        res = _sample_branches(
            l, u_s, state_lru_h[l], state_lru_conv[l].transpose(1, 0, 2), state_gdn_conv[l].transpose(1, 0, 2),
            state_gdn_s, state_hgrn_s, state_ret_s, sample_params, sample_states)
        ys_s, (h1, lconv_s, gconv_s), sample_states = res[:4], res[4:7], tuple(res[7:])
        xq_off = sum(w.shape[1] for w in (w_xa, w_qkv, w_z, w_hgrn, w_ret))
        q_s = u_s[:, xq_off:xq_off + W].reshape(Bs, H, HD)
        ye_s = _sample_xattn(l, q_s, cache_mem_k, cache_mem_v).reshape(Bs, W)
        xs = _row_call(merge_k, "merge_sample", [xs] + list(ys_s) + [ye_s], merge_consts, D)
        xs = _row_call(ffn_k, "ffn_sample", [xs], ffn_consts, D)
        for lst, val in zip(outs_s, (h1, lconv_s.transpose(1, 0, 2), gconv_s.transpose(1, 0, 2))):
            lst.append(val)

    mem_shape = (depth, B, n_mem, H, HD)
    return (xp, xs.reshape(Bs, 1, D), *(jnp.stack(v) for v in outs_p), mk_all.reshape(mem_shape),
            mv_all.reshape(mem_shape), *(jnp.stack(v) for v in outs_s), *sample_states)
```

```python
import functools
import math

import numpy as np
import jax
import jax.numpy as jnp
from jax import lax
from jax.experimental import pallas as pl
from jax.experimental.pallas import tpu as pltpu

f32 = jnp.float32
bf16 = jnp.bfloat16

N_HEADS = 4
HEAD_DIM = 128
W_BR = N_HEADS * HEAD_DIM
CONV_W = 4
N_BR = 5
LRU_C = 8.0
GDN_CHUNK = 64
HGRN_CHUNK = 16
ROPE_BASE = 10000.0
LN_EPS = 1e-5
NORM_EPS = 1e-6
PAST_LEN = 16384

T_TILE = 256
T_TILE_WIDE = 512
GDN_SEQS = 2
ROW_TILE = 512
SAMPLE_BLOCK = 8
CARRY_ROWS = 8
VMEM_LIMIT = 52 * 1024 * 1024


def _mm(a, b):
    return jnp.dot(a.astype(bf16), b.astype(bf16), preferred_element_type=f32)


def _mm_nt(a, b):
    return lax.dot_general(a.astype(bf16), b.astype(bf16), (((1,), (1,)), ((), ())), preferred_element_type=f32)


def _mm_tn(a, b):
    return lax.dot_general(a.astype(bf16), b.astype(bf16), (((0,), (0,)), ((), ())), preferred_element_type=f32)


def _mask_mm_exact(mask_bf16, x):
    hi = x.astype(bf16)
    r1 = x - hi.astype(f32)
    mid = r1.astype(bf16)
    lo = (r1 - mid.astype(f32)).astype(bf16)
    dot = functools.partial(jnp.dot, preferred_element_type=f32)
    return dot(mask_bf16, hi) + dot(mask_bf16, mid) + dot(mask_bf16, lo)


def _chunk_masks(n, chunk):
    shift = int(math.log2(chunk))
    row = lax.broadcasted_iota(jnp.int32, (n, n), 0)
    col = lax.broadcasted_iota(jnp.int32, (n, n), 1)
    same = (row >> shift) == (col >> shift)
    return jnp.logical_and(same, col <= row), same


def _as_bf16(mask):
    return jnp.where(mask, 1.0, 0.0).astype(bf16)


def _sigmoid(x):
    return jax.nn.sigmoid(x)


def _silu(x):
    return x * jax.nn.sigmoid(x)


def _layernorm(x, g, b):
    mu = jnp.mean(x, -1, keepdims=True)
    xc = x - mu
    var = jnp.mean(xc * xc, -1, keepdims=True)
    return xc * lax.rsqrt(var + LN_EPS) * g + b


def _head_slices(a):
    return [a[:, h * HEAD_DIM:(h + 1) * HEAD_DIM] for h in range(N_HEADS)]


def _causal_conv(buf_ref, x, cw):
    n = x.shape[0]
    buf_ref[CARRY_ROWS:CARRY_ROWS + n, :] = x
    out = cw[CONV_W - 1:CONV_W, :] * x
    for j in range(CONV_W - 1):
        off = CARRY_ROWS - (CONV_W - 1) + j
        out = out + cw[j:j + 1, :] * buf_ref[off:off + n, :]
    buf_ref[0:CARRY_ROWS, :] = buf_ref[n:n + CARRY_ROWS, :]
    return out


def _conv_state(buf_ref):
    return buf_ref[CARRY_ROWS - (CONV_W - 1):CARRY_ROWS, :]


def _lru_gates(xc, wa_ref, ba, wi_ref, bi, lam):
    xh = _head_slices(xc)
    r = _sigmoid(jnp.concatenate([_mm(xh[h], wa_ref[h]) for h in range(N_HEADS)], -1) + ba)
    ig = _sigmoid(jnp.concatenate([_mm(xh[h], wi_ref[h]) for h in range(N_HEADS)], -1) + bi)
    log_a = -LRU_C * r * jax.nn.softplus(-lam)
    a = jnp.exp(log_a)
    b = jnp.sqrt(jnp.tanh(-log_a) * (1.0 + a * a)) * (ig * xc)
    return a, b


def _gdn_gates(ba, alog_row, dtb_row):
    beta = _sigmoid(ba)
    g = -jnp.exp(alog_row) * jax.nn.softplus(ba + dtb_row)
    return beta, g


def _l2norm(a):
    return a * lax.rsqrt(jnp.sum(a * a, -1, keepdims=True) + NORM_EPS)


def _rmsnorm_heads(o_heads, w):
    parts = [o * lax.rsqrt(jnp.mean(o * o, -1, keepdims=True) + NORM_EPS) for o in o_heads]
    return jnp.concatenate(parts, -1) * w


def _groupnorm_heads(o_heads, w, b):
    parts = []
    for o in o_heads:
        mu = jnp.mean(o, -1, keepdims=True)
        oc = o - mu
        var = jnp.mean(oc * oc, -1, keepdims=True)
        parts.append(oc * lax.rsqrt(var + NORM_EPS))
    return jnp.concatenate(parts, -1) * w + b


def _hgrn_lower_bound(lbraw, layer):
    n = lbraw.shape[0]
    rows = [lbraw[i:i + 1, :] for i in range(n)]
    m = rows[0]
    for r in rows[1:]:
        m = jnp.maximum(m, r)
    es = [jnp.exp(r - m) for r in rows]
    tot = es[0]
    for e in es[1:]:
        tot = tot + e
    acc = jnp.zeros_like(m)
    for i in range(1, layer + 1):
        acc = acc + es[i] / tot
    return acc


def _hgrn_gates(fz, lb):
    log_f = jnp.logaddexp(jnp.log(lb), jnp.log1p(-lb) + jax.nn.log_sigmoid(fz))
    k = (1.0 - lb) * _sigmoid(-fz)
    return log_f, k


def _rotary(x, cos, sin_signed):
    return x * cos + pltpu.roll(x, HEAD_DIM // 2, 1) * sin_signed


def _ret_log_gamma(h):
    return math.log1p(-(2.0 ** (-5.0 - h)))


def _rope_kernel(ang_ref, cos_ref, sin_ref):
    ang = ang_ref[...]
    lane = lax.broadcasted_iota(jnp.int32, ang.shape, 1)
    s = jnp.sin(ang)
    cos_ref[...] = jnp.cos(ang)
    sin_ref[...] = jnp.where(lane < HEAD_DIM // 2, -s, s)


def _rope_tables(positions):
    half = HEAD_DIM // 2
    inv = ROPE_BASE ** (-np.arange(half, dtype=np.float64) / half)
    ang = np.mod(np.asarray(positions, np.float64)[:, None] * inv[None, :], 2.0 * np.pi)
    ang = np.concatenate([ang, ang], axis=1).astype(np.float32)
    n = ang.shape[0]
    pad = (-n) % 8
    ang = np.pad(ang, ((0, pad), (0, 0)))
    shp = jax.ShapeDtypeStruct(ang.shape, f32)
    cos, sin = pl.pallas_call(_rope_kernel, out_shape=(shp, shp), name="rope_tables")(jnp.asarray(ang))
    return cos[:n], sin[:n]


def _memkv_kernel(m_ref, wk_ref, wv_ref, k_ref, v_ref, kb_ref, vb_ref):
    m = m_ref[...]
    k = _mm(m, wk_ref[...])
    v = _mm(m, wv_ref[...])
    k_ref[...] = k
    v_ref[...] = v
    kb_ref[...] = k.astype(bf16)
    vb_ref[...] = v.astype(bf16)


def _memkv(mem2d, wk, wv):
    rows, d = mem2d.shape
    depth = wk.shape[0]
    tile = 256
    w_spec = pl.BlockSpec((None, d, W_BR), lambda l, i: (l, 0, 0))
    o_spec = pl.BlockSpec((None, tile, W_BR), lambda l, i: (l, i, 0))
    return pl.pallas_call(
        _memkv_kernel,
        grid=(depth, rows // tile),
        in_specs=[pl.BlockSpec((tile, d), lambda l, i: (i, 0)), w_spec, w_spec],
        out_specs=[o_spec] * 4,
        out_shape=[jax.ShapeDtypeStruct((depth, rows, W_BR), f32)] * 2
        + [jax.ShapeDtypeStruct((depth, rows, W_BR), bf16)] * 2,
        compiler_params=pltpu.CompilerParams(dimension_semantics=("arbitrary", "arbitrary"),
                                             vmem_limit_bytes=VMEM_LIMIT),
        name="mem_kv",
    )(mem2d, wk, wv)


def _prompt_call(kernel, name, x, tile, consts, per_batch, per_time, outs_tile, outs_batch, scratch, seqs=1):
    B, T, D = x.shape
    assert T % tile == 0 and B % seqs == 0
    nt = T // tile
    in_specs = [pl.BlockSpec((seqs, tile, D), lambda b, t: (b, t, 0))]
    for a in consts:
        in_specs.append(pl.BlockSpec(a.shape, lambda b, t, n=a.ndim: (0,) * n))
    for a in per_batch:
        in_specs.append(pl.BlockSpec((seqs,) + a.shape[1:], lambda b, t, n=a.ndim: (b,) + (0,) * (n - 1)))
    for a in per_time:
        in_specs.append(pl.BlockSpec((tile,) + a.shape[1:], lambda b, t, n=a.ndim: (t,) + (0,) * (n - 1)))
    out_specs, out_shape = [], []
    for width, dt in outs_tile:
        out_specs.append(pl.BlockSpec((seqs, tile, width), lambda b, t: (b, t, 0)))
        out_shape.append(jax.ShapeDtypeStruct((B, T, width), dt))
    for shp, dt in outs_batch:
        out_specs.append(pl.BlockSpec((seqs,) + shp, lambda b, t, n=len(shp): (b,) + (0,) * n))
        out_shape.append(jax.ShapeDtypeStruct((B,) + shp, dt))
    return pl.pallas_call(
        kernel,
        grid=(B // seqs, nt),
        in_specs=in_specs,
        out_specs=out_specs,
        out_shape=out_shape,
        scratch_shapes=scratch,
        compiler_params=pltpu.CompilerParams(dimension_semantics=("arbitrary", "arbitrary"),
                                             vmem_limit_bytes=VMEM_LIMIT),
        name=name,
    )(x, *consts, *per_batch, *per_time)


def _lru_prompt_kernel(x_ref, w_ref, cw_ref, cb_ref, wa_ref, ba_ref, wi_ref, bi_ref, lam_ref,
                       y_ref, h_ref, conv_ref, buf_ref, a_ref, b_ref, hcar_ref):
    n = x_ref.shape[1]

    @pl.when(pl.program_id(1) == 0)
    def _():
        buf_ref[0:CARRY_ROWS, :] = jnp.zeros((CARRY_ROWS, W_BR), f32)
        hcar_ref[...] = jnp.zeros_like(hcar_ref)

    xa = _mm(x_ref[0], w_ref[...])
    xc = _causal_conv(buf_ref, xa, cw_ref[...]) + cb_ref[...]
    a, b = _lru_gates(xc, wa_ref, ba_ref[...], wi_ref, bi_ref[...], lam_ref[...])

    a = a.reshape(n // 8, 8, W_BR)
    b = b.reshape(n // 8, 8, W_BR)
    row8 = lax.broadcasted_iota(jnp.int32, (n // 8, 8, W_BR), 1)
    for d in (1, 2, 4):
        keep = row8 >= d
        a_prev = jnp.where(keep, pltpu.roll(a, d, 1), 1.0)
        b_prev = jnp.where(keep, pltpu.roll(b, d, 1), 0.0)
        b = b + a * b_prev
        a = a * a_prev
    a_ref[...] = a.reshape(n, W_BR)
    b_ref[...] = b.reshape(n, W_BR)

    def group(gi, h_prev):
        i = pl.multiple_of(gi * 8, 8)
        blk = b_ref[pl.ds(i, 8), :] + a_ref[pl.ds(i, 8), :] * h_prev
        b_ref[pl.ds(i, 8), :] = blk
        return jnp.broadcast_to(blk[7:8, :], (8, W_BR))

    h_last = lax.fori_loop(0, n // 8, group, hcar_ref[...])
    hcar_ref[...] = h_last
    y_ref[0] = b_ref[...].astype(y_ref.dtype)
    h_ref[0] = h_last[0:1, :]
    conv_ref[0] = _conv_state(buf_ref)


def _gdn_prompt_kernel(x_ref, wqkv_ref, wz_ref, wba_ref, cw_ref, alog_ref, dtb_ref, nw_ref,
                       y_ref, conv_ref, s_ref, buf_ref, st_ref):
    seqs, n_seq, C = x_ref.shape[0], x_ref.shape[1], GDN_CHUNK
    assert n_seq == N_HEADS * C
    n = seqs * n_seq

    @pl.when(pl.program_id(1) == 0)
    def _():
        buf_ref[:, 0:CARRY_ROWS, :] = jnp.zeros((seqs, CARRY_ROWS, 3 * W_BR), f32)
        st_ref[...] = jnp.zeros_like(st_ref)

    x = jnp.concatenate([x_ref[s] for s in range(seqs)], 0).astype(bf16)
    qkv = _mm(x, wqkv_ref[...])
    c = _silu(jnp.concatenate([_causal_conv(buf_ref.at[s], qkv[s * n_seq:(s + 1) * n_seq], cw_ref[...])
                               for s in range(seqs)], 0))
    z = _mm(x, wz_ref[...])
    beta, g = _gdn_gates(_mm(x, wba_ref[...]), alog_ref[...], dtb_ref[...])

    incl_t = _as_bf16(_chunk_masks(n_seq, C)[0])
    G_seq = [_mask_mm_exact(incl_t, g[s * n_seq:(s + 1) * n_seq]) for s in range(seqs)]
    G = jnp.concatenate(G_seq, 0)
    GT = [a.T for a in G_seq]
    H, HD, W = N_HEADS, HEAD_DIM, W_BR
    nck = n // C
    nck_seq = n_seq // C
    heads = range(H)
    tile = lambda a, h: a[:, h * HD:(h + 1) * HD]

    q_all = jnp.concatenate([_l2norm(tile(c, h)) * HD ** -0.5 for h in heads], 1)
    k_all = jnp.concatenate([_l2norm(tile(c, H + h)) for h in heads], 1)
    v_all = c[:, 2 * W:]
    per_head = lambda a, lo: jnp.concatenate([jnp.broadcast_to(a[:, lo + h:lo + h + 1], (n, HD)) for h in heads], 1)
    beta_w = per_head(beta, 0)
    g_w = per_head(G, H)
    glast_w = jnp.concatenate([jnp.broadcast_to(g_w[(ck + 1) * C - 1:(ck + 1) * C, :], (C, W)) for ck in range(nck)], 0)
    eg_w = jnp.exp(g_w)
    kb_all = k_all * beta_w
    vb_all = v_all * beta_w
    kbe_all = kb_all * eg_w
    qe_all = q_all * eg_w
    kend_all = k_all * jnp.exp(glast_w - g_w)
    eglast_w = jnp.exp(glast_w)

    lane_blk = lax.broadcasted_iota(jnp.int32, (n, H * C), 1) >> int(math.log2(C))
    gcol_c = G[:, H + H - 1:H + H]
    for h in reversed(range(H - 1)):
        gcol_c = jnp.where(lane_blk == h, G[:, H + h:H + h + 1], gcol_c)
    ri = lax.broadcasted_iota(jnp.int32, (C, H * C), 0)
    lj = lax.broadcasted_iota(jnp.int32, (C, H * C), 1)
    incl, strict = (lj & (C - 1)) <= ri, (lj & (C - 1)) < ri
    row_blk = lax.broadcasted_iota(jnp.int32, (1, H * C), 1) >> int(math.log2(C))
    bd_mask = (lax.broadcasted_iota(jnp.int32, (H * C, H * C), 0) >> int(math.log2(C))) == (
        lax.broadcasted_iota(jnp.int32, (H * C, H * C), 1) >> int(math.log2(C)))

    def block_diag(parts, width):
        zero = jnp.zeros((parts[0].shape[0], width), parts[0].dtype)
        return jnp.concatenate([jnp.concatenate([parts[h] if h == hp else zero for h in heads], 1) for hp in heads], 0)

    def block_diag_small(m_b):
        return jnp.where(bd_mask, jnp.concatenate([m_b] * H, 0), jnp.zeros((), bf16))

    neg_a, a_qk, rhs_w, rhs_bd = [], [], [], []
    for ck in range(nck):
        sl = slice(ck * C, (ck + 1) * C)
        gt, cs = GT[ck // nck_seq], ck % nck_seq
        g_row = None
        for h in reversed(heads):
            r = gt[H + h:H + h + 1, :]
            r = pltpu.roll(r, ((h - cs) * C) % n_seq, 1) if (h - cs) % nck_seq else r
            g_row = r if g_row is None else jnp.where(row_blk == h, r, g_row)
        decay = jnp.where(incl, jnp.exp(gcol_c[sl] - g_row), 0.0)
        k_b = k_all[sl].astype(bf16)
        k_bd = block_diag([tile(k_b, h) for h in heads], HD)
        neg_a.append(jnp.where(strict, -(_mm_nt(kb_all[sl], k_bd) * decay), 0.0))
        a_qk.append(_mm_nt(q_all[sl], k_bd) * decay)
        parts = [jnp.concatenate([tile(vb_all[sl], h), tile(kbe_all[sl], h)], 1) for h in heads]
        rhs_w.append(jnp.concatenate(parts, 1))
        rhs_bd.append(block_diag([p.astype(bf16) for p in parts], 2 * HD))

    dot = functools.partial(jnp.dot, preferred_element_type=f32)
    m = list(neg_a)
    t_off = list(neg_a)
    m_bd = [block_diag_small(a.astype(bf16)) for a in m]
    for _ in range(int(math.log2(C)) - 1):
        m = [dot(m[ck].astype(bf16), m_bd[ck]) for ck in range(nck)]
        m_bd = [block_diag_small(a.astype(bf16)) for a in m]
        t_off = [t_off[ck] + m[ck] + dot(t_off[ck].astype(bf16), m_bd[ck]) for ck in range(nck)]
    sol = [rhs_w[ck] + dot(t_off[ck].astype(bf16), rhs_bd[ck]) for ck in range(nck)]

    S = [st_ref[s] for s in range(seqs)]
    zero_s = jnp.zeros((HD, HD), bf16)
    o_chunks = [None] * nck
    for cs in range(nck_seq):
        for s in range(seqs):
            ck = s * nck_seq + cs
            sl = slice(ck * C, (ck + 1) * C)
            S_b = S[s].astype(bf16)
            s_pairs = [jnp.concatenate([jnp.concatenate([tile(S_b, p), zero_s], 1),
                                        jnp.concatenate([zero_s, tile(S_b, p + 1)], 1)], 0) for p in (0, 2)]
            u = [sol[ck][:, 2 * h * HD:(2 * h + 1) * HD] for h in heads]
            w = [sol[ck][:, (2 * h + 1) * HD:(2 * h + 2) * HD].astype(bf16) for h in heads]
            w_s = jnp.concatenate([dot(jnp.concatenate([w[p], w[p + 1]], 1), s_pairs[p // 2]) for p in (0, 2)], 1)
            v_new = jnp.concatenate(u, 1) - w_s
            v_b = v_new.astype(bf16)
            v_bd = block_diag([tile(v_b, h) for h in heads], HD)
            qe_b = qe_all[sl].astype(bf16)
            q_s = jnp.concatenate([dot(qe_b[:, p * HD:(p + 2) * HD], s_pairs[p // 2]) for p in (0, 2)], 1)
            o_chunks[ck] = q_s + dot(a_qk[ck].astype(bf16), v_bd)
            kend_stack = jnp.concatenate([tile(kend_all[sl], h) for h in heads], 0)
            S[s] = eglast_w[ck * C:ck * C + 1, :] * S[s] + _mm_tn(kend_stack, v_bd)
    o_all = jnp.concatenate(o_chunks, 0)
    y = _rmsnorm_heads([tile(o_all, h) for h in heads], nw_ref[...]) * _silu(z)
    for s in range(seqs):
        st_ref[s] = S[s]
        for h in heads:
            s_ref[s, h] = tile(S[s], h)
        y_ref[s] = y[s * n_seq:(s + 1) * n_seq].astype(y_ref.dtype)
        conv_ref[s] = _conv_state(buf_ref.at[s])


def _hgrn_prompt_kernel(layer, x_ref, w_ref, lbraw_ref, nw_ref, y_ref, s_ref, st_ref):
    n, C = x_ref.shape[1], HGRN_CHUNK

    @pl.when(pl.program_id(1) == 0)
    def _():
        st_ref[...] = jnp.zeros_like(st_ref)

    u = _mm(x_ref[0], w_ref[...])
    hq, fz, hi, hg = (u[:, i * W_BR:(i + 1) * W_BR] for i in range(4))
    lb = _hgrn_lower_bound(lbraw_ref[...], layer)
    log_f, k = _hgrn_gates(fz, lb)
    q = _silu(hq)

    incl, _ = _chunk_masks(n, C)
    G = _mask_mm_exact(_as_bf16(incl), log_f)
    g_last = [G[(ck + 1) * C - 1:(ck + 1) * C, :] for ck in range(n // C)]
    Gtot = jnp.concatenate([jnp.broadcast_to(r, (C, W_BR)) for r in g_last], 0)
    q_g = q * jnp.exp(G)
    k_g = k * jnp.exp(-G)
    k_end = k * jnp.exp(Gtot - G)
    dT = jnp.exp(jnp.concatenate(g_last, 0)).T

    o_heads = []
    for h in range(N_HEADS):
        hs = slice(h * HEAD_DIM, (h + 1) * HEAD_DIM)
        a_qk = jnp.where(incl, _mm_nt(q_g[:, hs], k_g[:, hs]), 0.0)
        o_intra = _mm(a_qk, hi[:, hs])
        S = st_ref[h]
        o_inter = []
        for ck in range(n // C):
            sl = slice(ck * C, (ck + 1) * C)
            o_inter.append(_mm(q_g[sl, hs], S))
            S = dT[hs, ck:ck + 1] * S + _mm_tn(k_end[sl, hs], hi[sl, hs])
        st_ref[h] = S
        s_ref[0, h] = S
        o_heads.append(o_intra + jnp.concatenate(o_inter, 0))

    y = _rmsnorm_heads(o_heads, nw_ref[...]) * _sigmoid(hg)
    y_ref[0] = y.astype(y_ref.dtype)


def _ret_prompt_kernel(x_ref, w_ref, gw_ref, gb_ref, cos_ref, sin_ref, y_ref, s_ref, st_ref):
    n = x_ref.shape[1]

    @pl.when(pl.program_id(1) == 0)
    def _():
        st_ref[...] = jnp.zeros_like(st_ref)

    xb = x_ref[0].astype(bf16)

    def project(pair):
        lo = 2 * pair * HEAD_DIM
        return [_mm(xb, w_ref[:, i * W_BR + lo:i * W_BR + lo + 2 * HEAD_DIM]) for i in range(4)]

    cos, sin = cos_ref[...], sin_ref[...]
    ri = lax.broadcasted_iota(jnp.int32, (n, n), 0)
    rj = lax.broadcasted_iota(jnp.int32, (n, n), 1)
    dist = (ri - rj).astype(f32)
    pos = lax.broadcasted_iota(jnp.int32, (n, 1), 0).astype(f32)

    nxt = project(0)
    for h in range(N_HEADS):
        hs = slice(h * HEAD_DIM, (h + 1) * HEAD_DIM)
        if h % 2 == 0:
            cur = nxt
            if h + 2 < N_HEADS:
                nxt = project(h // 2 + 1)
        ps = slice((h % 2) * HEAD_DIM, (h % 2 + 1) * HEAD_DIM)
        rq, rk, v, rg = (a[:, ps] for a in cur)
        lg = _ret_log_gamma(h)
        q = _rotary(rq, cos, sin)
        k = _rotary(rk, cos, sin) * HEAD_DIM ** -0.5
        decay = jnp.where(rj <= ri, jnp.exp(dist * lg), 0.0)
        S = st_ref[h]
        o = _mm(_mm_nt(q, k) * decay, v) + _mm(q * jnp.exp((pos + 1.0) * lg), S)
        S = math.exp(n * lg) * S + _mm_tn(k * jnp.exp((n - 1.0 - pos) * lg), v)
        st_ref[h] = S
        s_ref[0, h] = S
        y = _groupnorm_heads([o], gw_ref[:, hs], gb_ref[:, hs]) * _silu(rg)
        y_ref[0, :, hs] = y.astype(y_ref.dtype)


def _xattn_prompt_kernel(x_ref, w_ref, mk_ref, mv_ref, y_ref):
    q = _mm(x_ref[0], w_ref[...])
    mk, mv = mk_ref[0], mv_ref[0]
    outs = []
    for h in range(N_HEADS):
        hs = slice(h * HEAD_DIM, (h + 1) * HEAD_DIM)
        s = _mm_nt(q[:, hs], mk[:, hs]) * HEAD_DIM ** -0.5
        e = jnp.exp(s - jnp.max(s, -1, keepdims=True))
        p = e / jnp.sum(e, -1, keepdims=True)
        outs.append(_mm(p, mv[:, hs]))
    y_ref[0] = jnp.concatenate(outs, -1).astype(y_ref.dtype)


def _merge_kernel(alpha, x_ref, ya_ref, yb_ref, yc_ref, yd_ref, ye_ref, wg_ref, bg_ref, wb_ref, wo_ref,
                  g_ref, b_ref, o_ref):
    x = x_ref[...]
    xb = x.astype(bf16)
    merged = None
    for i, y_ref in enumerate((ya_ref, yb_ref, yc_ref, yd_ref, ye_ref)):
        gate = _sigmoid(_mm(xb, wg_ref[i]) + bg_ref[i])
        term = gate * _mm(y_ref[...], wb_ref[i])
        merged = term if merged is None else merged + term
    mix = _mm(merged, wo_ref[...])
    o_ref[...] = _layernorm(alpha * x + mix, g_ref[...], b_ref[...])


def _ffn_kernel(alpha, n_split, x_ref, wg_ref, wv_ref, wd_ref, g_ref, b_ref, o_ref):
    x = x_ref[...]
    xb = x.astype(bf16)
    d_ff = wg_ref.shape[1]
    step = d_ff // n_split
    ffn = None
    for i in range(n_split):
        cs = slice(i * step, (i + 1) * step)
        hidden = _silu(_mm(xb, wg_ref[:, cs])) * _mm(xb, wv_ref[:, cs])
        part = _mm(hidden, wd_ref[cs, :])
        ffn = part if ffn is None else ffn + part
    o_ref[...] = _layernorm(alpha * x + ffn, g_ref[...], b_ref[...])


def _row_call(kernel, name, row_ops, consts, d_out):
    rows = row_ops[0].shape[0]
    tile = min(ROW_TILE, rows)
    in_specs = [pl.BlockSpec((tile, a.shape[1]), lambda i: (i, 0)) for a in row_ops]
    in_specs += [pl.BlockSpec(a.shape, lambda i, n=a.ndim: (0,) * n, pipeline_mode=pl.Buffered(1)) for a in consts]
    return pl.pallas_call(
        kernel,
        grid=(rows // tile,),
        in_specs=in_specs,
        out_specs=pl.BlockSpec((tile, d_out), lambda i: (i, 0)),
        out_shape=jax.ShapeDtypeStruct((rows, d_out), f32),
        compiler_params=pltpu.CompilerParams(dimension_semantics=("arbitrary",), vmem_limit_bytes=VMEM_LIMIT),
        name=name,
    )(*row_ops, *consts)


def _proj_kernel(x_ref, *refs):
    w_refs, o_ref = refs[:-1], refs[-1]
    x = x_ref[...].astype(bf16)
    off = 0
    for w_ref in w_refs:
        width = w_ref.shape[1]
        o_ref[:, off:off + width] = _mm(x, w_ref[...])
        off += width


def _sample_proj(x, weights):
    rows = x.shape[0]
    n = sum(w.shape[1] for w in weights)
    return pl.pallas_call(
        _proj_kernel,
        out_shape=jax.ShapeDtypeStruct((rows, n), f32),
        compiler_params=pltpu.CompilerParams(vmem_limit_bytes=VMEM_LIMIT),
        name="sample_proj",
    )(x, *weights)


def _columns(rows_list):
    nb = rows_list[0].shape[0]
    out = []
    for b in range(nb):
        tile = jnp.concatenate([r[b:b + 1, :] for r in rows_list]
                               + [jnp.zeros((8 - len(rows_list), HEAD_DIM), f32)], 0)
        out.append(tile.T)
    return out


N_SAMPLE_INPUTS = 24


def _rows_of(parts):
    nb = 1 + max(bi for bi, _ in parts)
    return [jnp.concatenate([parts[bi, h] for bi in range(nb)], 0) for h in range(N_HEADS)]


def _sample_kernel(layer, n_alias, *refs):
    (u_ref, h0_ref, lconv_ref, gconv_ref, sg_ref, sh_ref, sr_ref,
     lcw_ref, lcb_ref, lwa_ref, lba_ref, lwi_ref, lbi_ref, lam_ref,
     gcw_ref, alog_ref, dtb_ref, gnw_ref, lbraw_ref, hnw_ref, rgw_ref, rgb_ref, cos_ref, sin_ref) = refs[:N_SAMPLE_INPUTS]
    (ya_ref, yb_ref, yc_ref, yd_ref, h1_ref, lconv_o_ref, gconv_o_ref,
     sg_o_ref, sh_o_ref, sr_o_ref) = refs[N_SAMPLE_INPUTS + n_alias:]
    nb = SAMPLE_BLOCK
    W = W_BR
    pairs = [(bi, h) for bi in range(nb) for h in range(N_HEADS)]
    u = u_ref[...]
    off = 0

    def take(width):
        nonlocal off
        part = u[:, off:off + width]
        off += width
        return part

    xa, qkv, z = take(W), take(3 * W), take(W)
    hq, fz, hi, hg = take(W), take(W), take(W), take(W)
    rq, rk, rv, rg = take(W), take(W), take(W), take(W)
    xq, ba = take(W), take(HEAD_DIM)

    lcw = lcw_ref[...]
    xc = lcw[CONV_W - 1:CONV_W, :] * xa + lcb_ref[...]
    for j in range(CONV_W - 1):
        xc = xc + lcw[j:j + 1, :] * lconv_ref[j]
    for j in range(CONV_W - 2):
        lconv_o_ref[j] = lconv_ref[j + 1]
    lconv_o_ref[CONV_W - 2] = xa
    a, b = _lru_gates(xc, lwa_ref, lba_ref[...], lwi_ref, lbi_ref[...], lam_ref[...])
    h1 = a * h0_ref[...] + b
    h1_ref[...] = h1
    ya_ref[...] = h1

    gcw = gcw_ref[...]
    c = gcw[CONV_W - 1:CONV_W, :] * qkv
    for j in range(CONV_W - 1):
        c = c + gcw[j:j + 1, :] * gconv_ref[j]
    for j in range(CONV_W - 2):
        gconv_o_ref[j] = gconv_ref[j + 1]
    gconv_o_ref[CONV_W - 2] = qkv
    c = _silu(c)
    beta, g = _gdn_gates(ba, alog_ref[...], dtb_ref[...])
    eg = jnp.exp(g)
    qn, kn, vv, kcol = [], [], [], {}
    for h in range(N_HEADS):
        qn.append(_l2norm(c[:, h * HEAD_DIM:(h + 1) * HEAD_DIM]) * HEAD_DIM ** -0.5)
        kn.append(_l2norm(c[:, W + h * HEAD_DIM:W + (h + 1) * HEAD_DIM]))
        vv.append(c[:, 2 * W + h * HEAD_DIM:2 * W + (h + 1) * HEAD_DIM])
        cols = _columns([kn[h]])
        for bi in range(nb):
            kcol[bi, h] = cols[bi][:, 0:1]
    kS = _rows_of({(bi, h): _mm(kn[h][bi:bi + 1, :], sg_ref[bi, h]) for bi, h in pairs})
    v_new = [beta[:, h:h + 1] * (vv[h] - eg[:, N_HEADS + h:N_HEADS + h + 1] * kS[h]) for h in range(N_HEADS)]
    S1 = {}
    for bi, h in pairs:
        S1[bi, h] = (eg[bi:bi + 1, N_HEADS + h:N_HEADS + h + 1] * sg_ref[bi, h]
                     + kcol[bi, h] * v_new[h][bi:bi + 1, :])
        sg_o_ref[bi, h] = S1[bi, h]
    o = _rows_of({(bi, h): _mm(qn[h][bi:bi + 1, :], S1[bi, h]) for bi, h in pairs})
    yb_ref[...] = _rmsnorm_heads(o, gnw_ref[...]) * _silu(z)

    lb = _hgrn_lower_bound(lbraw_ref[...], layer)
    log_f, kk = _hgrn_gates(fz, lb)
    f = jnp.exp(log_f)
    qs = _silu(hq)
    kcol, fcol = {}, {}
    for h in range(N_HEADS):
        hs = slice(h * HEAD_DIM, (h + 1) * HEAD_DIM)
        cols = _columns([kk[:, hs], f[:, hs]])
        for bi in range(nb):
            kcol[bi, h], fcol[bi, h] = cols[bi][:, 0:1], cols[bi][:, 1:2]
    S1 = {}
    for bi, h in pairs:
        hs = slice(h * HEAD_DIM, (h + 1) * HEAD_DIM)
        S1[bi, h] = fcol[bi, h] * sh_ref[bi, h] + kcol[bi, h] * hi[bi:bi + 1, hs]
        sh_o_ref[bi, h] = S1[bi, h]
    o = _rows_of({(bi, h): _mm(qs[bi:bi + 1, h * HEAD_DIM:(h + 1) * HEAD_DIM], S1[bi, h]) for bi, h in pairs})
    yc_ref[...] = _rmsnorm_heads(o, hnw_ref[...]) * _sigmoid(hg)

    cos, sin = cos_ref[...], sin_ref[...]
    qr, kcol = [], {}
    for h in range(N_HEADS):
        hs = slice(h * HEAD_DIM, (h + 1) * HEAD_DIM)
        qr.append(_rotary(rq[:, hs], cos, sin))
        cols = _columns([_rotary(rk[:, hs], cos, sin) * HEAD_DIM ** -0.5])
        for bi in range(nb):
            kcol[bi, h] = cols[bi][:, 0:1]
    S1 = {}
    for bi, h in pairs:
        hs = slice(h * HEAD_DIM, (h + 1) * HEAD_DIM)
        S1[bi, h] = math.exp(_ret_log_gamma(h)) * sr_ref[bi, h] + kcol[bi, h] * rv[bi:bi + 1, hs]
        sr_o_ref[bi, h] = S1[bi, h]
    o = _rows_of({(bi, h): _mm(qr[h][bi:bi + 1, :], S1[bi, h]) for bi, h in pairs})
    yd_ref[...] = _groupnorm_heads(o, rgw_ref[...], rgb_ref[...]) * _silu(rg)


def _sample_xattn_kernel(q_ref, ck_ref, cv_ref, o_ref):
    for bi in range(q_ref.shape[0]):
        q = q_ref[bi]
        s = jnp.sum(ck_ref[bi] * q[None, :, :], axis=-1, keepdims=True) * HEAD_DIM ** -0.5
        e = jnp.exp(s - jnp.max(s, axis=0, keepdims=True))
        p = e / jnp.sum(e, axis=0, keepdims=True)
        o_ref[bi] = jnp.sum(p * cv_ref[bi], axis=0)


def _sample_xattn(layer, q, ck, cv):
    nb = SAMPLE_BLOCK
    cache_spec = pl.BlockSpec((None, nb) + ck.shape[2:], lambda i: (layer, i, 0, 0, 0))
    q_spec = pl.BlockSpec((nb,) + q.shape[1:], lambda i: (i, 0, 0))
    return pl.pallas_call(
        _sample_xattn_kernel,
        grid=(q.shape[0] // nb,),
        in_specs=[q_spec, cache_spec, cache_spec],
        out_specs=q_spec,
        out_shape=jax.ShapeDtypeStruct(q.shape, f32),
        compiler_params=pltpu.CompilerParams(dimension_semantics=("arbitrary",), vmem_limit_bytes=VMEM_LIMIT),
        name="sample_xattn",
    )(q, ck, cv)


def _sample_branches(layer, u, h0, lconv_t, gconv_t, sg, sh, sr, params, prev_states):
    nb = SAMPLE_BLOCK
    rows = u.shape[0]
    blk2 = lambda a: pl.BlockSpec((nb, a.shape[1]), lambda i: (i, 0))
    blk_t = lambda a: pl.BlockSpec((a.shape[0], nb, a.shape[2]), lambda i: (0, i, 0))
    blk_layer = lambda a: pl.BlockSpec((None, nb) + a.shape[2:],
                                       lambda i, n=a.ndim: (layer, i) + (0,) * (n - 2))
    full = lambda a: pl.BlockSpec(a.shape, lambda i, n=a.ndim: (0,) * n)
    sds = lambda a: jax.ShapeDtypeStruct(a.shape, f32)
    y_shape = jax.ShapeDtypeStruct((rows, W_BR), f32)
    y_spec = pl.BlockSpec((nb, W_BR), lambda i: (i, 0))
    n_in = N_SAMPLE_INPUTS
    assert 7 + len(params) == n_in
    n_small_outs = 7
    return pl.pallas_call(
        functools.partial(_sample_kernel, layer, len(prev_states)),
        grid=(rows // nb,),
        in_specs=[blk2(u), blk2(h0), blk_t(lconv_t), blk_t(gconv_t), blk_layer(sg), blk_layer(sh), blk_layer(sr)]
        + [full(p) for p in params] + [pl.BlockSpec(memory_space=pl.ANY)] * len(prev_states),
        out_specs=[y_spec] * 4 + [blk2(h0), blk_t(lconv_t), blk_t(gconv_t), blk_layer(sg), blk_layer(sh), blk_layer(sr)],
        out_shape=[y_shape] * 4 + [sds(h0), sds(lconv_t), sds(gconv_t), sds(sg), sds(sh), sds(sr)],
        input_output_aliases={n_in + j: n_small_outs + j for j in range(len(prev_states))},
        compiler_params=pltpu.CompilerParams(dimension_semantics=("arbitrary",), vmem_limit_bytes=VMEM_LIMIT),
        name="sample_branches",
    )(u, h0, lconv_t, gconv_t, sg, sh, sr, *params, *prev_states)


def kernel(x_prompt, x_sample, mem_prompt, state_lru_h, state_lru_conv, state_gdn_conv, state_gdn_s, state_hgrn_s, state_ret_s, cache_mem_k, cache_mem_v, w_in, lru_conv_w, lru_conv_b, lru_wa, lru_ba, lru_wi, lru_bi, lru_lambda, gdn_conv_w, gdn_a_log, gdn_dt_bias, gdn_norm_w, hgrn_lb_raw, hgrn_norm_w, ret_gn_w, ret_gn_b, w_mem_k, w_mem_v, w_merge_gate, b_merge_gate, w_branch, w_out, ln1_g, ln1_b, w_ffn_up, w_ffn_down, ln2_g, ln2_b):
    B, T, D = x_prompt.shape
    Bs = x_sample.shape[0]
    depth = w_in.shape[0]
    n_mem = mem_prompt.shape[1]
    W, H, HD = W_BR, N_HEADS, HEAD_DIM
    d_ff = w_ffn_down.shape[1]
    alpha = (2 * depth) ** 0.25
    assert Bs % SAMPLE_BLOCK == 0 and (B * T) % ROW_TILE == 0

    cos_all, sin_all = _rope_tables(list(range(T)) + [PAST_LEN])
    cos_p, sin_p = cos_all[:T], sin_all[:T]
    cos_s, sin_s = cos_all[T:T + 1], sin_all[T:T + 1]

    row = lambda a: a.reshape(1, -1)
    state_spec = ((H, HD, HD), f32)
    state_scratch = pltpu.VMEM((H, HD, HD), f32)

    xp = x_prompt
    xs = x_sample.reshape(Bs, D)
    mk_all, mv_all, mkb_all, mvb_all = _memkv(mem_prompt.reshape(B * n_mem, D), w_mem_k, w_mem_v)
    outs_p = [[] for _ in range(6)]
    outs_s = [[] for _ in range(3)]
    sample_states = ()

    for l in range(depth):
        wl = w_in[l]
        o = 0
        pieces = []
        for width in (W, 3 * W, W, H, H, W, W, W, W, W, W, W, W, W):
            pieces.append(wl[:, o:o + width])
            o += width
        (w_xa, w_qkv, w_z, w_gb, w_ga, w_hq, w_hf, w_hi, w_hg, w_rq, w_rk, w_rv, w_rg, w_xq) = pieces
        w_ba = jnp.concatenate([w_gb, w_ga, jnp.zeros((D, HD - 2 * H), f32)], 1).astype(bf16)
        w_xa, w_qkv, w_z, w_xq = (a.astype(bf16) for a in (w_xa, w_qkv, w_z, w_xq))
        w_hgrn = jnp.concatenate([w_hq, w_hf, w_hi, w_hg], 1).astype(bf16)
        w_ret = jnp.concatenate([w_rq, w_rk, w_rv, w_rg], 1).astype(bf16)

        lru_params = (lru_conv_w[l], row(lru_conv_b[l]), lru_wa[l].astype(bf16), row(lru_ba[l]),
                      lru_wi[l].astype(bf16), row(lru_bi[l]), row(lru_lambda[l]))
        pad_lanes = lambda a, start: jnp.zeros((1, HD), f32).at[0, start:start + H].set(a)
        alog_row, dtb_row = pad_lanes(gdn_a_log[l], H), pad_lanes(gdn_dt_bias[l], H)
        gdn_params = (gdn_conv_w[l], alog_row, dtb_row, row(gdn_norm_w[l]))
        hgrn_params = (hgrn_lb_raw, row(hgrn_norm_w[l]))
        ret_params = (row(ret_gn_w[l]), row(ret_gn_b[l]))
        merge_consts = (w_merge_gate[l].astype(bf16), b_merge_gate[l].reshape(N_BR, 1, D), w_branch[l].astype(bf16),
                        w_out[l].astype(bf16), row(ln1_g[l]), row(ln1_b[l]))
        ffn_consts = (w_ffn_up[l][:, :d_ff].astype(bf16), w_ffn_up[l][:, d_ff:].astype(bf16),
                      w_ffn_down[l].astype(bf16), row(ln2_g[l]), row(ln2_b[l]))
        merge_k = functools.partial(_merge_kernel, alpha)
        ffn_k = functools.partial(_ffn_kernel, alpha, 2)

        ya, h_last, lconv = _prompt_call(
            _lru_prompt_kernel, "lru_prompt", xp, T_TILE_WIDE, (w_xa,) + lru_params, (), (),
            [(W, bf16)], [((1, W), f32), ((CONV_W - 1, W), f32)],
            [pltpu.VMEM((CARRY_ROWS + T_TILE_WIDE, W), f32), pltpu.VMEM((T_TILE_WIDE, W), f32),
             pltpu.VMEM((T_TILE_WIDE, W), f32), pltpu.VMEM((8, W), f32)])
        yb, gconv, s_gdn = _prompt_call(
            _gdn_prompt_kernel, "gdn_prompt", xp, T_TILE, (w_qkv, w_z, w_ba) + gdn_params, (), (),
            [(W, bf16)], [((CONV_W - 1, 3 * W), f32), state_spec],
            [pltpu.VMEM((GDN_SEQS, CARRY_ROWS + T_TILE, 3 * W), f32), pltpu.VMEM((GDN_SEQS, HD, W), f32)],
            seqs=GDN_SEQS)
        yc, s_hgrn = _prompt_call(
            functools.partial(_hgrn_prompt_kernel, l), "hgrn_prompt", xp, T_TILE, (w_hgrn,) + hgrn_params, (), (),
            [(W, bf16)], [state_spec], [state_scratch])
        yd, s_ret = _prompt_call(
            _ret_prompt_kernel, "ret_prompt", xp, T_TILE, (w_ret,) + ret_params, (), (cos_p, sin_p),
            [(W, bf16)], [state_spec], [state_scratch])
        (ye,) = _prompt_call(
            _xattn_prompt_kernel, "xattn_prompt", xp, T_TILE_WIDE, (w_xq,),
            (mkb_all[l].reshape(B, n_mem, W), mvb_all[l].reshape(B, n_mem, W)), (), [(W, bf16)], [], [])
        x2d = xp.reshape(B * T, D)
        ys = [y.reshape(B * T, W) for y in (ya, yb, yc, yd, ye)]
        x2d = _row_call(merge_k, "merge_prompt", [x2d] + ys, merge_consts, D)
        x2d = _row_call(ffn_k, "ffn_prompt", [x2d], ffn_consts, D)
        xp = x2d.reshape(B, T, D)
        for lst, val in zip(outs_p, (h_last.reshape(B, W), lconv, gconv, s_gdn, s_hgrn, s_ret)):
            lst.append(val)

        u_s = _sample_proj(xs, (w_xa, w_qkv, w_z, w_hgrn, w_ret, w_xq, w_ba))
        sample_params = lru_params + gdn_params + hgrn_params + ret_params + (cos_s, sin_s)
        res = _sample_branches(
            l, u_s, state_lru_h[l], state_lru_conv[l].transpose(1, 0, 2), state_gdn_conv[l].transpose(1, 0, 2),
            state_gdn_s, state_hgrn_s, state_ret_s, sample_params, sample_states)
        ys_s, (h1, lconv_s, gconv_s), sample_states = res[:4], res[4:7], tuple(res[7:])
        xq_off = sum(w.shape[1] for w in (w_xa, w_qkv, w_z, w_hgrn, w_ret))
        q_s = u_s[:, xq_off:xq_off + W].reshape(Bs, H, HD)
        ye_s = _sample_xattn(l, q_s, cache_mem_k, cache_mem_v).reshape(Bs, W)
        xs = _row_call(merge_k, "merge_sample", [xs] + list(ys_s) + [ye_s], merge_consts, D)
        xs = _row_call(ffn_k, "ffn_sample", [xs], ffn_consts, D)
        for lst, val in zip(outs_s, (h1, lconv_s.transpose(1, 0, 2), gconv_s.transpose(1, 0, 2))):
            lst.append(val)

    mem_shape = (depth, B, n_mem, H, HD)
    return (xp, xs.reshape(Bs, 1, D), *(jnp.stack(v) for v in outs_p), mk_all.reshape(mem_shape),
            mv_all.reshape(mem_shape), *(jnp.stack(v) for v in outs_s), *sample_states)
```

```python
import functools
import math

import numpy as np
import jax
import jax.numpy as jnp
from jax import lax
from jax.experimental import pallas as pl
from jax.experimental.pallas import tpu as pltpu

f32 = jnp.float32
bf16 = jnp.bfloat16

N_HEADS = 4
HEAD_DIM = 128
W_BR = N_HEADS * HEAD_DIM
CONV_W = 4
N_BR = 5
LRU_C = 8.0
GDN_CHUNK = 64
HGRN_CHUNK = 16
ROPE_BASE = 10000.0
LN_EPS = 1e-5
NORM_EPS = 1e-6
PAST_LEN = 16384

T_TILE = 256
T_TILE_WIDE = 512
GDN_SEQS = 2
HGRN_LOOKAHEAD = 2
ROW_TILE = 512
SAMPLE_BLOCK = 8
CARRY_ROWS = 8
VMEM_LIMIT = 52 * 1024 * 1024


def _mm(a, b):
    return jnp.dot(a.astype(bf16), b.astype(bf16), preferred_element_type=f32)


def _mm_nt(a, b):
    return lax.dot_general(a.astype(bf16), b.astype(bf16), (((1,), (1,)), ((), ())), preferred_element_type=f32)


def _mm_tn(a, b):
    return lax.dot_general(a.astype(bf16), b.astype(bf16), (((0,), (0,)), ((), ())), preferred_element_type=f32)


def _mask_mm_exact(mask_bf16, x):
    hi = x.astype(bf16)
    r1 = x - hi.astype(f32)
    mid = r1.astype(bf16)
    lo = (r1 - mid.astype(f32)).astype(bf16)
    dot = functools.partial(jnp.dot, preferred_element_type=f32)
    return dot(mask_bf16, hi) + dot(mask_bf16, mid) + dot(mask_bf16, lo)


def _chunk_masks(n, chunk):
    shift = int(math.log2(chunk))
    row = lax.broadcasted_iota(jnp.int32, (n, n), 0)
    col = lax.broadcasted_iota(jnp.int32, (n, n), 1)
    same = (row >> shift) == (col >> shift)
    return jnp.logical_and(same, col <= row), same


def _as_bf16(mask):
    return jnp.where(mask, 1.0, 0.0).astype(bf16)


def _sigmoid(x):
    return jax.nn.sigmoid(x)


def _silu(x):
    return x * jax.nn.sigmoid(x)


def _layernorm(x, g, b):
    mu = jnp.mean(x, -1, keepdims=True)
    xc = x - mu
    var = jnp.mean(xc * xc, -1, keepdims=True)
    return xc * lax.rsqrt(var + LN_EPS) * g + b


def _head_slices(a):
    return [a[:, h * HEAD_DIM:(h + 1) * HEAD_DIM] for h in range(N_HEADS)]


def _causal_conv(buf_ref, x, cw):
    n = x.shape[0]
    buf_ref[CARRY_ROWS:CARRY_ROWS + n, :] = x
    out = cw[CONV_W - 1:CONV_W, :] * x
    for j in range(CONV_W - 1):
        off = CARRY_ROWS - (CONV_W - 1) + j
        out = out + cw[j:j + 1, :] * buf_ref[off:off + n, :]
    buf_ref[0:CARRY_ROWS, :] = buf_ref[n:n + CARRY_ROWS, :]
    return out


def _conv_state(buf_ref):
    return buf_ref[CARRY_ROWS - (CONV_W - 1):CARRY_ROWS, :]


def _lru_gates(xc, wa_ref, ba, wi_ref, bi, lam):
    xh = _head_slices(xc)
    r = _sigmoid(jnp.concatenate([_mm(xh[h], wa_ref[h]) for h in range(N_HEADS)], -1) + ba)
    ig = _sigmoid(jnp.concatenate([_mm(xh[h], wi_ref[h]) for h in range(N_HEADS)], -1) + bi)
    log_a = -LRU_C * r * jax.nn.softplus(-lam)
    a = jnp.exp(log_a)
    b = jnp.sqrt(jnp.tanh(-log_a) * (1.0 + a * a)) * (ig * xc)
    return a, b


def _gdn_gates(ba, alog_row, dtb_row):
    beta = _sigmoid(ba)
    g = -jnp.exp(alog_row) * jax.nn.softplus(ba + dtb_row)
    return beta, g


def _l2norm(a):
    return a * lax.rsqrt(jnp.sum(a * a, -1, keepdims=True) + NORM_EPS)


def _rmsnorm_heads(o_heads, w):
    parts = [o * lax.rsqrt(jnp.mean(o * o, -1, keepdims=True) + NORM_EPS) for o in o_heads]
    return jnp.concatenate(parts, -1) * w


def _groupnorm_heads(o_heads, w, b):
    parts = []
    for o in o_heads:
        mu = jnp.mean(o, -1, keepdims=True)
        oc = o - mu
        var = jnp.mean(oc * oc, -1, keepdims=True)
        parts.append(oc * lax.rsqrt(var + NORM_EPS))
    return jnp.concatenate(parts, -1) * w + b


def _hgrn_lower_bound(lbraw, layer):
    n = lbraw.shape[0]
    rows = [lbraw[i:i + 1, :] for i in range(n)]
    m = rows[0]
    for r in rows[1:]:
        m = jnp.maximum(m, r)
    es = [jnp.exp(r - m) for r in rows]
    tot = es[0]
    for e in es[1:]:
        tot = tot + e
    acc = jnp.zeros_like(m)
    for i in range(1, layer + 1):
        acc = acc + es[i] / tot
    return acc


def _hgrn_gates(fz, lb):
    log_f = jnp.logaddexp(jnp.log(lb), jnp.log1p(-lb) + jax.nn.log_sigmoid(fz))
    k = (1.0 - lb) * _sigmoid(-fz)
    return log_f, k


def _rotary(x, cos, sin_signed):
    return x * cos + pltpu.roll(x, HEAD_DIM // 2, 1) * sin_signed


def _ret_log_gamma(h):
    return math.log1p(-(2.0 ** (-5.0 - h)))


def _rope_kernel(ang_ref, cos_ref, sin_ref):
    ang = ang_ref[...]
    lane = lax.broadcasted_iota(jnp.int32, ang.shape, 1)
    s = jnp.sin(ang)
    cos_ref[...] = jnp.cos(ang)
    sin_ref[...] = jnp.where(lane < HEAD_DIM // 2, -s, s)


def _rope_tables(positions):
    half = HEAD_DIM // 2
    inv = ROPE_BASE ** (-np.arange(half, dtype=np.float64) / half)
    ang = np.mod(np.asarray(positions, np.float64)[:, None] * inv[None, :], 2.0 * np.pi)
    ang = np.concatenate([ang, ang], axis=1).astype(np.float32)
    n = ang.shape[0]
    pad = (-n) % 8
    ang = np.pad(ang, ((0, pad), (0, 0)))
    shp = jax.ShapeDtypeStruct(ang.shape, f32)
    cos, sin = pl.pallas_call(_rope_kernel, out_shape=(shp, shp), name="rope_tables")(jnp.asarray(ang))
    return cos[:n], sin[:n]


def _memkv_kernel(m_ref, wk_ref, wv_ref, k_ref, v_ref, kb_ref, vb_ref):
    m = m_ref[...]
    k = _mm(m, wk_ref[...])
    v = _mm(m, wv_ref[...])
    k_ref[...] = k
    v_ref[...] = v
    kb_ref[...] = k.astype(bf16)
    vb_ref[...] = v.astype(bf16)


def _memkv(mem2d, wk, wv):
    rows, d = mem2d.shape
    depth = wk.shape[0]
    tile = 256
    w_spec = pl.BlockSpec((None, d, W_BR), lambda l, i: (l, 0, 0))
    o_spec = pl.BlockSpec((None, tile, W_BR), lambda l, i: (l, i, 0))
    return pl.pallas_call(
        _memkv_kernel,
        grid=(depth, rows // tile),
        in_specs=[pl.BlockSpec((tile, d), lambda l, i: (i, 0)), w_spec, w_spec],
        out_specs=[o_spec] * 4,
        out_shape=[jax.ShapeDtypeStruct((depth, rows, W_BR), f32)] * 2
        + [jax.ShapeDtypeStruct((depth, rows, W_BR), bf16)] * 2,
        compiler_params=pltpu.CompilerParams(dimension_semantics=("arbitrary", "arbitrary"),
                                             vmem_limit_bytes=VMEM_LIMIT),
        name="mem_kv",
    )(mem2d, wk, wv)


def _prompt_call(kernel, name, x, tile, consts, per_batch, per_time, outs_tile, outs_batch, scratch, seqs=1):
    B, T, D = x.shape
    assert T % tile == 0 and B % seqs == 0
    nt = T // tile
    in_specs = [pl.BlockSpec((seqs, tile, D), lambda b, t: (b, t, 0))]
    for a in consts:
        in_specs.append(pl.BlockSpec(a.shape, lambda b, t, n=a.ndim: (0,) * n))
    for a in per_batch:
        in_specs.append(pl.BlockSpec((seqs,) + a.shape[1:], lambda b, t, n=a.ndim: (b,) + (0,) * (n - 1)))
    for a in per_time:
        in_specs.append(pl.BlockSpec((tile,) + a.shape[1:], lambda b, t, n=a.ndim: (t,) + (0,) * (n - 1)))
    out_specs, out_shape = [], []
    for width, dt in outs_tile:
        out_specs.append(pl.BlockSpec((seqs, tile, width), lambda b, t: (b, t, 0)))
        out_shape.append(jax.ShapeDtypeStruct((B, T, width), dt))
    for shp, dt in outs_batch:
        out_specs.append(pl.BlockSpec((seqs,) + shp, lambda b, t, n=len(shp): (b,) + (0,) * n))
        out_shape.append(jax.ShapeDtypeStruct((B,) + shp, dt))
    return pl.pallas_call(
        kernel,
        grid=(B // seqs, nt),
        in_specs=in_specs,
        out_specs=out_specs,
        out_shape=out_shape,
        scratch_shapes=scratch,
        compiler_params=pltpu.CompilerParams(dimension_semantics=("arbitrary", "arbitrary"),
                                             vmem_limit_bytes=VMEM_LIMIT),
        name=name,
    )(x, *consts, *per_batch, *per_time)


def _lru_prompt_kernel(x_ref, w_ref, cw_ref, cb_ref, wa_ref, ba_ref, wi_ref, bi_ref, lam_ref,
                       y_ref, h_ref, conv_ref, buf_ref, a_ref, b_ref, hcar_ref):
    n = x_ref.shape[1]

    @pl.when(pl.program_id(1) == 0)
    def _():
        buf_ref[0:CARRY_ROWS, :] = jnp.zeros((CARRY_ROWS, W_BR), f32)
        hcar_ref[...] = jnp.zeros_like(hcar_ref)

    xa = _mm(x_ref[0], w_ref[...])
    xc = _causal_conv(buf_ref, xa, cw_ref[...]) + cb_ref[...]
    a, b = _lru_gates(xc, wa_ref, ba_ref[...], wi_ref, bi_ref[...], lam_ref[...])

    a = a.reshape(n // 8, 8, W_BR)
    b = b.reshape(n // 8, 8, W_BR)
    row8 = lax.broadcasted_iota(jnp.int32, (n // 8, 8, W_BR), 1)
    for d in (1, 2, 4):
        keep = row8 >= d
        a_prev = jnp.where(keep, pltpu.roll(a, d, 1), 1.0)
        b_prev = jnp.where(keep, pltpu.roll(b, d, 1), 0.0)
        b = b + a * b_prev
        a = a * a_prev
    a_ref[...] = a.reshape(n, W_BR)
    b_ref[...] = b.reshape(n, W_BR)

    def group(gi, h_prev):
        i = pl.multiple_of(gi * 8, 8)
        blk = b_ref[pl.ds(i, 8), :] + a_ref[pl.ds(i, 8), :] * h_prev
        b_ref[pl.ds(i, 8), :] = blk
        return jnp.broadcast_to(blk[7:8, :], (8, W_BR))

    h_last = lax.fori_loop(0, n // 8, group, hcar_ref[...])
    hcar_ref[...] = h_last
    y_ref[0] = b_ref[...].astype(y_ref.dtype)
    h_ref[0] = h_last[0:1, :]
    conv_ref[0] = _conv_state(buf_ref)


def _gdn_prompt_kernel(x_ref, wqkv_ref, wz_ref, wba_ref, cw_ref, alog_ref, dtb_ref, nw_ref,
                       y_ref, conv_ref, s_ref, buf_ref, st_ref):
    seqs, n_seq, C = x_ref.shape[0], x_ref.shape[1], GDN_CHUNK
    assert n_seq == N_HEADS * C
    n = seqs * n_seq

    @pl.when(pl.program_id(1) == 0)
    def _():
        buf_ref[:, 0:CARRY_ROWS, :] = jnp.zeros((seqs, CARRY_ROWS, 3 * W_BR), f32)
        st_ref[...] = jnp.zeros_like(st_ref)

    x = jnp.concatenate([x_ref[s] for s in range(seqs)], 0).astype(bf16)
    qkv = _mm(x, wqkv_ref[...])
    c = _silu(jnp.concatenate([_causal_conv(buf_ref.at[s], qkv[s * n_seq:(s + 1) * n_seq], cw_ref[...])
                               for s in range(seqs)], 0))
    z = _mm(x, wz_ref[...])
    beta, g = _gdn_gates(_mm(x, wba_ref[...]), alog_ref[...], dtb_ref[...])

    incl_t = _as_bf16(_chunk_masks(n_seq, C)[0])
    G_seq = [_mask_mm_exact(incl_t, g[s * n_seq:(s + 1) * n_seq]) for s in range(seqs)]
    G = jnp.concatenate(G_seq, 0)
    GT = [a.T for a in G_seq]
    H, HD, W = N_HEADS, HEAD_DIM, W_BR
    nck = n // C
    nck_seq = n_seq // C
    heads = range(H)
    tile = lambda a, h: a[:, h * HD:(h + 1) * HD]

    q_all = jnp.concatenate([_l2norm(tile(c, h)) * HD ** -0.5 for h in heads], 1)
    k_all = jnp.concatenate([_l2norm(tile(c, H + h)) for h in heads], 1)
    v_all = c[:, 2 * W:]
    per_head = lambda a, lo: jnp.concatenate([jnp.broadcast_to(a[:, lo + h:lo + h + 1], (n, HD)) for h in heads], 1)
    beta_w = per_head(beta, 0)
    g_w = per_head(G, H)
    glast_w = jnp.concatenate([jnp.broadcast_to(g_w[(ck + 1) * C - 1:(ck + 1) * C, :], (C, W)) for ck in range(nck)], 0)
    eg_w = jnp.exp(g_w)
    kb_all = k_all * beta_w
    vb_all = v_all * beta_w
    kbe_all = kb_all * eg_w
    qe_all = q_all * eg_w
    kend_all = k_all * jnp.exp(glast_w - g_w)
    eglast_w = jnp.exp(glast_w)

    lane_blk = lax.broadcasted_iota(jnp.int32, (n, H * C), 1) >> int(math.log2(C))
    gcol_c = G[:, H + H - 1:H + H]
    for h in reversed(range(H - 1)):
        gcol_c = jnp.where(lane_blk == h, G[:, H + h:H + h + 1], gcol_c)
    ri = lax.broadcasted_iota(jnp.int32, (C, H * C), 0)
    lj = lax.broadcasted_iota(jnp.int32, (C, H * C), 1)
    incl, strict = (lj & (C - 1)) <= ri, (lj & (C - 1)) < ri
    row_blk = lax.broadcasted_iota(jnp.int32, (1, H * C), 1) >> int(math.log2(C))
    bd_mask = (lax.broadcasted_iota(jnp.int32, (H * C, H * C), 0) >> int(math.log2(C))) == (
        lax.broadcasted_iota(jnp.int32, (H * C, H * C), 1) >> int(math.log2(C)))

    def block_diag(parts, width):
        zero = jnp.zeros((parts[0].shape[0], width), parts[0].dtype)
        return jnp.concatenate([jnp.concatenate([parts[h] if h == hp else zero for h in heads], 1) for hp in heads], 0)

    def block_diag_small(m_b):
        return jnp.where(bd_mask, jnp.concatenate([m_b] * H, 0), jnp.zeros((), bf16))

    neg_a, a_qk, rhs_w, rhs_bd = [], [], [], []
    for ck in range(nck):
        sl = slice(ck * C, (ck + 1) * C)
        gt, cs = GT[ck // nck_seq], ck % nck_seq
        g_row = None
        for h in reversed(heads):
            r = gt[H + h:H + h + 1, :]
            r = pltpu.roll(r, ((h - cs) * C) % n_seq, 1) if (h - cs) % nck_seq else r
            g_row = r if g_row is None else jnp.where(row_blk == h, r, g_row)
        decay = jnp.where(incl, jnp.exp(gcol_c[sl] - g_row), 0.0)
        k_b = k_all[sl].astype(bf16)
        k_bd = block_diag([tile(k_b, h) for h in heads], HD)
        neg_a.append(jnp.where(strict, -(_mm_nt(kb_all[sl], k_bd) * decay), 0.0))
        a_qk.append(_mm_nt(q_all[sl], k_bd) * decay)
        parts = [jnp.concatenate([tile(vb_all[sl], h), tile(kbe_all[sl], h)], 1) for h in heads]
        rhs_w.append(jnp.concatenate(parts, 1))
        rhs_bd.append(block_diag([p.astype(bf16) for p in parts], 2 * HD))

    dot = functools.partial(jnp.dot, preferred_element_type=f32)
    m = list(neg_a)
    t_off = list(neg_a)
    m_bd = [block_diag_small(a.astype(bf16)) for a in m]
    for _ in range(int(math.log2(C)) - 1):
        m = [dot(m[ck].astype(bf16), m_bd[ck]) for ck in range(nck)]
        m_bd = [block_diag_small(a.astype(bf16)) for a in m]
        t_off = [t_off[ck] + m[ck] + dot(t_off[ck].astype(bf16), m_bd[ck]) for ck in range(nck)]
    sol = [rhs_w[ck] + dot(t_off[ck].astype(bf16), rhs_bd[ck]) for ck in range(nck)]

    S = [st_ref[s] for s in range(seqs)]
    zero_s = jnp.zeros((HD, HD), bf16)
    o_chunks = [None] * nck
    for cs in range(nck_seq):
        for s in range(seqs):
            ck = s * nck_seq + cs
            sl = slice(ck * C, (ck + 1) * C)
            S_b = S[s].astype(bf16)
            s_pairs = [jnp.concatenate([jnp.concatenate([tile(S_b, p), zero_s], 1),
                                        jnp.concatenate([zero_s, tile(S_b, p + 1)], 1)], 0) for p in (0, 2)]
            u = [sol[ck][:, 2 * h * HD:(2 * h + 1) * HD] for h in heads]
            w = [sol[ck][:, (2 * h + 1) * HD:(2 * h + 2) * HD].astype(bf16) for h in heads]
            w_s = jnp.concatenate([dot(jnp.concatenate([w[p], w[p + 1]], 1), s_pairs[p // 2]) for p in (0, 2)], 1)
            v_new = jnp.concatenate(u, 1) - w_s
            v_b = v_new.astype(bf16)
            v_bd = block_diag([tile(v_b, h) for h in heads], HD)
            qe_b = qe_all[sl].astype(bf16)
            q_s = jnp.concatenate([dot(qe_b[:, p * HD:(p + 2) * HD], s_pairs[p // 2]) for p in (0, 2)], 1)
            o_chunks[ck] = q_s + dot(a_qk[ck].astype(bf16), v_bd)
            kend_stack = jnp.concatenate([tile(kend_all[sl], h) for h in heads], 0)
            S[s] = eglast_w[ck * C:ck * C + 1, :] * S[s] + _mm_tn(kend_stack, v_bd)
    o_all = jnp.concatenate(o_chunks, 0)
    y = _rmsnorm_heads([tile(o_all, h) for h in heads], nw_ref[...]) * _silu(z)
    for s in range(seqs):
        st_ref[s] = S[s]
        for h in heads:
            s_ref[s, h] = tile(S[s], h)
        y_ref[s] = y[s * n_seq:(s + 1) * n_seq].astype(y_ref.dtype)
        conv_ref[s] = _conv_state(buf_ref.at[s])


def _hgrn_prompt_kernel(layer, x_ref, w_ref, lbraw_ref, nw_ref, y_ref, s_ref, st_ref):
    n, C = x_ref.shape[1], HGRN_CHUNK

    @pl.when(pl.program_id(1) == 0)
    def _():
        st_ref[...] = jnp.zeros_like(st_ref)

    u = _mm(x_ref[0], w_ref[...])
    hq, fz, hi, hg = (u[:, i * W_BR:(i + 1) * W_BR] for i in range(4))
    lb = _hgrn_lower_bound(lbraw_ref[...], layer)
    log_f, k = _hgrn_gates(fz, lb)
    q = _silu(hq)

    incl, _ = _chunk_masks(n, C)
    G = _mask_mm_exact(_as_bf16(incl), log_f)
    g_last = [G[(ck + 1) * C - 1:(ck + 1) * C, :] for ck in range(n // C)]
    Gtot = jnp.concatenate([jnp.broadcast_to(r, (C, W_BR)) for r in g_last], 0)
    q_g = q * jnp.exp(G)
    k_g = k * jnp.exp(-G)
    k_end = k * jnp.exp(Gtot - G)
    dT = jnp.exp(jnp.concatenate(g_last, 0)).T

    heads = range(N_HEADS)
    hsl = [slice(h * HEAD_DIM, (h + 1) * HEAD_DIM) for h in heads]
    csl = [slice(ck * C, (ck + 1) * C) for ck in range(n // C)]
    a_qk = [jnp.where(incl, _mm_nt(q_g[:, hs], k_g[:, hs]), 0.0) for hs in hsl]
    o_intra = [_mm(a_qk[h], hi[:, hsl[h]]) for h in heads]
    kv = {}
    for ck in range(min(HGRN_LOOKAHEAD, n // C)):
        for h in heads:
            kv[ck, h] = _mm_tn(k_end[csl[ck], hsl[h]], hi[csl[ck], hsl[h]])
    S = [st_ref[h] for h in heads]
    o_inter = {}
    for ck in range(n // C):
        if ck + HGRN_LOOKAHEAD < n // C:
            for h in heads:
                kv[ck + HGRN_LOOKAHEAD, h] = _mm_tn(k_end[csl[ck + HGRN_LOOKAHEAD], hsl[h]],
                                                    hi[csl[ck + HGRN_LOOKAHEAD], hsl[h]])
        for h in heads:
            o_inter[ck, h] = _mm(q_g[csl[ck], hsl[h]], S[h])
        S = [dT[hsl[h], ck:ck + 1] * S[h] + kv.pop((ck, h)) for h in heads]
    for h in heads:
        st_ref[h] = S[h]
        s_ref[0, h] = S[h]
    o_heads = [o_intra[h] + jnp.concatenate([o_inter[ck, h] for ck in range(n // C)], 0) for h in heads]

    y = _rmsnorm_heads(o_heads, nw_ref[...]) * _sigmoid(hg)
    y_ref[0] = y.astype(y_ref.dtype)


def _ret_prompt_kernel(x_ref, w_ref, gw_ref, gb_ref, cos_ref, sin_ref, y_ref, s_ref, st_ref):
    n = x_ref.shape[1]

    @pl.when(pl.program_id(1) == 0)
    def _():
        st_ref[...] = jnp.zeros_like(st_ref)

    xb = x_ref[0].astype(bf16)

    def project(pair):
        lo = 2 * pair * HEAD_DIM
        return [_mm(xb, w_ref[:, i * W_BR + lo:i * W_BR + lo + 2 * HEAD_DIM]) for i in range(4)]

    cos, sin = cos_ref[...], sin_ref[...]
    ri = lax.broadcasted_iota(jnp.int32, (n, n), 0)
    rj = lax.broadcasted_iota(jnp.int32, (n, n), 1)
    dist = (ri - rj).astype(f32)
    pos = lax.broadcasted_iota(jnp.int32, (n, 1), 0).astype(f32)

    heads = range(N_HEADS)
    proj = [project(p) for p in range(N_HEADS // 2)]
    piece = lambda i, h: proj[h // 2][i][:, (h % 2) * HEAD_DIM:(h % 2 + 1) * HEAD_DIM]
    lg = [_ret_log_gamma(h) for h in heads]
    q = [_rotary(piece(0, h), cos, sin) for h in heads]
    k = [_rotary(piece(1, h), cos, sin) * HEAD_DIM ** -0.5 for h in heads]
    v = [piece(2, h) for h in heads]
    a_qk = [_mm_nt(q[h], k[h]) * jnp.where(rj <= ri, jnp.exp(dist * lg[h]), 0.0) for h in heads]
    S = [st_ref[h] for h in heads]
    o = [_mm(a_qk[h], v[h]) + _mm(q[h] * jnp.exp((pos + 1.0) * lg[h]), S[h]) for h in heads]
    S = [math.exp(n * lg[h]) * S[h] + _mm_tn(k[h] * jnp.exp((n - 1.0 - pos) * lg[h]), v[h]) for h in heads]
    for h in heads:
        st_ref[h] = S[h]
        s_ref[0, h] = S[h]
    rg = jnp.concatenate([piece(3, h) for h in heads], 1)
    y = _groupnorm_heads(o, gw_ref[...], gb_ref[...]) * _silu(rg)
    y_ref[0] = y.astype(y_ref.dtype)


def _xattn_prompt_kernel(x_ref, w_ref, mk_ref, mv_ref, y_ref):
    q = _mm(x_ref[0], w_ref[...])
    mk, mv = mk_ref[0], mv_ref[0]
    hsl = [slice(h * HEAD_DIM, (h + 1) * HEAD_DIM) for h in range(N_HEADS)]
    s = [_mm_nt(q[:, hs], mk[:, hs]) * HEAD_DIM ** -0.5 for hs in hsl]
    e = [jnp.exp(a - jnp.max(a, -1, keepdims=True)) for a in s]
    p = [a / jnp.sum(a, -1, keepdims=True) for a in e]
    outs = [_mm(a, mv[:, hs]) for a, hs in zip(p, hsl)]
    y_ref[0] = jnp.concatenate(outs, -1).astype(y_ref.dtype)


def _merge_kernel(alpha, x_ref, ya_ref, yb_ref, yc_ref, yd_ref, ye_ref, wg_ref, bg_ref, wb_ref, wo_ref,
                  g_ref, b_ref, o_ref):
    x = x_ref[...]
    xb = x.astype(bf16)
    merged = None
    for i, y_ref in enumerate((ya_ref, yb_ref, yc_ref, yd_ref, ye_ref)):
        gate = _sigmoid(_mm(xb, wg_ref[i]) + bg_ref[i])
        term = gate * _mm(y_ref[...], wb_ref[i])
        merged = term if merged is None else merged + term
    mix = _mm(merged, wo_ref[...])
    o_ref[...] = _layernorm(alpha * x + mix, g_ref[...], b_ref[...])


def _ffn_kernel(alpha, n_split, x_ref, wg_ref, wv_ref, wd_ref, g_ref, b_ref, o_ref):
    x = x_ref[...]
    xb = x.astype(bf16)
    d_ff = wg_ref.shape[1]
    step = d_ff // n_split
    ffn = None
    for i in range(n_split):
        cs = slice(i * step, (i + 1) * step)
        hidden = _silu(_mm(xb, wg_ref[:, cs])) * _mm(xb, wv_ref[:, cs])
        part = _mm(hidden, wd_ref[cs, :])
        ffn = part if ffn is None else ffn + part
    o_ref[...] = _layernorm(alpha * x + ffn, g_ref[...], b_ref[...])


def _row_call(kernel, name, row_ops, consts, d_out):
    rows = row_ops[0].shape[0]
    tile = min(ROW_TILE, rows)
    in_specs = [pl.BlockSpec((tile, a.shape[1]), lambda i: (i, 0)) for a in row_ops]
    in_specs += [pl.BlockSpec(a.shape, lambda i, n=a.ndim: (0,) * n, pipeline_mode=pl.Buffered(1)) for a in consts]
    return pl.pallas_call(
        kernel,
        grid=(rows // tile,),
        in_specs=in_specs,
        out_specs=pl.BlockSpec((tile, d_out), lambda i: (i, 0)),
        out_shape=jax.ShapeDtypeStruct((rows, d_out), f32),
        compiler_params=pltpu.CompilerParams(dimension_semantics=("arbitrary",), vmem_limit_bytes=VMEM_LIMIT),
        name=name,
    )(*row_ops, *consts)


def _proj_kernel(x_ref, *refs):
    w_refs, o_ref = refs[:-1], refs[-1]
    x = x_ref[...].astype(bf16)
    off = 0
    for w_ref in w_refs:
        width = w_ref.shape[1]
        o_ref[:, off:off + width] = _mm(x, w_ref[...])
        off += width


def _sample_proj(x, weights):
    rows = x.shape[0]
    n = sum(w.shape[1] for w in weights)
    return pl.pallas_call(
        _proj_kernel,
        out_shape=jax.ShapeDtypeStruct((rows, n), f32),
        compiler_params=pltpu.CompilerParams(vmem_limit_bytes=VMEM_LIMIT),
        name="sample_proj",
    )(x, *weights)


def _columns(rows_list):
    nb = rows_list[0].shape[0]
    out = []
    for b in range(nb):
        tile = jnp.concatenate([r[b:b + 1, :] for r in rows_list]
                               + [jnp.zeros((8 - len(rows_list), HEAD_DIM), f32)], 0)
        out.append(tile.T)
    return out


N_SAMPLE_INPUTS = 24


def _rows_of(parts):
    nb = 1 + max(bi for bi, _ in parts)
    return [jnp.concatenate([parts[bi, h] for bi in range(nb)], 0) for h in range(N_HEADS)]


def _sample_kernel(layer, n_alias, *refs):
    (u_ref, h0_ref, lconv_ref, gconv_ref, sg_ref, sh_ref, sr_ref,
     lcw_ref, lcb_ref, lwa_ref, lba_ref, lwi_ref, lbi_ref, lam_ref,
     gcw_ref, alog_ref, dtb_ref, gnw_ref, lbraw_ref, hnw_ref, rgw_ref, rgb_ref, cos_ref, sin_ref) = refs[:N_SAMPLE_INPUTS]
    (ya_ref, yb_ref, yc_ref, yd_ref, h1_ref, lconv_o_ref, gconv_o_ref,
     sg_o_ref, sh_o_ref, sr_o_ref) = refs[N_SAMPLE_INPUTS + n_alias:]
    if n_alias == 0:
        for o_ref in (sg_o_ref, sh_o_ref, sr_o_ref):
            for other in range(o_ref.shape[0]):
                if other != layer:
                    o_ref[other] = jnp.zeros(o_ref.shape[1:], f32)
        sg_o_ref, sh_o_ref, sr_o_ref = (r.at[layer] for r in (sg_o_ref, sh_o_ref, sr_o_ref))
    nb = SAMPLE_BLOCK
    W = W_BR
    pairs = [(bi, h) for bi in range(nb) for h in range(N_HEADS)]
    u = u_ref[...]
    off = 0

    def take(width):
        nonlocal off
        part = u[:, off:off + width]
        off += width
        return part

    xa, qkv, z = take(W), take(3 * W), take(W)
    hq, fz, hi, hg = take(W), take(W), take(W), take(W)
    rq, rk, rv, rg = take(W), take(W), take(W), take(W)
    xq, ba = take(W), take(HEAD_DIM)

    lcw = lcw_ref[...]
    xc = lcw[CONV_W - 1:CONV_W, :] * xa + lcb_ref[...]
    for j in range(CONV_W - 1):
        xc = xc + lcw[j:j + 1, :] * lconv_ref[j]
    for j in range(CONV_W - 2):
        lconv_o_ref[j] = lconv_ref[j + 1]
    lconv_o_ref[CONV_W - 2] = xa
    a, b = _lru_gates(xc, lwa_ref, lba_ref[...], lwi_ref, lbi_ref[...], lam_ref[...])
    h1 = a * h0_ref[...] + b
    h1_ref[...] = h1
    ya_ref[...] = h1

    gcw = gcw_ref[...]
    c = gcw[CONV_W - 1:CONV_W, :] * qkv
    for j in range(CONV_W - 1):
        c = c + gcw[j:j + 1, :] * gconv_ref[j]
    for j in range(CONV_W - 2):
        gconv_o_ref[j] = gconv_ref[j + 1]
    gconv_o_ref[CONV_W - 2] = qkv
    c = _silu(c)
    beta, g = _gdn_gates(ba, alog_ref[...], dtb_ref[...])
    eg = jnp.exp(g)
    qn, kn, vv, kcol = [], [], [], {}
    for h in range(N_HEADS):
        qn.append(_l2norm(c[:, h * HEAD_DIM:(h + 1) * HEAD_DIM]) * HEAD_DIM ** -0.5)
        kn.append(_l2norm(c[:, W + h * HEAD_DIM:W + (h + 1) * HEAD_DIM]))
        vv.append(c[:, 2 * W + h * HEAD_DIM:2 * W + (h + 1) * HEAD_DIM])
        cols = _columns([kn[h]])
        for bi in range(nb):
            kcol[bi, h] = cols[bi][:, 0:1]
    kS = _rows_of({(bi, h): _mm(kn[h][bi:bi + 1, :], sg_ref[bi, h]) for bi, h in pairs})
    v_new = [beta[:, h:h + 1] * (vv[h] - eg[:, N_HEADS + h:N_HEADS + h + 1] * kS[h]) for h in range(N_HEADS)]
    S1 = {}
    for bi, h in pairs:
        S1[bi, h] = (eg[bi:bi + 1, N_HEADS + h:N_HEADS + h + 1] * sg_ref[bi, h]
                     + kcol[bi, h] * v_new[h][bi:bi + 1, :])
        sg_o_ref[bi, h] = S1[bi, h]
    o = _rows_of({(bi, h): _mm(qn[h][bi:bi + 1, :], S1[bi, h]) for bi, h in pairs})
    yb_ref[...] = _rmsnorm_heads(o, gnw_ref[...]) * _silu(z)

    lb = _hgrn_lower_bound(lbraw_ref[...], layer)
    log_f, kk = _hgrn_gates(fz, lb)
    f = jnp.exp(log_f)
    qs = _silu(hq)
    kcol, fcol = {}, {}
    for h in range(N_HEADS):
        hs = slice(h * HEAD_DIM, (h + 1) * HEAD_DIM)
        cols = _columns([kk[:, hs], f[:, hs]])
        for bi in range(nb):
            kcol[bi, h], fcol[bi, h] = cols[bi][:, 0:1], cols[bi][:, 1:2]
    S1 = {}
    for bi, h in pairs:
        hs = slice(h * HEAD_DIM, (h + 1) * HEAD_DIM)
        S1[bi, h] = fcol[bi, h] * sh_ref[bi, h] + kcol[bi, h] * hi[bi:bi + 1, hs]
        sh_o_ref[bi, h] = S1[bi, h]
    o = _rows_of({(bi, h): _mm(qs[bi:bi + 1, h * HEAD_DIM:(h + 1) * HEAD_DIM], S1[bi, h]) for bi, h in pairs})
    yc_ref[...] = _rmsnorm_heads(o, hnw_ref[...]) * _sigmoid(hg)

    cos, sin = cos_ref[...], sin_ref[...]
    qr, kcol = [], {}
    for h in range(N_HEADS):
        hs = slice(h * HEAD_DIM, (h + 1) * HEAD_DIM)
        qr.append(_rotary(rq[:, hs], cos, sin))
        cols = _columns([_rotary(rk[:, hs], cos, sin) * HEAD_DIM ** -0.5])
        for bi in range(nb):
            kcol[bi, h] = cols[bi][:, 0:1]
    S1 = {}
    for bi, h in pairs:
        hs = slice(h * HEAD_DIM, (h + 1) * HEAD_DIM)
        S1[bi, h] = math.exp(_ret_log_gamma(h)) * sr_ref[bi, h] + kcol[bi, h] * rv[bi:bi + 1, hs]
        sr_o_ref[bi, h] = S1[bi, h]
    o = _rows_of({(bi, h): _mm(qr[h][bi:bi + 1, :], S1[bi, h]) for bi, h in pairs})
    yd_ref[...] = _groupnorm_heads(o, rgw_ref[...], rgb_ref[...]) * _silu(rg)


def _sample_xattn_kernel(q_ref, ck_ref, cv_ref, o_ref):
    for bi in range(q_ref.shape[0]):
        q = q_ref[bi]
        s = jnp.sum(ck_ref[bi] * q[None, :, :], axis=-1, keepdims=True) * HEAD_DIM ** -0.5
        e = jnp.exp(s - jnp.max(s, axis=0, keepdims=True))
        p = e / jnp.sum(e, axis=0, keepdims=True)
        o_ref[bi] = jnp.sum(p * cv_ref[bi], axis=0)


def _sample_xattn(layer, q, ck, cv):
    nb = SAMPLE_BLOCK
    cache_spec = pl.BlockSpec((None, nb) + ck.shape[2:], lambda i: (layer, i, 0, 0, 0))
    q_spec = pl.BlockSpec((nb,) + q.shape[1:], lambda i: (i, 0, 0))
    return pl.pallas_call(
        _sample_xattn_kernel,
        grid=(q.shape[0] // nb,),
        in_specs=[q_spec, cache_spec, cache_spec],
        out_specs=q_spec,
        out_shape=jax.ShapeDtypeStruct(q.shape, f32),
        compiler_params=pltpu.CompilerParams(dimension_semantics=("arbitrary",), vmem_limit_bytes=VMEM_LIMIT),
        name="sample_xattn",
    )(q, ck, cv)


def _sample_branches(layer, u, h0, lconv_t, gconv_t, sg, sh, sr, params, prev_states):
    nb = SAMPLE_BLOCK
    rows = u.shape[0]
    blk2 = lambda a: pl.BlockSpec((nb, a.shape[1]), lambda i: (i, 0))
    blk_t = lambda a: pl.BlockSpec((a.shape[0], nb, a.shape[2]), lambda i: (0, i, 0))
    blk_layer = lambda a: pl.BlockSpec((None, nb) + a.shape[2:],
                                       lambda i, n=a.ndim: (layer, i) + (0,) * (n - 2))
    full = lambda a: pl.BlockSpec(a.shape, lambda i, n=a.ndim: (0,) * n)
    sds = lambda a: jax.ShapeDtypeStruct(a.shape, f32)
    y_shape = jax.ShapeDtypeStruct((rows, W_BR), f32)
    y_spec = pl.BlockSpec((nb, W_BR), lambda i: (i, 0))
    n_in = N_SAMPLE_INPUTS
    assert 7 + len(params) == n_in
    n_small_outs = 7
    blk_state_out = blk_layer if prev_states else (
        lambda a: pl.BlockSpec((a.shape[0], nb) + a.shape[2:], lambda i, n=a.ndim: (0, i) + (0,) * (n - 2)))
    return pl.pallas_call(
        functools.partial(_sample_kernel, layer, len(prev_states)),
        grid=(rows // nb,),
        in_specs=[blk2(u), blk2(h0), blk_t(lconv_t), blk_t(gconv_t), blk_layer(sg), blk_layer(sh), blk_layer(sr)]
        + [full(p) for p in params] + [pl.BlockSpec(memory_space=pl.ANY)] * len(prev_states),
        out_specs=[y_spec] * 4 + [blk2(h0), blk_t(lconv_t), blk_t(gconv_t),
                                  blk_state_out(sg), blk_state_out(sh), blk_state_out(sr)],
        out_shape=[y_shape] * 4 + [sds(h0), sds(lconv_t), sds(gconv_t), sds(sg), sds(sh), sds(sr)],
        input_output_aliases={n_in + j: n_small_outs + j for j in range(len(prev_states))},
        compiler_params=pltpu.CompilerParams(dimension_semantics=("arbitrary",), vmem_limit_bytes=VMEM_LIMIT),
        name="sample_branches",
    )(u, h0, lconv_t, gconv_t, sg, sh, sr, *params, *prev_states)


def kernel(x_prompt, x_sample, mem_prompt, state_lru_h, state_lru_conv, state_gdn_conv, state_gdn_s, state_hgrn_s, state_ret_s, cache_mem_k, cache_mem_v, w_in, lru_conv_w, lru_conv_b, lru_wa, lru_ba, lru_wi, lru_bi, lru_lambda, gdn_conv_w, gdn_a_log, gdn_dt_bias, gdn_norm_w, hgrn_lb_raw, hgrn_norm_w, ret_gn_w, ret_gn_b, w_mem_k, w_mem_v, w_merge_gate, b_merge_gate, w_branch, w_out, ln1_g, ln1_b, w_ffn_up, w_ffn_down, ln2_g, ln2_b):
    B, T, D = x_prompt.shape
    Bs = x_sample.shape[0]
    depth = w_in.shape[0]
    n_mem = mem_prompt.shape[1]
    W, H, HD = W_BR, N_HEADS, HEAD_DIM
    d_ff = w_ffn_down.shape[1]
    alpha = (2 * depth) ** 0.25
    assert Bs % SAMPLE_BLOCK == 0 and (B * T) % ROW_TILE == 0

    cos_all, sin_all = _rope_tables(list(range(T)) + [PAST_LEN])
    cos_p, sin_p = cos_all[:T], sin_all[:T]
    cos_s, sin_s = cos_all[T:T + 1], sin_all[T:T + 1]

    row = lambda a: a.reshape(1, -1)
    state_spec = ((H, HD, HD), f32)
    state_scratch = pltpu.VMEM((H, HD, HD), f32)

    xp = x_prompt
    xs = x_sample.reshape(Bs, D)
    mk_all, mv_all, mkb_all, mvb_all = _memkv(mem_prompt.reshape(B * n_mem, D), w_mem_k, w_mem_v)
    outs_p = [[] for _ in range(6)]
    outs_s = [[] for _ in range(3)]
    sample_states = ()

    for l in range(depth):
        wl = w_in[l]
        o = 0
        pieces = []
        for width in (W, 3 * W, W, H, H, W, W, W, W, W, W, W, W, W):
            pieces.append(wl[:, o:o + width])
            o += width
        (w_xa, w_qkv, w_z, w_gb, w_ga, w_hq, w_hf, w_hi, w_hg, w_rq, w_rk, w_rv, w_rg, w_xq) = pieces
        w_ba = jnp.concatenate([w_gb, w_ga, jnp.zeros((D, HD - 2 * H), f32)], 1).astype(bf16)
        w_xa, w_qkv, w_z, w_xq = (a.astype(bf16) for a in (w_xa, w_qkv, w_z, w_xq))
        w_hgrn = jnp.concatenate([w_hq, w_hf, w_hi, w_hg], 1).astype(bf16)
        w_ret = jnp.concatenate([w_rq, w_rk, w_rv, w_rg], 1).astype(bf16)

        lru_params = (lru_conv_w[l], row(lru_conv_b[l]), lru_wa[l].astype(bf16), row(lru_ba[l]),
                      lru_wi[l].astype(bf16), row(lru_bi[l]), row(lru_lambda[l]))
        pad_lanes = lambda a, start: jnp.zeros((1, HD), f32).at[0, start:start + H].set(a)
        alog_row, dtb_row = pad_lanes(gdn_a_log[l], H), pad_lanes(gdn_dt_bias[l], H)
        gdn_params = (gdn_conv_w[l], alog_row, dtb_row, row(gdn_norm_w[l]))
        hgrn_params = (hgrn_lb_raw, row(hgrn_norm_w[l]))
        ret_params = (row(ret_gn_w[l]), row(ret_gn_b[l]))
        merge_consts = (w_merge_gate[l].astype(bf16), b_merge_gate[l].reshape(N_BR, 1, D), w_branch[l].astype(bf16),
                        w_out[l].astype(bf16), row(ln1_g[l]), row(ln1_b[l]))
        ffn_consts = (w_ffn_up[l][:, :d_ff].astype(bf16), w_ffn_up[l][:, d_ff:].astype(bf16),
                      w_ffn_down[l].astype(bf16), row(ln2_g[l]), row(ln2_b[l]))
        merge_k = functools.partial(_merge_kernel, alpha)
        ffn_k = functools.partial(_ffn_kernel, alpha, 2)

        ya, h_last, lconv = _prompt_call(
            _lru_prompt_kernel, "lru_prompt", xp, T_TILE_WIDE, (w_xa,) + lru_params, (), (),
            [(W, bf16)], [((1, W), f32), ((CONV_W - 1, W), f32)],
            [pltpu.VMEM((CARRY_ROWS + T_TILE_WIDE, W), f32), pltpu.VMEM((T_TILE_WIDE, W), f32),
             pltpu.VMEM((T_TILE_WIDE, W), f32), pltpu.VMEM((8, W), f32)])
        yb, gconv, s_gdn = _prompt_call(
            _gdn_prompt_kernel, "gdn_prompt", xp, T_TILE, (w_qkv, w_z, w_ba) + gdn_params, (), (),
            [(W, bf16)], [((CONV_W - 1, 3 * W), f32), state_spec],
            [pltpu.VMEM((GDN_SEQS, CARRY_ROWS + T_TILE, 3 * W), f32), pltpu.VMEM((GDN_SEQS, HD, W), f32)],
            seqs=GDN_SEQS)
        yc, s_hgrn = _prompt_call(
            functools.partial(_hgrn_prompt_kernel, l), "hgrn_prompt", xp, T_TILE, (w_hgrn,) + hgrn_params, (), (),
            [(W, bf16)], [state_spec], [state_scratch])
        yd, s_ret = _prompt_call(
            _ret_prompt_kernel, "ret_prompt", xp, T_TILE, (w_ret,) + ret_params, (), (cos_p, sin_p),
            [(W, bf16)], [state_spec], [state_scratch])
        (ye,) = _prompt_call(
            _xattn_prompt_kernel, "xattn_prompt", xp, T_TILE_WIDE, (w_xq,),
            (mkb_all[l].reshape(B, n_mem, W), mvb_all[l].reshape(B, n_mem, W)), (), [(W, bf16)], [], [])
        x2d = xp.reshape(B * T, D)
        ys = [y.reshape(B * T, W) for y in (ya, yb, yc, yd, ye)]
        x2d = _row_call(merge_k, "merge_prompt", [x2d] + ys, merge_consts, D)
        x2d = _row_call(ffn_k, "ffn_prompt", [x2d], ffn_consts, D)
        xp = x2d.reshape(B, T, D)
        for lst, val in zip(outs_p, (h_last.reshape(B, W), lconv, gconv, s_gdn, s_hgrn, s_ret)):
            lst.append(val)

        u_s = _sample_proj(xs, (w_xa, w_qkv, w_z, w_hgrn, w_ret, w_xq, w_ba))
        sample_params = lru_params + gdn_params + hgrn_params + ret_params + (cos_s, sin_s)
        res = _sample_branches(
            l, u_s, state_lru_h[l], state_lru_conv[l].transpose(1, 0, 2), state_gdn_conv[l].transpose(1, 0, 2),
            state_gdn_s, state_hgrn_s, state_ret_s, sample_params, sample_states)
        ys_s, (h1, lconv_s, gconv_s), sample_states = res[:4], res[4:7], tuple(res[7:])
        xq_off = sum(w.shape[1] for w in (w_xa, w_qkv, w_z, w_hgrn, w_ret))
        q_s = u_s[:, xq_off:xq_off + W].reshape(Bs, H, HD)
        ye_s = _sample_xattn(l, q_s, cache_mem_k, cache_mem_v).reshape(Bs, W)
        xs = _row_call(merge_k, "merge_sample", [xs] + list(ys_s) + [ye_s], merge_consts, D)
        xs = _row_call(ffn_k, "ffn_sample", [xs], ffn_consts, D)
        for lst, val in zip(outs_s, (h1, lconv_s.transpose(1, 0, 2), gconv_s.transpose(1, 0, 2))):
            lst.append(val)

    mem_shape = (depth, B, n_mem, H, HD)
    return (xp, xs.reshape(Bs, 1, D), *(jnp.stack(v) for v in outs_p), mk_all.reshape(mem_shape),
            mv_all.reshape(mem_shape), *(jnp.stack(v) for v in outs_s), *sample_states)
```

```python
import functools
import math

import numpy as np
import jax
import jax.numpy as jnp
from jax import lax
from jax.experimental import pallas as pl
from jax.experimental.pallas import tpu as pltpu

f32 = jnp.float32
bf16 = jnp.bfloat16

N_HEADS = 4
HEAD_DIM = 128
W_BR = N_HEADS * HEAD_DIM
CONV_W = 4
N_BR = 5
LRU_C = 8.0
GDN_CHUNK = 64
HGRN_CHUNK = 16
ROPE_BASE = 10000.0
LN_EPS = 1e-5
NORM_EPS = 1e-6
PAST_LEN = 16384

T_TILE = 256
T_TILE_WIDE = 512
GDN_SEQS = 2
HGRN_LOOKAHEAD = 2
ROW_TILE = 1024
ROW_SUBTILE = 256
SAMPLE_BLOCK = 8
CARRY_ROWS = 8
VMEM_LIMIT = 52 * 1024 * 1024


def _mm(a, b):
    return jnp.dot(a.astype(bf16), b.astype(bf16), preferred_element_type=f32)


def _mm_nt(a, b):
    return lax.dot_general(a.astype(bf16), b.astype(bf16), (((1,), (1,)), ((), ())), preferred_element_type=f32)


def _mm_tn(a, b):
    return lax.dot_general(a.astype(bf16), b.astype(bf16), (((0,), (0,)), ((), ())), preferred_element_type=f32)


def _mask_mm_exact(mask_bf16, x):
    hi = x.astype(bf16)
    r1 = x - hi.astype(f32)
    mid = r1.astype(bf16)
    lo = (r1 - mid.astype(f32)).astype(bf16)
    dot = functools.partial(jnp.dot, preferred_element_type=f32)
    return dot(mask_bf16, hi) + dot(mask_bf16, mid) + dot(mask_bf16, lo)


def _chunk_masks(n, chunk):
    shift = int(math.log2(chunk))
    row = lax.broadcasted_iota(jnp.int32, (n, n), 0)
    col = lax.broadcasted_iota(jnp.int32, (n, n), 1)
    same = (row >> shift) == (col >> shift)
    return jnp.logical_and(same, col <= row), same


def _as_bf16(mask):
    return jnp.where(mask, 1.0, 0.0).astype(bf16)


def _sigmoid(x):
    return jax.nn.sigmoid(x)


def _silu(x):
    return x * jax.nn.sigmoid(x)


def _layernorm(x, g, b):
    mu = jnp.mean(x, -1, keepdims=True)
    xc = x - mu
    var = jnp.mean(xc * xc, -1, keepdims=True)
    return xc * lax.rsqrt(var + LN_EPS) * g + b


def _head_slices(a):
    return [a[:, h * HEAD_DIM:(h + 1) * HEAD_DIM] for h in range(N_HEADS)]


def _causal_conv(buf_ref, x, cw):
    n = x.shape[0]
    buf_ref[CARRY_ROWS:CARRY_ROWS + n, :] = x
    out = cw[CONV_W - 1:CONV_W, :] * x
    for j in range(CONV_W - 1):
        off = CARRY_ROWS - (CONV_W - 1) + j
        out = out + cw[j:j + 1, :] * buf_ref[off:off + n, :]
    buf_ref[0:CARRY_ROWS, :] = buf_ref[n:n + CARRY_ROWS, :]
    return out


def _conv_state(buf_ref):
    return buf_ref[CARRY_ROWS - (CONV_W - 1):CARRY_ROWS, :]


def _lru_gates(xc, wa_ref, ba, wi_ref, bi, lam):
    xh = _head_slices(xc)
    r = _sigmoid(jnp.concatenate([_mm(xh[h], wa_ref[h]) for h in range(N_HEADS)], -1) + ba)
    ig = _sigmoid(jnp.concatenate([_mm(xh[h], wi_ref[h]) for h in range(N_HEADS)], -1) + bi)
    log_a = -LRU_C * r * jax.nn.softplus(-lam)
    a = jnp.exp(log_a)
    b = jnp.sqrt(jnp.tanh(-log_a) * (1.0 + a * a)) * (ig * xc)
    return a, b


def _gdn_gates(ba, alog_row, dtb_row):
    beta = _sigmoid(ba)
    g = -jnp.exp(alog_row) * jax.nn.softplus(ba + dtb_row)
    return beta, g


def _l2norm(a):
    return a * lax.rsqrt(jnp.sum(a * a, -1, keepdims=True) + NORM_EPS)


def _rmsnorm_heads(o_heads, w):
    parts = [o * lax.rsqrt(jnp.mean(o * o, -1, keepdims=True) + NORM_EPS) for o in o_heads]
    return jnp.concatenate(parts, -1) * w


def _groupnorm_heads(o_heads, w, b):
    parts = []
    for o in o_heads:
        mu = jnp.mean(o, -1, keepdims=True)
        oc = o - mu
        var = jnp.mean(oc * oc, -1, keepdims=True)
        parts.append(oc * lax.rsqrt(var + NORM_EPS))
    return jnp.concatenate(parts, -1) * w + b


def _hgrn_lower_bound(lbraw, layer):
    n = lbraw.shape[0]
    rows = [lbraw[i:i + 1, :] for i in range(n)]
    m = rows[0]
    for r in rows[1:]:
        m = jnp.maximum(m, r)
    es = [jnp.exp(r - m) for r in rows]
    tot = es[0]
    for e in es[1:]:
        tot = tot + e
    acc = jnp.zeros_like(m)
    for i in range(1, layer + 1):
        acc = acc + es[i] / tot
    return acc


def _hgrn_gates(fz, lb):
    log_f = jnp.logaddexp(jnp.log(lb), jnp.log1p(-lb) + jax.nn.log_sigmoid(fz))
    k = (1.0 - lb) * _sigmoid(-fz)
    return log_f, k


def _rotary(x, cos, sin_signed):
    return x * cos + pltpu.roll(x, HEAD_DIM // 2, 1) * sin_signed


def _ret_log_gamma(h):
    return math.log1p(-(2.0 ** (-5.0 - h)))


def _rope_kernel(ang_ref, cos_ref, sin_ref):
    ang = ang_ref[...]
    lane = lax.broadcasted_iota(jnp.int32, ang.shape, 1)
    s = jnp.sin(ang)
    cos_ref[...] = jnp.cos(ang)
    sin_ref[...] = jnp.where(lane < HEAD_DIM // 2, -s, s)


def _rope_tables(positions):
    half = HEAD_DIM // 2
    inv = ROPE_BASE ** (-np.arange(half, dtype=np.float64) / half)
    ang = np.mod(np.asarray(positions, np.float64)[:, None] * inv[None, :], 2.0 * np.pi)
    ang = np.concatenate([ang, ang], axis=1).astype(np.float32)
    n = ang.shape[0]
    pad = (-n) % 8
    ang = np.pad(ang, ((0, pad), (0, 0)))
    shp = jax.ShapeDtypeStruct(ang.shape, f32)
    cos, sin = pl.pallas_call(_rope_kernel, out_shape=(shp, shp), name="rope_tables")(jnp.asarray(ang))
    return cos[:n], sin[:n]


def _memkv_kernel(m_ref, wk_ref, wv_ref, k_ref, v_ref, kb_ref, vb_ref):
    m = m_ref[...]
    k = _mm(m, wk_ref[...])
    v = _mm(m, wv_ref[...])
    k_ref[...] = k
    v_ref[...] = v
    kb_ref[...] = k.astype(bf16)
    vb_ref[...] = v.astype(bf16)


def _memkv(mem2d, wk, wv):
    rows, d = mem2d.shape
    depth = wk.shape[0]
    tile = 256
    w_spec = pl.BlockSpec((None, d, W_BR), lambda l, i: (l, 0, 0))
    o_spec = pl.BlockSpec((None, tile, W_BR), lambda l, i: (l, i, 0))
    return pl.pallas_call(
        _memkv_kernel,
        grid=(depth, rows // tile),
        in_specs=[pl.BlockSpec((tile, d), lambda l, i: (i, 0)), w_spec, w_spec],
        out_specs=[o_spec] * 4,
        out_shape=[jax.ShapeDtypeStruct((depth, rows, W_BR), f32)] * 2
        + [jax.ShapeDtypeStruct((depth, rows, W_BR), bf16)] * 2,
        compiler_params=pltpu.CompilerParams(dimension_semantics=("arbitrary", "arbitrary"),
                                             vmem_limit_bytes=VMEM_LIMIT),
        name="mem_kv",
    )(mem2d, wk, wv)


def _prompt_call(kernel, name, x, tile, consts, per_batch, per_time, outs_tile, outs_batch, scratch, seqs=1):
    B, T, D = x.shape
    assert T % tile == 0 and B % seqs == 0
    nt = T // tile
    in_specs = [pl.BlockSpec((seqs, tile, D), lambda b, t: (b, t, 0))]
    for a in consts:
        in_specs.append(pl.BlockSpec(a.shape, lambda b, t, n=a.ndim: (0,) * n))
    for a in per_batch:
        in_specs.append(pl.BlockSpec((seqs,) + a.shape[1:], lambda b, t, n=a.ndim: (b,) + (0,) * (n - 1)))
    for a in per_time:
        in_specs.append(pl.BlockSpec((tile,) + a.shape[1:], lambda b, t, n=a.ndim: (t,) + (0,) * (n - 1)))
    out_specs, out_shape = [], []
    for width, dt in outs_tile:
        out_specs.append(pl.BlockSpec((seqs, tile, width), lambda b, t: (b, t, 0)))
        out_shape.append(jax.ShapeDtypeStruct((B, T, width), dt))
    for shp, dt in outs_batch:
        out_specs.append(pl.BlockSpec((seqs,) + shp, lambda b, t, n=len(shp): (b,) + (0,) * n))
        out_shape.append(jax.ShapeDtypeStruct((B,) + shp, dt))
    return pl.pallas_call(
        kernel,
        grid=(B // seqs, nt),
        in_specs=in_specs,
        out_specs=out_specs,
        out_shape=out_shape,
        scratch_shapes=scratch,
        compiler_params=pltpu.CompilerParams(dimension_semantics=("arbitrary", "arbitrary"),
                                             vmem_limit_bytes=VMEM_LIMIT),
        name=name,
    )(x, *consts, *per_batch, *per_time)


def _lru_prompt_kernel(x_ref, w_ref, cw_ref, cb_ref, wa_ref, ba_ref, wi_ref, bi_ref, lam_ref,
                       y_ref, h_ref, conv_ref, buf_ref, a_ref, b_ref, hcar_ref):
    n = x_ref.shape[1]

    @pl.when(pl.program_id(1) == 0)
    def _():
        buf_ref[0:CARRY_ROWS, :] = jnp.zeros((CARRY_ROWS, W_BR), f32)
        hcar_ref[...] = jnp.zeros_like(hcar_ref)

    xa = _mm(x_ref[0], w_ref[...])
    xc = _causal_conv(buf_ref, xa, cw_ref[...]) + cb_ref[...]
    a, b = _lru_gates(xc, wa_ref, ba_ref[...], wi_ref, bi_ref[...], lam_ref[...])

    a = a.reshape(n // 8, 8, W_BR)
    b = b.reshape(n // 8, 8, W_BR)
    row8 = lax.broadcasted_iota(jnp.int32, (n // 8, 8, W_BR), 1)
    for d in (1, 2, 4):
        keep = row8 >= d
        a_prev = jnp.where(keep, pltpu.roll(a, d, 1), 1.0)
        b_prev = jnp.where(keep, pltpu.roll(b, d, 1), 0.0)
        b = b + a * b_prev
        a = a * a_prev
    a_ref[...] = a.reshape(n, W_BR)
    b_ref[...] = b.reshape(n, W_BR)

    def group(gi, h_prev):
        i = pl.multiple_of(gi * 8, 8)
        blk = b_ref[pl.ds(i, 8), :] + a_ref[pl.ds(i, 8), :] * h_prev
        b_ref[pl.ds(i, 8), :] = blk
        return jnp.broadcast_to(blk[7:8, :], (8, W_BR))

    h_last = lax.fori_loop(0, n // 8, group, hcar_ref[...])
    hcar_ref[...] = h_last
    y_ref[0] = b_ref[...].astype(y_ref.dtype)
    h_ref[0] = h_last[0:1, :]
    conv_ref[0] = _conv_state(buf_ref)


def _gdn_prompt_kernel(x_ref, wqkv_ref, wz_ref, wba_ref, cw_ref, alog_ref, dtb_ref, nw_ref,
                       y_ref, conv_ref, s_ref, buf_ref, st_ref):
    seqs, n_seq, C = x_ref.shape[0], x_ref.shape[1], GDN_CHUNK
    assert n_seq == N_HEADS * C
    n = seqs * n_seq

    @pl.when(pl.program_id(1) == 0)
    def _():
        buf_ref[:, 0:CARRY_ROWS, :] = jnp.zeros((seqs, CARRY_ROWS, 3 * W_BR), f32)
        st_ref[...] = jnp.zeros_like(st_ref)

    x_seq = [x_ref[s].astype(bf16) for s in range(seqs)]
    x = jnp.concatenate(x_seq, 0)
    qkv = [_mm(x_seq[s], wqkv_ref[...]) for s in range(seqs)]
    c = jnp.concatenate([_silu(_causal_conv(buf_ref.at[s], qkv[s], cw_ref[...])) for s in range(seqs)], 0)
    z = _mm(x, wz_ref[...])
    beta, g = _gdn_gates(_mm(x, wba_ref[...]), alog_ref[...], dtb_ref[...])

    incl_t = _as_bf16(_chunk_masks(n_seq, C)[0])
    G_seq = [_mask_mm_exact(incl_t, g[s * n_seq:(s + 1) * n_seq]) for s in range(seqs)]
    G = jnp.concatenate(G_seq, 0)
    GT = [a.T for a in G_seq]
    H, HD, W = N_HEADS, HEAD_DIM, W_BR
    nck = n // C
    nck_seq = n_seq // C
    heads = range(H)
    tile = lambda a, h: a[:, h * HD:(h + 1) * HD]

    q_all = jnp.concatenate([_l2norm(tile(c, h)) * HD ** -0.5 for h in heads], 1)
    k_all = jnp.concatenate([_l2norm(tile(c, H + h)) for h in heads], 1)
    v_all = c[:, 2 * W:]
    per_head = lambda a, lo: jnp.concatenate([jnp.broadcast_to(a[:, lo + h:lo + h + 1], (n, HD)) for h in heads], 1)
    beta_w = per_head(beta, 0)
    g_w = per_head(G, H)
    glast_w = jnp.concatenate([jnp.broadcast_to(g_w[(ck + 1) * C - 1:(ck + 1) * C, :], (C, W)) for ck in range(nck)], 0)
    eg_w = jnp.exp(g_w)
    kb_all = k_all * beta_w
    vb_all = v_all * beta_w
    kbe_all = kb_all * eg_w
    qe_all = q_all * eg_w
    kend_all = k_all * jnp.exp(glast_w - g_w)
    eglast_w = jnp.exp(glast_w)

    lane_blk = lax.broadcasted_iota(jnp.int32, (n, H * C), 1) >> int(math.log2(C))
    gcol_c = G[:, H + H - 1:H + H]
    for h in reversed(range(H - 1)):
        gcol_c = jnp.where(lane_blk == h, G[:, H + h:H + h + 1], gcol_c)
    ri = lax.broadcasted_iota(jnp.int32, (C, H * C), 0)
    lj = lax.broadcasted_iota(jnp.int32, (C, H * C), 1)
    incl, strict = (lj & (C - 1)) <= ri, (lj & (C - 1)) < ri
    row_blk = lax.broadcasted_iota(jnp.int32, (1, H * C), 1) >> int(math.log2(C))
    bd_mask = (lax.broadcasted_iota(jnp.int32, (H * C, H * C), 0) >> int(math.log2(C))) == (
        lax.broadcasted_iota(jnp.int32, (H * C, H * C), 1) >> int(math.log2(C)))

    def block_diag(parts, width):
        zero = jnp.zeros((parts[0].shape[0], width), parts[0].dtype)
        return jnp.concatenate([jnp.concatenate([parts[h] if h == hp else zero for h in heads], 1) for hp in heads], 0)

    def block_diag_small(m_b):
        return jnp.where(bd_mask, jnp.concatenate([m_b] * H, 0), jnp.zeros((), bf16))

    neg_a, a_qk, rhs_w, rhs_bd = [], [], [], []
    for ck in range(nck):
        sl = slice(ck * C, (ck + 1) * C)
        gt, cs = GT[ck // nck_seq], ck % nck_seq
        g_row = None
        for h in reversed(heads):
            r = gt[H + h:H + h + 1, :]
            r = pltpu.roll(r, ((h - cs) * C) % n_seq, 1) if (h - cs) % nck_seq else r
            g_row = r if g_row is None else jnp.where(row_blk == h, r, g_row)
        decay = jnp.where(incl, jnp.exp(gcol_c[sl] - g_row), 0.0)
        k_b = k_all[sl].astype(bf16)
        k_bd = block_diag([tile(k_b, h) for h in heads], HD)
        neg_a.append(jnp.where(strict, -(_mm_nt(kb_all[sl], k_bd) * decay), 0.0))
        a_qk.append(_mm_nt(q_all[sl], k_bd) * decay)
        parts = [jnp.concatenate([tile(vb_all[sl], h), tile(kbe_all[sl], h)], 1) for h in heads]
        rhs_w.append(jnp.concatenate(parts, 1))
        rhs_bd.append(block_diag([p.astype(bf16) for p in parts], 2 * HD))

    dot = functools.partial(jnp.dot, preferred_element_type=f32)
    m = list(neg_a)
    t_off = list(neg_a)
    m_bd = [block_diag_small(a.astype(bf16)) for a in m]
    for _ in range(int(math.log2(C)) - 1):
        m = [dot(m[ck].astype(bf16), m_bd[ck]) for ck in range(nck)]
        m_bd = [block_diag_small(a.astype(bf16)) for a in m]
        t_off = [t_off[ck] + m[ck] + dot(t_off[ck].astype(bf16), m_bd[ck]) for ck in range(nck)]
    sol = [rhs_w[ck] + dot(t_off[ck].astype(bf16), rhs_bd[ck]) for ck in range(nck)]

    S = [st_ref[s] for s in range(seqs)]
    zero_s = jnp.zeros((HD, HD), bf16)
    o_chunks = [None] * nck
    for cs in range(nck_seq):
        for s in range(seqs):
            ck = s * nck_seq + cs
            sl = slice(ck * C, (ck + 1) * C)
            S_b = S[s].astype(bf16)
            s_pairs = [jnp.concatenate([jnp.concatenate([tile(S_b, p), zero_s], 1),
                                        jnp.concatenate([zero_s, tile(S_b, p + 1)], 1)], 0) for p in (0, 2)]
            u = [sol[ck][:, 2 * h * HD:(2 * h + 1) * HD] for h in heads]
            w = [sol[ck][:, (2 * h + 1) * HD:(2 * h + 2) * HD].astype(bf16) for h in heads]
            w_s = jnp.concatenate([dot(jnp.concatenate([w[p], w[p + 1]], 1), s_pairs[p // 2]) for p in (0, 2)], 1)
            v_new = jnp.concatenate(u, 1) - w_s
            v_b = v_new.astype(bf16)
            v_bd = block_diag([tile(v_b, h) for h in heads], HD)
            qe_b = qe_all[sl].astype(bf16)
            q_s = jnp.concatenate([dot(qe_b[:, p * HD:(p + 2) * HD], s_pairs[p // 2]) for p in (0, 2)], 1)
            o_chunks[ck] = q_s + dot(a_qk[ck].astype(bf16), v_bd)
            kend_stack = jnp.concatenate([tile(kend_all[sl], h) for h in heads], 0)
            S[s] = eglast_w[ck * C:ck * C + 1, :] * S[s] + _mm_tn(kend_stack, v_bd)
    o_all = jnp.concatenate(o_chunks, 0)
    y = _rmsnorm_heads([tile(o_all, h) for h in heads], nw_ref[...]) * _silu(z)
    for s in range(seqs):
        st_ref[s] = S[s]
        for h in heads:
            s_ref[s, h] = tile(S[s], h)
        y_ref[s] = y[s * n_seq:(s + 1) * n_seq].astype(y_ref.dtype)
        conv_ref[s] = _conv_state(buf_ref.at[s])


def _hgrn_prompt_kernel(layer, x_ref, w_ref, lbraw_ref, nw_ref, y_ref, s_ref, st_ref):
    n, C = x_ref.shape[1], HGRN_CHUNK

    @pl.when(pl.program_id(1) == 0)
    def _():
        st_ref[...] = jnp.zeros_like(st_ref)

    u = _mm(x_ref[0], w_ref[...])
    hq, fz, hi, hg = (u[:, i * W_BR:(i + 1) * W_BR] for i in range(4))
    lb = _hgrn_lower_bound(lbraw_ref[...], layer)
    log_f, k = _hgrn_gates(fz, lb)
    q = _silu(hq)

    incl, _ = _chunk_masks(n, C)
    G = _mask_mm_exact(_as_bf16(incl), log_f)
    g_last = [G[(ck + 1) * C - 1:(ck + 1) * C, :] for ck in range(n // C)]
    Gtot = jnp.concatenate([jnp.broadcast_to(r, (C, W_BR)) for r in g_last], 0)
    q_g = q * jnp.exp(G)
    k_g = k * jnp.exp(-G)
    k_end = k * jnp.exp(Gtot - G)
    dT = jnp.exp(jnp.concatenate(g_last, 0)).T

    heads = range(N_HEADS)
    hsl = [slice(h * HEAD_DIM, (h + 1) * HEAD_DIM) for h in heads]
    csl = [slice(ck * C, (ck + 1) * C) for ck in range(n // C)]
    a_qk = [jnp.where(incl, _mm_nt(q_g[:, hs], k_g[:, hs]), 0.0) for hs in hsl]
    o_intra = [_mm(a_qk[h], hi[:, hsl[h]]) for h in heads]
    kv = {}
    for ck in range(min(HGRN_LOOKAHEAD, n // C)):
        for h in heads:
            kv[ck, h] = _mm_tn(k_end[csl[ck], hsl[h]], hi[csl[ck], hsl[h]])
    S = [st_ref[h] for h in heads]
    o_inter = {}
    for ck in range(n // C):
        if ck + HGRN_LOOKAHEAD < n // C:
            for h in heads:
                kv[ck + HGRN_LOOKAHEAD, h] = _mm_tn(k_end[csl[ck + HGRN_LOOKAHEAD], hsl[h]],
                                                    hi[csl[ck + HGRN_LOOKAHEAD], hsl[h]])
        for h in heads:
            o_inter[ck, h] = _mm(q_g[csl[ck], hsl[h]], S[h])
        S = [dT[hsl[h], ck:ck + 1] * S[h] + kv.pop((ck, h)) for h in heads]
    for h in heads:
        st_ref[h] = S[h]
        s_ref[0, h] = S[h]
    o_heads = [o_intra[h] + jnp.concatenate([o_inter[ck, h] for ck in range(n // C)], 0) for h in heads]

    y = _rmsnorm_heads(o_heads, nw_ref[...]) * _sigmoid(hg)
    y_ref[0] = y.astype(y_ref.dtype)


def _ret_prompt_kernel(x_ref, w_ref, gw_ref, gb_ref, cos_ref, sin_ref, y_ref, s_ref, st_ref):
    n = x_ref.shape[1]

    @pl.when(pl.program_id(1) == 0)
    def _():
        st_ref[...] = jnp.zeros_like(st_ref)

    xb = x_ref[0].astype(bf16)

    def project(pair):
        lo = 2 * pair * HEAD_DIM
        return [_mm(xb, w_ref[:, i * W_BR + lo:i * W_BR + lo + 2 * HEAD_DIM]) for i in range(4)]

    cos, sin = cos_ref[...], sin_ref[...]
    ri = lax.broadcasted_iota(jnp.int32, (n, n), 0)
    rj = lax.broadcasted_iota(jnp.int32, (n, n), 1)
    dist = (ri - rj).astype(f32)
    pos = lax.broadcasted_iota(jnp.int32, (n, 1), 0).astype(f32)

    heads = range(N_HEADS)
    proj = [project(p) for p in range(N_HEADS // 2)]
    piece = lambda i, h: proj[h // 2][i][:, (h % 2) * HEAD_DIM:(h % 2 + 1) * HEAD_DIM]
    lg = [_ret_log_gamma(h) for h in heads]
    q = [_rotary(piece(0, h), cos, sin) for h in heads]
    k = [_rotary(piece(1, h), cos, sin) * HEAD_DIM ** -0.5 for h in heads]
    v = [piece(2, h) for h in heads]
    a_qk = [_mm_nt(q[h], k[h]) * jnp.where(rj <= ri, jnp.exp(dist * lg[h]), 0.0) for h in heads]
    S = [st_ref[h] for h in heads]
    o = [_mm(a_qk[h], v[h]) + _mm(q[h] * jnp.exp((pos + 1.0) * lg[h]), S[h]) for h in heads]
    S = [math.exp(n * lg[h]) * S[h] + _mm_tn(k[h] * jnp.exp((n - 1.0 - pos) * lg[h]), v[h]) for h in heads]
    for h in heads:
        st_ref[h] = S[h]
        s_ref[0, h] = S[h]
    rg = jnp.concatenate([piece(3, h) for h in heads], 1)
    y = _groupnorm_heads(o, gw_ref[...], gb_ref[...]) * _silu(rg)
    y_ref[0] = y.astype(y_ref.dtype)


def _xattn_prompt_kernel(x_ref, w_ref, mk_ref, mv_ref, y_ref):
    q = _mm(x_ref[0], w_ref[...])
    mk, mv = mk_ref[0], mv_ref[0]
    hsl = [slice(h * HEAD_DIM, (h + 1) * HEAD_DIM) for h in range(N_HEADS)]
    s = [_mm_nt(q[:, hs], mk[:, hs]) * HEAD_DIM ** -0.5 for hs in hsl]
    e = [jnp.exp(a - jnp.max(a, -1, keepdims=True)) for a in s]
    p = [a / jnp.sum(a, -1, keepdims=True) for a in e]
    outs = [_mm(a, mv[:, hs]) for a, hs in zip(p, hsl)]
    y_ref[0] = jnp.concatenate(outs, -1).astype(y_ref.dtype)


def _row_blocks(n):
    sub = min(ROW_SUBTILE, n)
    return [slice(r, r + sub) for r in range(0, n, sub)]


def _merge_kernel(alpha, x_ref, ya_ref, yb_ref, yc_ref, yd_ref, ye_ref, wg_ref, bg_ref, wb_ref, wo_ref,
                  g_ref, b_ref, o_ref):
    for rs in _row_blocks(x_ref.shape[0]):
        x = x_ref[rs, :]
        xb = x.astype(bf16)
        merged = None
        for i, y_ref in enumerate((ya_ref, yb_ref, yc_ref, yd_ref, ye_ref)):
            gate = _sigmoid(_mm(xb, wg_ref[i]) + bg_ref[i])
            term = gate * _mm(y_ref[rs, :], wb_ref[i])
            merged = term if merged is None else merged + term
        mix = _mm(merged, wo_ref[...])
        o_ref[rs, :] = _layernorm(alpha * x + mix, g_ref[...], b_ref[...])


def _ffn_kernel(alpha, n_split, x_ref, wg_ref, wv_ref, wd_ref, g_ref, b_ref, o_ref):
    d_ff = wg_ref.shape[1]
    step = d_ff // n_split
    for rs in _row_blocks(x_ref.shape[0]):
        x = x_ref[rs, :]
        xb = x.astype(bf16)
        ffn = None
        for i in range(n_split):
            cs = slice(i * step, (i + 1) * step)
            hidden = _silu(_mm(xb, wg_ref[:, cs])) * _mm(xb, wv_ref[:, cs])
            part = _mm(hidden, wd_ref[cs, :])
            ffn = part if ffn is None else ffn + part
        o_ref[rs, :] = _layernorm(alpha * x + ffn, g_ref[...], b_ref[...])


def _row_call(kernel, name, row_ops, consts, d_out):
    rows = row_ops[0].shape[0]
    tile = min(ROW_TILE, rows)
    in_specs = [pl.BlockSpec((tile, a.shape[1]), lambda i: (i, 0)) for a in row_ops]
    in_specs += [pl.BlockSpec(a.shape, lambda i, n=a.ndim: (0,) * n, pipeline_mode=pl.Buffered(1)) for a in consts]
    return pl.pallas_call(
        kernel,
        grid=(rows // tile,),
        in_specs=in_specs,
        out_specs=pl.BlockSpec((tile, d_out), lambda i: (i, 0)),
        out_shape=jax.ShapeDtypeStruct((rows, d_out), f32),
        compiler_params=pltpu.CompilerParams(dimension_semantics=("arbitrary",), vmem_limit_bytes=VMEM_LIMIT),
        name=name,
    )(*row_ops, *consts)


def _proj_kernel(x_ref, *refs):
    w_refs, o_ref = refs[:-1], refs[-1]
    x = x_ref[...].astype(bf16)
    off = 0
    for w_ref in w_refs:
        width = w_ref.shape[1]
        o_ref[:, off:off + width] = _mm(x, w_ref[...])
        off += width


def _sample_proj(x, weights):
    rows = x.shape[0]
    n = sum(w.shape[1] for w in weights)
    return pl.pallas_call(
        _proj_kernel,
        out_shape=jax.ShapeDtypeStruct((rows, n), f32),
        compiler_params=pltpu.CompilerParams(vmem_limit_bytes=VMEM_LIMIT),
        name="sample_proj",
    )(x, *weights)


def _columns(rows_list):
    nb = rows_list[0].shape[0]
    out = []
    for b in range(nb):
        tile = jnp.concatenate([r[b:b + 1, :] for r in rows_list]
                               + [jnp.zeros((8 - len(rows_list), HEAD_DIM), f32)], 0)
        out.append(tile.T)
    return out


N_SAMPLE_INPUTS = 24


def _rows_of(parts):
    nb = 1 + max(bi for bi, _ in parts)
    return [jnp.concatenate([parts[bi, h] for bi in range(nb)], 0) for h in range(N_HEADS)]


def _sample_kernel(layer, n_alias, *refs):
    (u_ref, h0_ref, lconv_ref, gconv_ref, sg_ref, sh_ref, sr_ref,
     lcw_ref, lcb_ref, lwa_ref, lba_ref, lwi_ref, lbi_ref, lam_ref,
     gcw_ref, alog_ref, dtb_ref, gnw_ref, lbraw_ref, hnw_ref, rgw_ref, rgb_ref, cos_ref, sin_ref) = refs[:N_SAMPLE_INPUTS]
    (ya_ref, yb_ref, yc_ref, yd_ref, h1_ref, lconv_o_ref, gconv_o_ref,
     sg_o_ref, sh_o_ref, sr_o_ref) = refs[N_SAMPLE_INPUTS + n_alias:]
    if n_alias == 0:
        for o_ref in (sg_o_ref, sh_o_ref, sr_o_ref):
            for other in range(o_ref.shape[0]):
                if other != layer:
                    o_ref[other] = jnp.zeros(o_ref.shape[1:], f32)
        sg_o_ref, sh_o_ref, sr_o_ref = (r.at[layer] for r in (sg_o_ref, sh_o_ref, sr_o_ref))
    nb = SAMPLE_BLOCK
    W = W_BR
    pairs = [(bi, h) for bi in range(nb) for h in range(N_HEADS)]
    u = u_ref[...]
    off = 0

    def take(width):
        nonlocal off
        part = u[:, off:off + width]
        off += width
        return part

    xa, qkv, z = take(W), take(3 * W), take(W)
    hq, fz, hi, hg = take(W), take(W), take(W), take(W)
    rq, rk, rv, rg = take(W), take(W), take(W), take(W)
    xq, ba = take(W), take(HEAD_DIM)

    lcw = lcw_ref[...]
    xc = lcw[CONV_W - 1:CONV_W, :] * xa + lcb_ref[...]
    for j in range(CONV_W - 1):
        xc = xc + lcw[j:j + 1, :] * lconv_ref[j]
    for j in range(CONV_W - 2):
        lconv_o_ref[j] = lconv_ref[j + 1]
    lconv_o_ref[CONV_W - 2] = xa
    a, b = _lru_gates(xc, lwa_ref, lba_ref[...], lwi_ref, lbi_ref[...], lam_ref[...])
    h1 = a * h0_ref[...] + b
    h1_ref[...] = h1
    ya_ref[...] = h1

    gcw = gcw_ref[...]
    c = gcw[CONV_W - 1:CONV_W, :] * qkv
    for j in range(CONV_W - 1):
        c = c + gcw[j:j + 1, :] * gconv_ref[j]
    for j in range(CONV_W - 2):
        gconv_o_ref[j] = gconv_ref[j + 1]
    gconv_o_ref[CONV_W - 2] = qkv
    c = _silu(c)
    beta, g = _gdn_gates(ba, alog_ref[...], dtb_ref[...])
    eg = jnp.exp(g)
    qn, kn, vv, kcol = [], [], [], {}
    for h in range(N_HEADS):
        qn.append(_l2norm(c[:, h * HEAD_DIM:(h + 1) * HEAD_DIM]) * HEAD_DIM ** -0.5)
        kn.append(_l2norm(c[:, W + h * HEAD_DIM:W + (h + 1) * HEAD_DIM]))
        vv.append(c[:, 2 * W + h * HEAD_DIM:2 * W + (h + 1) * HEAD_DIM])
        cols = _columns([kn[h]])
        for bi in range(nb):
            kcol[bi, h] = cols[bi][:, 0:1]
    kS = _rows_of({(bi, h): _mm(kn[h][bi:bi + 1, :], sg_ref[bi, h]) for bi, h in pairs})
    v_new = [beta[:, h:h + 1] * (vv[h] - eg[:, N_HEADS + h:N_HEADS + h + 1] * kS[h]) for h in range(N_HEADS)]
    S1 = {}
    for bi, h in pairs:
        S1[bi, h] = (eg[bi:bi + 1, N_HEADS + h:N_HEADS + h + 1] * sg_ref[bi, h]
                     + kcol[bi, h] * v_new[h][bi:bi + 1, :])
        sg_o_ref[bi, h] = S1[bi, h]
    o = _rows_of({(bi, h): _mm(qn[h][bi:bi + 1, :], S1[bi, h]) for bi, h in pairs})
    yb_ref[...] = _rmsnorm_heads(o, gnw_ref[...]) * _silu(z)

    lb = _hgrn_lower_bound(lbraw_ref[...], layer)
    log_f, kk = _hgrn_gates(fz, lb)
    f = jnp.exp(log_f)
    qs = _silu(hq)
    kcol, fcol = {}, {}
    for h in range(N_HEADS):
        hs = slice(h * HEAD_DIM, (h + 1) * HEAD_DIM)
        cols = _columns([kk[:, hs], f[:, hs]])
        for bi in range(nb):
            kcol[bi, h], fcol[bi, h] = cols[bi][:, 0:1], cols[bi][:, 1:2]
    S1 = {}
    for bi, h in pairs:
        hs = slice(h * HEAD_DIM, (h + 1) * HEAD_DIM)
        S1[bi, h] = fcol[bi, h] * sh_ref[bi, h] + kcol[bi, h] * hi[bi:bi + 1, hs]
        sh_o_ref[bi, h] = S1[bi, h]
    o = _rows_of({(bi, h): _mm(qs[bi:bi + 1, h * HEAD_DIM:(h + 1) * HEAD_DIM], S1[bi, h]) for bi, h in pairs})
    yc_ref[...] = _rmsnorm_heads(o, hnw_ref[...]) * _sigmoid(hg)

    cos, sin = cos_ref[...], sin_ref[...]
    qr, kcol = [], {}
    for h in range(N_HEADS):
        hs = slice(h * HEAD_DIM, (h + 1) * HEAD_DIM)
        qr.append(_rotary(rq[:, hs], cos, sin))
        cols = _columns([_rotary(rk[:, hs], cos, sin) * HEAD_DIM ** -0.5])
        for bi in range(nb):
            kcol[bi, h] = cols[bi][:, 0:1]
    S1 = {}
    for bi, h in pairs:
        hs = slice(h * HEAD_DIM, (h + 1) * HEAD_DIM)
        S1[bi, h] = math.exp(_ret_log_gamma(h)) * sr_ref[bi, h] + kcol[bi, h] * rv[bi:bi + 1, hs]
        sr_o_ref[bi, h] = S1[bi, h]
    o = _rows_of({(bi, h): _mm(qr[h][bi:bi + 1, :], S1[bi, h]) for bi, h in pairs})
    yd_ref[...] = _groupnorm_heads(o, rgw_ref[...], rgb_ref[...]) * _silu(rg)


def _sample_xattn_kernel(q_ref, ck_ref, cv_ref, o_ref):
    for bi in range(q_ref.shape[0]):
        q = q_ref[bi] * HEAD_DIM ** -0.5
        s = jnp.sum(ck_ref[bi] * q[None, :, :], axis=-1, keepdims=True)
        e = jnp.exp(s - jnp.max(s, axis=0, keepdims=True))
        o_ref[bi] = jnp.sum(e * cv_ref[bi], axis=0) / jnp.sum(e, axis=0)


def _sample_xattn(layer, q, ck, cv):
    nb = SAMPLE_BLOCK
    cache_spec = pl.BlockSpec((None, nb) + ck.shape[2:], lambda i: (layer, i, 0, 0, 0))
    q_spec = pl.BlockSpec((nb,) + q.shape[1:], lambda i: (i, 0, 0))
    return pl.pallas_call(
        _sample_xattn_kernel,
        grid=(q.shape[0] // nb,),
        in_specs=[q_spec, cache_spec, cache_spec],
        out_specs=q_spec,
        out_shape=jax.ShapeDtypeStruct(q.shape, f32),
        compiler_params=pltpu.CompilerParams(dimension_semantics=("arbitrary",), vmem_limit_bytes=VMEM_LIMIT),
        name="sample_xattn",
    )(q, ck, cv)


def _sample_branches(layer, u, h0, lconv_t, gconv_t, sg, sh, sr, params, prev_states):
    nb = SAMPLE_BLOCK
    rows = u.shape[0]
    blk2 = lambda a: pl.BlockSpec((nb, a.shape[1]), lambda i: (i, 0))
    blk_t = lambda a: pl.BlockSpec((a.shape[0], nb, a.shape[2]), lambda i: (0, i, 0))
    blk_layer = lambda a: pl.BlockSpec((None, nb) + a.shape[2:],
                                       lambda i, n=a.ndim: (layer, i) + (0,) * (n - 2))
    full = lambda a: pl.BlockSpec(a.shape, lambda i, n=a.ndim: (0,) * n)
    sds = lambda a: jax.ShapeDtypeStruct(a.shape, f32)
    y_shape = jax.ShapeDtypeStruct((rows, W_BR), f32)
    y_spec = pl.BlockSpec((nb, W_BR), lambda i: (i, 0))
    n_in = N_SAMPLE_INPUTS
    assert 7 + len(params) == n_in
    n_small_outs = 7
    blk_state_out = blk_layer if prev_states else (
        lambda a: pl.BlockSpec((a.shape[0], nb) + a.shape[2:], lambda i, n=a.ndim: (0, i) + (0,) * (n - 2)))
    return pl.pallas_call(
        functools.partial(_sample_kernel, layer, len(prev_states)),
        grid=(rows // nb,),
        in_specs=[blk2(u), blk2(h0), blk_t(lconv_t), blk_t(gconv_t), blk_layer(sg), blk_layer(sh), blk_layer(sr)]
        + [full(p) for p in params] + [pl.BlockSpec(memory_space=pl.ANY)] * len(prev_states),
        out_specs=[y_spec] * 4 + [blk2(h0), blk_t(lconv_t), blk_t(gconv_t),
                                  blk_state_out(sg), blk_state_out(sh), blk_state_out(sr)],
        out_shape=[y_shape] * 4 + [sds(h0), sds(lconv_t), sds(gconv_t), sds(sg), sds(sh), sds(sr)],
        input_output_aliases={n_in + j: n_small_outs + j for j in range(len(prev_states))},
        compiler_params=pltpu.CompilerParams(dimension_semantics=("arbitrary",), vmem_limit_bytes=VMEM_LIMIT),
        name="sample_branches",
    )(u, h0, lconv_t, gconv_t, sg, sh, sr, *params, *prev_states)


def kernel(x_prompt, x_sample, mem_prompt, state_lru_h, state_lru_conv, state_gdn_conv, state_gdn_s, state_hgrn_s, state_ret_s, cache_mem_k, cache_mem_v, w_in, lru_conv_w, lru_conv_b, lru_wa, lru_ba, lru_wi, lru_bi, lru_lambda, gdn_conv_w, gdn_a_log, gdn_dt_bias, gdn_norm_w, hgrn_lb_raw, hgrn_norm_w, ret_gn_w, ret_gn_b, w_mem_k, w_mem_v, w_merge_gate, b_merge_gate, w_branch, w_out, ln1_g, ln1_b, w_ffn_up, w_ffn_down, ln2_g, ln2_b):
    B, T, D = x_prompt.shape
    Bs = x_sample.shape[0]
    depth = w_in.shape[0]
    n_mem = mem_prompt.shape[1]
    W, H, HD = W_BR, N_HEADS, HEAD_DIM
    d_ff = w_ffn_down.shape[1]
    alpha = (2 * depth) ** 0.25
    assert Bs % SAMPLE_BLOCK == 0 and (B * T) % ROW_TILE == 0

    cos_all, sin_all = _rope_tables(list(range(T)) + [PAST_LEN])
    cos_p, sin_p = cos_all[:T], sin_all[:T]
    cos_s, sin_s = cos_all[T:T + 1], sin_all[T:T + 1]

    row = lambda a: a.reshape(1, -1)
    state_spec = ((H, HD, HD), f32)
    state_scratch = pltpu.VMEM((H, HD, HD), f32)

    xp = x_prompt
    xs = x_sample.reshape(Bs, D)
    mk_all, mv_all, mkb_all, mvb_all = _memkv(mem_prompt.reshape(B * n_mem, D), w_mem_k, w_mem_v)
    outs_p = [[] for _ in range(6)]
    outs_s = [[] for _ in range(3)]
    sample_states = ()

    for l in range(depth):
        wl = w_in[l]
        o = 0
        pieces = []
        for width in (W, 3 * W, W, H, H, W, W, W, W, W, W, W, W, W):
            pieces.append(wl[:, o:o + width])
            o += width
        (w_xa, w_qkv, w_z, w_gb, w_ga, w_hq, w_hf, w_hi, w_hg, w_rq, w_rk, w_rv, w_rg, w_xq) = pieces
        w_ba = jnp.concatenate([w_gb, w_ga, jnp.zeros((D, HD - 2 * H), f32)], 1).astype(bf16)
        w_xa, w_qkv, w_z, w_xq = (a.astype(bf16) for a in (w_xa, w_qkv, w_z, w_xq))
        w_hgrn = jnp.concatenate([w_hq, w_hf, w_hi, w_hg], 1).astype(bf16)
        w_ret = jnp.concatenate([w_rq, w_rk, w_rv, w_rg], 1).astype(bf16)

        lru_params = (lru_conv_w[l], row(lru_conv_b[l]), lru_wa[l].astype(bf16), row(lru_ba[l]),
                      lru_wi[l].astype(bf16), row(lru_bi[l]), row(lru_lambda[l]))
        pad_lanes = lambda a, start: jnp.zeros((1, HD), f32).at[0, start:start + H].set(a)
        alog_row, dtb_row = pad_lanes(gdn_a_log[l], H), pad_lanes(gdn_dt_bias[l], H)
        gdn_params = (gdn_conv_w[l], alog_row, dtb_row, row(gdn_norm_w[l]))
        hgrn_params = (hgrn_lb_raw, row(hgrn_norm_w[l]))
        ret_params = (row(ret_gn_w[l]), row(ret_gn_b[l]))
        merge_consts = (w_merge_gate[l].astype(bf16), b_merge_gate[l].reshape(N_BR, 1, D), w_branch[l].astype(bf16),
                        w_out[l].astype(bf16), row(ln1_g[l]), row(ln1_b[l]))
        ffn_consts = (w_ffn_up[l][:, :d_ff].astype(bf16), w_ffn_up[l][:, d_ff:].astype(bf16),
                      w_ffn_down[l].astype(bf16), row(ln2_g[l]), row(ln2_b[l]))
        merge_k = functools.partial(_merge_kernel, alpha)
        ffn_k = functools.partial(_ffn_kernel, alpha, 2)

        ya, h_last, lconv = _prompt_call(
            _lru_prompt_kernel, "lru_prompt", xp, T_TILE_WIDE, (w_xa,) + lru_params, (), (),
            [(W, bf16)], [((1, W), f32), ((CONV_W - 1, W), f32)],
            [pltpu.VMEM((CARRY_ROWS + T_TILE_WIDE, W), f32), pltpu.VMEM((T_TILE_WIDE, W), f32),
             pltpu.VMEM((T_TILE_WIDE, W), f32), pltpu.VMEM((8, W), f32)])
        yb, gconv, s_gdn = _prompt_call(
            _gdn_prompt_kernel, "gdn_prompt", xp, T_TILE, (w_qkv, w_z, w_ba) + gdn_params, (), (),
            [(W, bf16)], [((CONV_W - 1, 3 * W), f32), state_spec],
            [pltpu.VMEM((GDN_SEQS, CARRY_ROWS + T_TILE, 3 * W), f32), pltpu.VMEM((GDN_SEQS, HD, W), f32)],
            seqs=GDN_SEQS)
        yc, s_hgrn = _prompt_call(
            functools.partial(_hgrn_prompt_kernel, l), "hgrn_prompt", xp, T_TILE, (w_hgrn,) + hgrn_params, (), (),
            [(W, bf16)], [state_spec], [state_scratch])
        yd, s_ret = _prompt_call(
            _ret_prompt_kernel, "ret_prompt", xp, T_TILE, (w_ret,) + ret_params, (), (cos_p, sin_p),
            [(W, bf16)], [state_spec], [state_scratch])
        (ye,) = _prompt_call(
            _xattn_prompt_kernel, "xattn_prompt", xp, T_TILE_WIDE, (w_xq,),
            (mkb_all[l].reshape(B, n_mem, W), mvb_all[l].reshape(B, n_mem, W)), (), [(W, bf16)], [], [])
        x2d = xp.reshape(B * T, D)
        ys = [y.reshape(B * T, W) for y in (ya, yb, yc, yd, ye)]
        x2d = _row_call(merge_k, "merge_prompt", [x2d] + ys, merge_consts, D)
        x2d = _row_call(ffn_k, "ffn_prompt", [x2d], ffn_consts, D)
        xp = x2d.reshape(B, T, D)
        for lst, val in zip(outs_p, (h_last.reshape(B, W), lconv, gconv, s_gdn, s_hgrn, s_ret)):
            lst.append(val)

        u_s = _sample_proj(xs, (w_xa, w_qkv, w_z, w_hgrn, w_ret, w_xq, w_ba))
        sample_params = lru_params + gdn_params + hgrn_params + ret_params + (cos_s, sin_s)
        res = _sample_branches(
            l, u_s, state_lru_h[l], state_lru_conv[l].transpose(1, 0, 2), state_gdn_conv[l].transpose(1, 0, 2),
            state_gdn_s, state_hgrn_s, state_ret_s, sample_params, sample_states)
        ys_s, (h1, lconv_s, gconv_s), sample_states = res[:4], res[4:7], tuple(res[7:])
        xq_off = sum(w.shape[1] for w in (w_xa, w_qkv, w_z, w_hgrn, w_ret))
        q_s = u_s[:, xq_off:xq_off + W].reshape(Bs, H, HD)
        ye_s = _sample_xattn(l, q_s, cache_mem_k, cache_mem_v).reshape(Bs, W)
        xs = _row_call(merge_k, "merge_sample", [xs] + list(ys_s) + [ye_s], merge_consts, D)
        xs = _row_call(ffn_k, "ffn_sample", [xs], ffn_consts, D)
        for lst, val in zip(outs_s, (h1, lconv_s.transpose(1, 0, 2), gconv_s.transpose(1, 0, 2))):
            lst.append(val)

    mem_shape = (depth, B, n_mem, H, HD)
    return (xp, xs.reshape(Bs, 1, D), *(jnp.stack(v) for v in outs_p), mk_all.reshape(mem_shape),
            mv_all.reshape(mem_shape), *(jnp.stack(v) for v in outs_s), *sample_states)
```

```python
import functools
import math

import numpy as np
import jax
import jax.numpy as jnp
from jax import lax
from jax.experimental import pallas as pl
from jax.experimental.pallas import tpu as pltpu

f32 = jnp.float32
bf16 = jnp.bfloat16

N_HEADS = 4
HEAD_DIM = 128
W_BR = N_HEADS * HEAD_DIM
CONV_W = 4
N_BR = 5
LRU_C = 8.0
GDN_CHUNK = 64
HGRN_CHUNK = 16
ROPE_BASE = 10000.0
LN_EPS = 1e-5
NORM_EPS = 1e-6
PAST_LEN = 16384

T_TILE = 256
T_TILE_WIDE = 512
GDN_SEQS = 2
RET_SEQS = 2
XATTN_SEQS = 2
HGRN_SEQS = 2
HGRN_LOOKAHEAD = 2
ROW_TILE = 1024
ROW_SUBTILE = 256
SAMPLE_BLOCK = 8
CARRY_ROWS = 8
VMEM_LIMIT = 52 * 1024 * 1024


def _mm(a, b):
    return jnp.dot(a.astype(bf16), b.astype(bf16), preferred_element_type=f32)


def _mm_nt(a, b):
    return lax.dot_general(a.astype(bf16), b.astype(bf16), (((1,), (1,)), ((), ())), preferred_element_type=f32)


def _mm_tn(a, b):
    return lax.dot_general(a.astype(bf16), b.astype(bf16), (((0,), (0,)), ((), ())), preferred_element_type=f32)


def _mask_mm_exact(mask_bf16, x):
    hi = x.astype(bf16)
    r1 = x - hi.astype(f32)
    mid = r1.astype(bf16)
    lo = (r1 - mid.astype(f32)).astype(bf16)
    dot = functools.partial(jnp.dot, preferred_element_type=f32)
    return dot(mask_bf16, hi) + dot(mask_bf16, mid) + dot(mask_bf16, lo)


def _chunk_masks(n, chunk):
    shift = int(math.log2(chunk))
    row = lax.broadcasted_iota(jnp.int32, (n, n), 0)
    col = lax.broadcasted_iota(jnp.int32, (n, n), 1)
    same = (row >> shift) == (col >> shift)
    return jnp.logical_and(same, col <= row), same


def _as_bf16(mask):
    return jnp.where(mask, 1.0, 0.0).astype(bf16)


def _sigmoid(x):
    return jax.nn.sigmoid(x)


def _silu(x):
    return x * jax.nn.sigmoid(x)


def _layernorm(x, g, b):
    mu = jnp.mean(x, -1, keepdims=True)
    xc = x - mu
    var = jnp.mean(xc * xc, -1, keepdims=True)
    return xc * lax.rsqrt(var + LN_EPS) * g + b


def _head_slices(a):
    return [a[:, h * HEAD_DIM:(h + 1) * HEAD_DIM] for h in range(N_HEADS)]


def _causal_conv(buf_ref, x, cw):
    n = x.shape[0]
    buf_ref[CARRY_ROWS:CARRY_ROWS + n, :] = x
    out = cw[CONV_W - 1:CONV_W, :] * x
    for j in range(CONV_W - 1):
        off = CARRY_ROWS - (CONV_W - 1) + j
        out = out + cw[j:j + 1, :] * buf_ref[off:off + n, :]
    buf_ref[0:CARRY_ROWS, :] = buf_ref[n:n + CARRY_ROWS, :]
    return out


def _conv_state(buf_ref):
    return buf_ref[CARRY_ROWS - (CONV_W - 1):CARRY_ROWS, :]


def _lru_gates(xc, wa_ref, ba, wi_ref, bi, lam):
    xh = _head_slices(xc)
    r = _sigmoid(jnp.concatenate([_mm(xh[h], wa_ref[h]) for h in range(N_HEADS)], -1) + ba)
    ig = _sigmoid(jnp.concatenate([_mm(xh[h], wi_ref[h]) for h in range(N_HEADS)], -1) + bi)
    log_a = -LRU_C * r * jax.nn.softplus(-lam)
    a = jnp.exp(log_a)
    b = jnp.sqrt(jnp.tanh(-log_a) * (1.0 + a * a)) * (ig * xc)
    return a, b


def _gdn_gates(ba, alog_row, dtb_row):
    beta = _sigmoid(ba)
    g = -jnp.exp(alog_row) * jax.nn.softplus(ba + dtb_row)
    return beta, g


def _l2norm(a):
    return a * lax.rsqrt(jnp.sum(a * a, -1, keepdims=True) + NORM_EPS)


def _rmsnorm_heads(o_heads, w):
    parts = [o * lax.rsqrt(jnp.mean(o * o, -1, keepdims=True) + NORM_EPS) for o in o_heads]
    return jnp.concatenate(parts, -1) * w


def _groupnorm_heads(o_heads, w, b):
    parts = []
    for o in o_heads:
        mu = jnp.mean(o, -1, keepdims=True)
        oc = o - mu
        var = jnp.mean(oc * oc, -1, keepdims=True)
        parts.append(oc * lax.rsqrt(var + NORM_EPS))
    return jnp.concatenate(parts, -1) * w + b


def _hgrn_lower_bound(lbraw, layer):
    n = lbraw.shape[0]
    rows = [lbraw[i:i + 1, :] for i in range(n)]
    m = rows[0]
    for r in rows[1:]:
        m = jnp.maximum(m, r)
    es = [jnp.exp(r - m) for r in rows]
    tot = es[0]
    for e in es[1:]:
        tot = tot + e
    acc = jnp.zeros_like(m)
    for i in range(1, layer + 1):
        acc = acc + es[i] / tot
    return acc


def _hgrn_gates(fz, lb):
    log_f = jnp.logaddexp(jnp.log(lb), jnp.log1p(-lb) + jax.nn.log_sigmoid(fz))
    k = (1.0 - lb) * _sigmoid(-fz)
    return log_f, k


def _rotary(x, cos, sin_signed):
    return x * cos + pltpu.roll(x, HEAD_DIM // 2, 1) * sin_signed


def _ret_log_gamma(h):
    return math.log1p(-(2.0 ** (-5.0 - h)))


def _rope_kernel(ang_ref, cos_ref, sin_ref):
    ang = ang_ref[...]
    lane = lax.broadcasted_iota(jnp.int32, ang.shape, 1)
    s = jnp.sin(ang)
    cos_ref[...] = jnp.cos(ang)
    sin_ref[...] = jnp.where(lane < HEAD_DIM // 2, -s, s)


def _rope_tables(positions):
    half = HEAD_DIM // 2
    inv = ROPE_BASE ** (-np.arange(half, dtype=np.float64) / half)
    ang = np.mod(np.asarray(positions, np.float64)[:, None] * inv[None, :], 2.0 * np.pi)
    ang = np.concatenate([ang, ang], axis=1).astype(np.float32)
    n = ang.shape[0]
    pad = (-n) % 8
    ang = np.pad(ang, ((0, pad), (0, 0)))
    shp = jax.ShapeDtypeStruct(ang.shape, f32)
    cos, sin = pl.pallas_call(_rope_kernel, out_shape=(shp, shp), name="rope_tables")(jnp.asarray(ang))
    return cos[:n], sin[:n]


def _memkv_kernel(m_ref, wk_ref, wv_ref, k_ref, v_ref, kb_ref, vb_ref):
    m = m_ref[...]
    k = _mm(m, wk_ref[...])
    v = _mm(m, wv_ref[...])
    k_ref[...] = k
    v_ref[...] = v
    kb_ref[...] = k.astype(bf16)
    vb_ref[...] = v.astype(bf16)


def _memkv(mem2d, wk, wv):
    rows, d = mem2d.shape
    depth = wk.shape[0]
    tile = 256
    w_spec = pl.BlockSpec((None, d, W_BR), lambda l, i: (l, 0, 0))
    o_spec = pl.BlockSpec((None, tile, W_BR), lambda l, i: (l, i, 0))
    return pl.pallas_call(
        _memkv_kernel,
        grid=(depth, rows // tile),
        in_specs=[pl.BlockSpec((tile, d), lambda l, i: (i, 0)), w_spec, w_spec],
        out_specs=[o_spec] * 4,
        out_shape=[jax.ShapeDtypeStruct((depth, rows, W_BR), f32)] * 2
        + [jax.ShapeDtypeStruct((depth, rows, W_BR), bf16)] * 2,
        compiler_params=pltpu.CompilerParams(dimension_semantics=("arbitrary", "arbitrary"),
                                             vmem_limit_bytes=VMEM_LIMIT),
        name="mem_kv",
    )(mem2d, wk, wv)


def _prompt_call(kernel, name, x, tile, consts, per_batch, per_time, outs_tile, outs_batch, scratch, seqs=1):
    B, T, D = x.shape
    assert T % tile == 0 and B % seqs == 0
    nt = T // tile
    in_specs = [pl.BlockSpec((seqs, tile, D), lambda b, t: (b, t, 0))]
    for a in consts:
        in_specs.append(pl.BlockSpec(a.shape, lambda b, t, n=a.ndim: (0,) * n))
    for a in per_batch:
        in_specs.append(pl.BlockSpec((seqs,) + a.shape[1:], lambda b, t, n=a.ndim: (b,) + (0,) * (n - 1)))
    for a in per_time:
        in_specs.append(pl.BlockSpec((tile,) + a.shape[1:], lambda b, t, n=a.ndim: (t,) + (0,) * (n - 1)))
    out_specs, out_shape = [], []
    for width, dt in outs_tile:
        out_specs.append(pl.BlockSpec((seqs, tile, width), lambda b, t: (b, t, 0)))
        out_shape.append(jax.ShapeDtypeStruct((B, T, width), dt))
    for shp, dt in outs_batch:
        out_specs.append(pl.BlockSpec((seqs,) + shp, lambda b, t, n=len(shp): (b,) + (0,) * n))
        out_shape.append(jax.ShapeDtypeStruct((B,) + shp, dt))
    return pl.pallas_call(
        kernel,
        grid=(B // seqs, nt),
        in_specs=in_specs,
        out_specs=out_specs,
        out_shape=out_shape,
        scratch_shapes=scratch,
        compiler_params=pltpu.CompilerParams(dimension_semantics=("arbitrary", "arbitrary"),
                                             vmem_limit_bytes=VMEM_LIMIT),
        name=name,
    )(x, *consts, *per_batch, *per_time)


def _lru_prompt_kernel(x_ref, w_ref, cw_ref, cb_ref, wa_ref, ba_ref, wi_ref, bi_ref, lam_ref,
                       y_ref, h_ref, conv_ref, buf_ref, a_ref, b_ref, hcar_ref):
    n = x_ref.shape[1]

    @pl.when(pl.program_id(1) == 0)
    def _():
        buf_ref[0:CARRY_ROWS, :] = jnp.zeros((CARRY_ROWS, W_BR), f32)
        hcar_ref[...] = jnp.zeros_like(hcar_ref)

    xa = _mm(x_ref[0], w_ref[...])
    xc = _causal_conv(buf_ref, xa, cw_ref[...]) + cb_ref[...]
    a, b = _lru_gates(xc, wa_ref, ba_ref[...], wi_ref, bi_ref[...], lam_ref[...])

    a = a.reshape(n // 8, 8, W_BR)
    b = b.reshape(n // 8, 8, W_BR)
    row8 = lax.broadcasted_iota(jnp.int32, (n // 8, 8, W_BR), 1)
    for d in (1, 2, 4):
        keep = row8 >= d
        a_prev = jnp.where(keep, pltpu.roll(a, d, 1), 1.0)
        b_prev = jnp.where(keep, pltpu.roll(b, d, 1), 0.0)
        b = b + a * b_prev
        a = a * a_prev
    a_ref[...] = a.reshape(n, W_BR)
    b_ref[...] = b.reshape(n, W_BR)

    def group(gi, h_prev):
        i = pl.multiple_of(gi * 8, 8)
        blk = b_ref[pl.ds(i, 8), :] + a_ref[pl.ds(i, 8), :] * h_prev
        b_ref[pl.ds(i, 8), :] = blk
        return jnp.broadcast_to(blk[7:8, :], (8, W_BR))

    h_last = lax.fori_loop(0, n // 8, group, hcar_ref[...])
    hcar_ref[...] = h_last
    y_ref[0] = b_ref[...].astype(y_ref.dtype)
    h_ref[0] = h_last[0:1, :]
    conv_ref[0] = _conv_state(buf_ref)


def _gdn_prompt_kernel(x_ref, wqkv_ref, wz_ref, wba_ref, cw_ref, alog_ref, dtb_ref, nw_ref,
                       y_ref, conv_ref, s_ref, buf_ref, st_ref):
    seqs, n_seq, C = x_ref.shape[0], x_ref.shape[1], GDN_CHUNK
    assert n_seq == N_HEADS * C
    n = seqs * n_seq

    @pl.when(pl.program_id(1) == 0)
    def _():
        buf_ref[:, 0:CARRY_ROWS, :] = jnp.zeros((seqs, CARRY_ROWS, 3 * W_BR), f32)
        st_ref[...] = jnp.zeros_like(st_ref)

    x_seq = [x_ref[s].astype(bf16) for s in range(seqs)]
    x = jnp.concatenate(x_seq, 0)
    qkv = [_mm(x_seq[s], wqkv_ref[...]) for s in range(seqs)]
    c = jnp.concatenate([_silu(_causal_conv(buf_ref.at[s], qkv[s], cw_ref[...])) for s in range(seqs)], 0)
    z = _mm(x, wz_ref[...])
    beta, g = _gdn_gates(_mm(x, wba_ref[...]), alog_ref[...], dtb_ref[...])

    incl_t = _as_bf16(_chunk_masks(n_seq, C)[0])
    G_seq = [_mask_mm_exact(incl_t, g[s * n_seq:(s + 1) * n_seq]) for s in range(seqs)]
    G = jnp.concatenate(G_seq, 0)
    GT = [a.T for a in G_seq]
    H, HD, W = N_HEADS, HEAD_DIM, W_BR
    nck = n // C
    nck_seq = n_seq // C
    heads = range(H)
    tile = lambda a, h: a[:, h * HD:(h + 1) * HD]

    q_all = jnp.concatenate([_l2norm(tile(c, h)) * HD ** -0.5 for h in heads], 1)
    k_all = jnp.concatenate([_l2norm(tile(c, H + h)) for h in heads], 1)
    v_all = c[:, 2 * W:]
    per_head = lambda a, lo: jnp.concatenate([jnp.broadcast_to(a[:, lo + h:lo + h + 1], (n, HD)) for h in heads], 1)
    beta_w = per_head(beta, 0)
    g_w = per_head(G, H)
    glast_w = jnp.concatenate([jnp.broadcast_to(g_w[(ck + 1) * C - 1:(ck + 1) * C, :], (C, W)) for ck in range(nck)], 0)
    eg_w = jnp.exp(g_w)
    kb_all = k_all * beta_w
    vb_all = v_all * beta_w
    kbe_all = kb_all * eg_w
    qe_all = q_all * eg_w
    kend_all = k_all * jnp.exp(glast_w - g_w)
    eglast_w = jnp.exp(glast_w)

    lane_blk = lax.broadcasted_iota(jnp.int32, (n, H * C), 1) >> int(math.log2(C))
    gcol_c = G[:, H + H - 1:H + H]
    for h in reversed(range(H - 1)):
        gcol_c = jnp.where(lane_blk == h, G[:, H + h:H + h + 1], gcol_c)
    ri = lax.broadcasted_iota(jnp.int32, (C, H * C), 0)
    lj = lax.broadcasted_iota(jnp.int32, (C, H * C), 1)
    incl, strict = (lj & (C - 1)) <= ri, (lj & (C - 1)) < ri
    row_blk = lax.broadcasted_iota(jnp.int32, (1, H * C), 1) >> int(math.log2(C))
    bd_mask = (lax.broadcasted_iota(jnp.int32, (H * C, H * C), 0) >> int(math.log2(C))) == (
        lax.broadcasted_iota(jnp.int32, (H * C, H * C), 1) >> int(math.log2(C)))

    def block_diag(parts, width):
        zero = jnp.zeros((parts[0].shape[0], width), parts[0].dtype)
        return jnp.concatenate([jnp.concatenate([parts[h] if h == hp else zero for h in heads], 1) for hp in heads], 0)

    def block_diag_small(m_b):
        return jnp.where(bd_mask, jnp.concatenate([m_b] * H, 0), jnp.zeros((), bf16))

    neg_a, a_qk, rhs_w, rhs_bd = [], [], [], []
    for ck in range(nck):
        sl = slice(ck * C, (ck + 1) * C)
        gt, cs = GT[ck // nck_seq], ck % nck_seq
        g_row = None
        for h in reversed(heads):
            r = gt[H + h:H + h + 1, :]
            r = pltpu.roll(r, ((h - cs) * C) % n_seq, 1) if (h - cs) % nck_seq else r
            g_row = r if g_row is None else jnp.where(row_blk == h, r, g_row)
        decay = jnp.where(incl, jnp.exp(gcol_c[sl] - g_row), 0.0)
        k_b = k_all[sl].astype(bf16)
        k_bd = block_diag([tile(k_b, h) for h in heads], HD)
        neg_a.append(jnp.where(strict, -(_mm_nt(kb_all[sl], k_bd) * decay), 0.0))
        a_qk.append(_mm_nt(q_all[sl], k_bd) * decay)
        parts = [jnp.concatenate([tile(vb_all[sl], h), tile(kbe_all[sl], h)], 1) for h in heads]
        rhs_w.append(jnp.concatenate(parts, 1))
        rhs_bd.append(block_diag([p.astype(bf16) for p in parts], 2 * HD))

    dot = functools.partial(jnp.dot, preferred_element_type=f32)
    m = list(neg_a)
    t_off = list(neg_a)
    m_bd = [block_diag_small(a.astype(bf16)) for a in m]
    for _ in range(int(math.log2(C)) - 1):
        m = [dot(m[ck].astype(bf16), m_bd[ck]) for ck in range(nck)]
        m_bd = [block_diag_small(a.astype(bf16)) for a in m]
        t_off = [t_off[ck] + m[ck] + dot(t_off[ck].astype(bf16), m_bd[ck]) for ck in range(nck)]
    sol = [rhs_w[ck] + dot(t_off[ck].astype(bf16), rhs_bd[ck]) for ck in range(nck)]

    S = [st_ref[s] for s in range(seqs)]
    zero_s = jnp.zeros((HD, HD), bf16)
    o_chunks = [None] * nck
    for cs in range(nck_seq):
        for s in range(seqs):
            ck = s * nck_seq + cs
            sl = slice(ck * C, (ck + 1) * C)
            S_b = S[s].astype(bf16)
            s_pairs = [jnp.concatenate([jnp.concatenate([tile(S_b, p), zero_s], 1),
                                        jnp.concatenate([zero_s, tile(S_b, p + 1)], 1)], 0) for p in (0, 2)]
            u = [sol[ck][:, 2 * h * HD:(2 * h + 1) * HD] for h in heads]
            w = [sol[ck][:, (2 * h + 1) * HD:(2 * h + 2) * HD].astype(bf16) for h in heads]
            w_s = jnp.concatenate([dot(jnp.concatenate([w[p], w[p + 1]], 1), s_pairs[p // 2]) for p in (0, 2)], 1)
            v_new = jnp.concatenate(u, 1) - w_s
            v_b = v_new.astype(bf16)
            v_bd = block_diag([tile(v_b, h) for h in heads], HD)
            qe_b = qe_all[sl].astype(bf16)
            q_s = jnp.concatenate([dot(qe_b[:, p * HD:(p + 2) * HD], s_pairs[p // 2]) for p in (0, 2)], 1)
            o_chunks[ck] = q_s + dot(a_qk[ck].astype(bf16), v_bd)
            kend_stack = jnp.concatenate([tile(kend_all[sl], h) for h in heads], 0)
            S[s] = eglast_w[ck * C:ck * C + 1, :] * S[s] + _mm_tn(kend_stack, v_bd)
    o_all = jnp.concatenate(o_chunks, 0)
    y = _rmsnorm_heads([tile(o_all, h) for h in heads], nw_ref[...]) * _silu(z)
    for s in range(seqs):
        st_ref[s] = S[s]
        for h in heads:
            s_ref[s, h] = tile(S[s], h)
        y_ref[s] = y[s * n_seq:(s + 1) * n_seq].astype(y_ref.dtype)
        conv_ref[s] = _conv_state(buf_ref.at[s])


def _hgrn_prompt_kernel(layer, x_ref, w_ref, lbraw_ref, nw_ref, y_ref, s_ref, st_ref):
    n, C = x_ref.shape[1], HGRN_CHUNK

    @pl.when(pl.program_id(1) == 0)
    def _():
        st_ref[...] = jnp.zeros_like(st_ref)

    seqs = x_ref.shape[0]
    rows, nck = seqs * n, n // C
    u = _mm(jnp.concatenate([x_ref[s] for s in range(seqs)], 0), w_ref[...])
    hq, fz, hi, hg = (u[:, i * W_BR:(i + 1) * W_BR] for i in range(4))
    lb = _hgrn_lower_bound(lbraw_ref[...], layer)
    log_f, k = _hgrn_gates(fz, lb)
    q = _silu(hq)

    incl, _ = _chunk_masks(n, C)
    incl_b = _as_bf16(incl)
    G = jnp.concatenate([_mask_mm_exact(incl_b, log_f[s * n:(s + 1) * n]) for s in range(seqs)], 0)
    g_last = [G[(ck + 1) * C - 1:(ck + 1) * C, :] for ck in range(rows // C)]
    Gtot = jnp.concatenate([jnp.broadcast_to(r, (C, W_BR)) for r in g_last], 0)
    q_g = q * jnp.exp(G)
    k_g = k * jnp.exp(-G)
    k_end = k * jnp.exp(Gtot - G)
    dT = jnp.exp(jnp.concatenate(g_last, 0)).T

    heads = range(N_HEADS)
    hsl = [slice(h * HEAD_DIM, (h + 1) * HEAD_DIM) for h in heads]
    csl = [slice(ck * C, (ck + 1) * C) for ck in range(rows // C)]
    ssl = [slice(s * n, (s + 1) * n) for s in range(seqs)]
    lanes = [(s, h) for s in range(seqs) for h in heads]
    a_qk = {(s, h): jnp.where(incl, _mm_nt(q_g[ssl[s], hsl[h]], k_g[ssl[s], hsl[h]]), 0.0) for s, h in lanes}
    o_intra = {(s, h): _mm(a_qk[s, h], hi[ssl[s], hsl[h]]) for s, h in lanes}
    kv_of = lambda s, h, cs: _mm_tn(k_end[csl[s * nck + cs], hsl[h]], hi[csl[s * nck + cs], hsl[h]])
    kv = {}
    for cs in range(min(HGRN_LOOKAHEAD, nck)):
        for s, h in lanes:
            kv[s, h, cs] = kv_of(s, h, cs)
    S = {(s, h): st_ref[s, h] for s, h in lanes}
    o_inter = {}
    for cs in range(nck):
        if cs + HGRN_LOOKAHEAD < nck:
            for s, h in lanes:
                kv[s, h, cs + HGRN_LOOKAHEAD] = kv_of(s, h, cs + HGRN_LOOKAHEAD)
        for s, h in lanes:
            o_inter[s, h, cs] = _mm(q_g[csl[s * nck + cs], hsl[h]], S[s, h])
        for s, h in lanes:
            S[s, h] = dT[hsl[h], s * nck + cs:s * nck + cs + 1] * S[s, h] + kv.pop((s, h, cs))
    for s, h in lanes:
        st_ref[s, h] = S[s, h]
        s_ref[s, h] = S[s, h]
    o_heads = [jnp.concatenate([o_intra[s, h] + jnp.concatenate([o_inter[s, h, cs] for cs in range(nck)], 0)
                                for s in range(seqs)], 0) for h in heads]

    y = _rmsnorm_heads(o_heads, nw_ref[...]) * _sigmoid(hg)
    for s in range(seqs):
        y_ref[s] = y[ssl[s]].astype(y_ref.dtype)


def _ret_prompt_kernel(x_ref, w_ref, gw_ref, gb_ref, cos_ref, sin_ref, y_ref, s_ref, st_ref):
    n = x_ref.shape[1]

    @pl.when(pl.program_id(1) == 0)
    def _():
        st_ref[...] = jnp.zeros_like(st_ref)

    cos, sin = cos_ref[...], sin_ref[...]
    ri = lax.broadcasted_iota(jnp.int32, (n, n), 0)
    rj = lax.broadcasted_iota(jnp.int32, (n, n), 1)
    dist = (ri - rj).astype(f32)
    pos = lax.broadcasted_iota(jnp.int32, (n, 1), 0).astype(f32)
    heads = range(N_HEADS)
    lg = [_ret_log_gamma(h) for h in heads]

    for s in range(x_ref.shape[0]):
        xb = x_ref[s].astype(bf16)
        proj = [[_mm(xb, w_ref[:, i * W_BR + 2 * p * HEAD_DIM:i * W_BR + (2 * p + 2) * HEAD_DIM]) for i in range(4)]
                for p in range(N_HEADS // 2)]
        piece = lambda i, h: proj[h // 2][i][:, (h % 2) * HEAD_DIM:(h % 2 + 1) * HEAD_DIM]
        q = [_rotary(piece(0, h), cos, sin) for h in heads]
        k = [_rotary(piece(1, h), cos, sin) * HEAD_DIM ** -0.5 for h in heads]
        v = [piece(2, h) for h in heads]
        a_qk = [_mm_nt(q[h], k[h]) * jnp.where(rj <= ri, jnp.exp(dist * lg[h]), 0.0) for h in heads]
        S = [st_ref[s, h] for h in heads]
        o = [_mm(a_qk[h], v[h]) + _mm(q[h] * jnp.exp((pos + 1.0) * lg[h]), S[h]) for h in heads]
        S = [math.exp(n * lg[h]) * S[h] + _mm_tn(k[h] * jnp.exp((n - 1.0 - pos) * lg[h]), v[h]) for h in heads]
        for h in heads:
            st_ref[s, h] = S[h]
            s_ref[s, h] = S[h]
        rg = jnp.concatenate([piece(3, h) for h in heads], 1)
        y = _groupnorm_heads(o, gw_ref[...], gb_ref[...]) * _silu(rg)
        y_ref[s] = y.astype(y_ref.dtype)


def _xattn_prompt_kernel(x_ref, w_ref, mk_ref, mv_ref, y_ref):
    hsl = [slice(h * HEAD_DIM, (h + 1) * HEAD_DIM) for h in range(N_HEADS)]
    for b in range(x_ref.shape[0]):
        q = _mm(x_ref[b], w_ref[...])
        mk, mv = mk_ref[b], mv_ref[b]
        s = [_mm_nt(q[:, hs], mk[:, hs]) * HEAD_DIM ** -0.5 for hs in hsl]
        e = [jnp.exp(a - jnp.max(a, -1, keepdims=True)) for a in s]
        p = [a / jnp.sum(a, -1, keepdims=True) for a in e]
        outs = [_mm(a, mv[:, hs]) for a, hs in zip(p, hsl)]
        y_ref[b] = jnp.concatenate(outs, -1).astype(y_ref.dtype)


def _row_blocks(n):
    sub = min(ROW_SUBTILE, n)
    return [slice(r, r + sub) for r in range(0, n, sub)]


def _merge_kernel(alpha, x_ref, ya_ref, yb_ref, yc_ref, yd_ref, ye_ref, wg_ref, bg_ref, wb_ref, wo_ref,
                  g_ref, b_ref, o_ref):
    for rs in _row_blocks(x_ref.shape[0]):
        x = x_ref[rs, :]
        xb = x.astype(bf16)
        merged = None
        for i, y_ref in enumerate((ya_ref, yb_ref, yc_ref, yd_ref, ye_ref)):
            gate = _sigmoid(_mm(xb, wg_ref[i]) + bg_ref[i])
            term = gate * _mm(y_ref[rs, :], wb_ref[i])
            merged = term if merged is None else merged + term
        mix = _mm(merged, wo_ref[...])
        o_ref[rs, :] = _layernorm(alpha * x + mix, g_ref[...], b_ref[...])


def _ffn_kernel(alpha, n_split, x_ref, wg_ref, wv_ref, wd_ref, g_ref, b_ref, o_ref):
    d_ff = wg_ref.shape[1]
    step = d_ff // n_split
    for rs in _row_blocks(x_ref.shape[0]):
        x = x_ref[rs, :]
        xb = x.astype(bf16)
        ffn = None
        for i in range(n_split):
            cs = slice(i * step, (i + 1) * step)
            hidden = _silu(_mm(xb, wg_ref[:, cs])) * _mm(xb, wv_ref[:, cs])
            part = _mm(hidden, wd_ref[cs, :])
            ffn = part if ffn is None else ffn + part
        o_ref[rs, :] = _layernorm(alpha * x + ffn, g_ref[...], b_ref[...])


def _row_call(kernel, name, row_ops, consts, d_out):
    rows = row_ops[0].shape[0]
    tile = min(ROW_TILE, rows)
    in_specs = [pl.BlockSpec((tile, a.shape[1]), lambda i: (i, 0)) for a in row_ops]
    in_specs += [pl.BlockSpec(a.shape, lambda i, n=a.ndim: (0,) * n, pipeline_mode=pl.Buffered(1)) for a in consts]
    return pl.pallas_call(
        kernel,
        grid=(rows // tile,),
        in_specs=in_specs,
        out_specs=pl.BlockSpec((tile, d_out), lambda i: (i, 0)),
        out_shape=jax.ShapeDtypeStruct((rows, d_out), f32),
        compiler_params=pltpu.CompilerParams(dimension_semantics=("arbitrary",), vmem_limit_bytes=VMEM_LIMIT),
        name=name,
    )(*row_ops, *consts)


def _proj_kernel(x_ref, *refs):
    w_refs, o_ref = refs[:-1], refs[-1]
    x = x_ref[...].astype(bf16)
    off = 0
    for w_ref in w_refs:
        width = w_ref.shape[1]
        o_ref[:, off:off + width] = _mm(x, w_ref[...])
        off += width


def _sample_proj(x, weights):
    rows = x.shape[0]
    n = sum(w.shape[1] for w in weights)
    return pl.pallas_call(
        _proj_kernel,
        out_shape=jax.ShapeDtypeStruct((rows, n), f32),
        compiler_params=pltpu.CompilerParams(vmem_limit_bytes=VMEM_LIMIT),
        name="sample_proj",
    )(x, *weights)


def _columns(rows_list):
    nb = rows_list[0].shape[0]
    out = []
    for b in range(nb):
        tile = jnp.concatenate([r[b:b + 1, :] for r in rows_list]
                               + [jnp.zeros((8 - len(rows_list), HEAD_DIM), f32)], 0)
        out.append(tile.T)
    return out


N_SAMPLE_INPUTS = 24


def _rows_of(parts):
    nb = 1 + max(bi for bi, _ in parts)
    return [jnp.concatenate([parts[bi, h] for bi in range(nb)], 0) for h in range(N_HEADS)]


def _sample_kernel(layer, n_alias, *refs):
    (u_ref, h0_ref, lconv_ref, gconv_ref, sg_ref, sh_ref, sr_ref,
     lcw_ref, lcb_ref, lwa_ref, lba_ref, lwi_ref, lbi_ref, lam_ref,
     gcw_ref, alog_ref, dtb_ref, gnw_ref, lbraw_ref, hnw_ref, rgw_ref, rgb_ref, cos_ref, sin_ref) = refs[:N_SAMPLE_INPUTS]
    (ya_ref, yb_ref, yc_ref, yd_ref, h1_ref, lconv_o_ref, gconv_o_ref,
     sg_o_ref, sh_o_ref, sr_o_ref) = refs[N_SAMPLE_INPUTS + n_alias:]
    if n_alias == 0:
        for o_ref in (sg_o_ref, sh_o_ref, sr_o_ref):
            for other in range(o_ref.shape[0]):
                if other != layer:
                    o_ref[other] = jnp.zeros(o_ref.shape[1:], f32)
        sg_o_ref, sh_o_ref, sr_o_ref = (r.at[layer] for r in (sg_o_ref, sh_o_ref, sr_o_ref))
    nb = SAMPLE_BLOCK
    W = W_BR
    pairs = [(bi, h) for bi in range(nb) for h in range(N_HEADS)]
    u = u_ref[...]
    off = 0

    def take(width):
        nonlocal off
        part = u[:, off:off + width]
        off += width
        return part

    xa, qkv, z = take(W), take(3 * W), take(W)
    hq, fz, hi, hg = take(W), take(W), take(W), take(W)
    rq, rk, rv, rg = take(W), take(W), take(W), take(W)
    xq, ba = take(W), take(HEAD_DIM)

    lcw = lcw_ref[...]
    xc = lcw[CONV_W - 1:CONV_W, :] * xa + lcb_ref[...]
    for j in range(CONV_W - 1):
        xc = xc + lcw[j:j + 1, :] * lconv_ref[j]
    for j in range(CONV_W - 2):
        lconv_o_ref[j] = lconv_ref[j + 1]
    lconv_o_ref[CONV_W - 2] = xa
    a, b = _lru_gates(xc, lwa_ref, lba_ref[...], lwi_ref, lbi_ref[...], lam_ref[...])
    h1 = a * h0_ref[...] + b
    h1_ref[...] = h1
    ya_ref[...] = h1

    gcw = gcw_ref[...]
    c = gcw[CONV_W - 1:CONV_W, :] * qkv
    for j in range(CONV_W - 1):
        c = c + gcw[j:j + 1, :] * gconv_ref[j]
    for j in range(CONV_W - 2):
        gconv_o_ref[j] = gconv_ref[j + 1]
    gconv_o_ref[CONV_W - 2] = qkv
    c = _silu(c)
    beta, g = _gdn_gates(ba, alog_ref[...], dtb_ref[...])
    eg = jnp.exp(g)
    qn, kn, vv, kcol = [], [], [], {}
    for h in range(N_HEADS):
        qn.append(_l2norm(c[:, h * HEAD_DIM:(h + 1) * HEAD_DIM]) * HEAD_DIM ** -0.5)
        kn.append(_l2norm(c[:, W + h * HEAD_DIM:W + (h + 1) * HEAD_DIM]))
        vv.append(c[:, 2 * W + h * HEAD_DIM:2 * W + (h + 1) * HEAD_DIM])
        cols = _columns([kn[h]])
        for bi in range(nb):
            kcol[bi, h] = cols[bi][:, 0:1]
    kS = _rows_of({(bi, h): _mm(kn[h][bi:bi + 1, :], sg_ref[bi, h]) for bi, h in pairs})
    v_new = [beta[:, h:h + 1] * (vv[h] - eg[:, N_HEADS + h:N_HEADS + h + 1] * kS[h]) for h in range(N_HEADS)]
    S1 = {}
    for bi, h in pairs:
        S1[bi, h] = (eg[bi:bi + 1, N_HEADS + h:N_HEADS + h + 1] * sg_ref[bi, h]
                     + kcol[bi, h] * v_new[h][bi:bi + 1, :])
        sg_o_ref[bi, h] = S1[bi, h]
    o = _rows_of({(bi, h): _mm(qn[h][bi:bi + 1, :], S1[bi, h]) for bi, h in pairs})
    yb_ref[...] = _rmsnorm_heads(o, gnw_ref[...]) * _silu(z)

    lb = _hgrn_lower_bound(lbraw_ref[...], layer)
    log_f, kk = _hgrn_gates(fz, lb)
    f = jnp.exp(log_f)
    qs = _silu(hq)
    kcol, fcol = {}, {}
    for h in range(N_HEADS):
        hs = slice(h * HEAD_DIM, (h + 1) * HEAD_DIM)
        cols = _columns([kk[:, hs], f[:, hs]])
        for bi in range(nb):
            kcol[bi, h], fcol[bi, h] = cols[bi][:, 0:1], cols[bi][:, 1:2]
    S1 = {}
    for bi, h in pairs:
        hs = slice(h * HEAD_DIM, (h + 1) * HEAD_DIM)
        S1[bi, h] = fcol[bi, h] * sh_ref[bi, h] + kcol[bi, h] * hi[bi:bi + 1, hs]
        sh_o_ref[bi, h] = S1[bi, h]
    o = _rows_of({(bi, h): _mm(qs[bi:bi + 1, h * HEAD_DIM:(h + 1) * HEAD_DIM], S1[bi, h]) for bi, h in pairs})
    yc_ref[...] = _rmsnorm_heads(o, hnw_ref[...]) * _sigmoid(hg)

    cos, sin = cos_ref[...], sin_ref[...]
    qr, kcol = [], {}
    for h in range(N_HEADS):
        hs = slice(h * HEAD_DIM, (h + 1) * HEAD_DIM)
        qr.append(_rotary(rq[:, hs], cos, sin))
        cols = _columns([_rotary(rk[:, hs], cos, sin) * HEAD_DIM ** -0.5])
        for bi in range(nb):
            kcol[bi, h] = cols[bi][:, 0:1]
    S1 = {}
    for bi, h in pairs:
        hs = slice(h * HEAD_DIM, (h + 1) * HEAD_DIM)
        S1[bi, h] = math.exp(_ret_log_gamma(h)) * sr_ref[bi, h] + kcol[bi, h] * rv[bi:bi + 1, hs]
        sr_o_ref[bi, h] = S1[bi, h]
    o = _rows_of({(bi, h): _mm(qr[h][bi:bi + 1, :], S1[bi, h]) for bi, h in pairs})
    yd_ref[...] = _groupnorm_heads(o, rgw_ref[...], rgb_ref[...]) * _silu(rg)


def _sample_xattn_kernel(q_ref, ck_ref, cv_ref, o_ref):
    for bi in range(q_ref.shape[0]):
        q = q_ref[bi] * HEAD_DIM ** -0.5
        s = jnp.sum(ck_ref[bi] * q[None, :, :], axis=-1, keepdims=True)
        e = jnp.exp(s - jnp.max(s, axis=0, keepdims=True))
        o_ref[bi] = jnp.sum(e * cv_ref[bi], axis=0) / jnp.sum(e, axis=0)


def _sample_xattn(layer, q, ck, cv):
    nb = SAMPLE_BLOCK
    cache_spec = pl.BlockSpec((None, nb) + ck.shape[2:], lambda i: (layer, i, 0, 0, 0))
    q_spec = pl.BlockSpec((nb,) + q.shape[1:], lambda i: (i, 0, 0))
    return pl.pallas_call(
        _sample_xattn_kernel,
        grid=(q.shape[0] // nb,),
        in_specs=[q_spec, cache_spec, cache_spec],
        out_specs=q_spec,
        out_shape=jax.ShapeDtypeStruct(q.shape, f32),
        compiler_params=pltpu.CompilerParams(dimension_semantics=("arbitrary",), vmem_limit_bytes=VMEM_LIMIT),
        name="sample_xattn",
    )(q, ck, cv)


def _sample_branches(layer, u, h0, lconv_t, gconv_t, sg, sh, sr, params, prev_states):
    nb = SAMPLE_BLOCK
    rows = u.shape[0]
    blk2 = lambda a: pl.BlockSpec((nb, a.shape[1]), lambda i: (i, 0))
    blk_t = lambda a: pl.BlockSpec((a.shape[0], nb, a.shape[2]), lambda i: (0, i, 0))
    blk_layer = lambda a: pl.BlockSpec((None, nb) + a.shape[2:],
                                       lambda i, n=a.ndim: (layer, i) + (0,) * (n - 2))
    full = lambda a: pl.BlockSpec(a.shape, lambda i, n=a.ndim: (0,) * n)
    sds = lambda a: jax.ShapeDtypeStruct(a.shape, f32)
    y_shape = jax.ShapeDtypeStruct((rows, W_BR), f32)
    y_spec = pl.BlockSpec((nb, W_BR), lambda i: (i, 0))
    n_in = N_SAMPLE_INPUTS
    assert 7 + len(params) == n_in
    n_small_outs = 7
    blk_state_out = blk_layer if prev_states else (
        lambda a: pl.BlockSpec((a.shape[0], nb) + a.shape[2:], lambda i, n=a.ndim: (0, i) + (0,) * (n - 2)))
    return pl.pallas_call(
        functools.partial(_sample_kernel, layer, len(prev_states)),
        grid=(rows // nb,),
        in_specs=[blk2(u), blk2(h0), blk_t(lconv_t), blk_t(gconv_t), blk_layer(sg), blk_layer(sh), blk_layer(sr)]
        + [full(p) for p in params] + [pl.BlockSpec(memory_space=pl.ANY)] * len(prev_states),
        out_specs=[y_spec] * 4 + [blk2(h0), blk_t(lconv_t), blk_t(gconv_t),
                                  blk_state_out(sg), blk_state_out(sh), blk_state_out(sr)],
        out_shape=[y_shape] * 4 + [sds(h0), sds(lconv_t), sds(gconv_t), sds(sg), sds(sh), sds(sr)],
        input_output_aliases={n_in + j: n_small_outs + j for j in range(len(prev_states))},
        compiler_params=pltpu.CompilerParams(dimension_semantics=("arbitrary",), vmem_limit_bytes=VMEM_LIMIT),
        name="sample_branches",
    )(u, h0, lconv_t, gconv_t, sg, sh, sr, *params, *prev_states)


def kernel(x_prompt, x_sample, mem_prompt, state_lru_h, state_lru_conv, state_gdn_conv, state_gdn_s, state_hgrn_s, state_ret_s, cache_mem_k, cache_mem_v, w_in, lru_conv_w, lru_conv_b, lru_wa, lru_ba, lru_wi, lru_bi, lru_lambda, gdn_conv_w, gdn_a_log, gdn_dt_bias, gdn_norm_w, hgrn_lb_raw, hgrn_norm_w, ret_gn_w, ret_gn_b, w_mem_k, w_mem_v, w_merge_gate, b_merge_gate, w_branch, w_out, ln1_g, ln1_b, w_ffn_up, w_ffn_down, ln2_g, ln2_b):
    B, T, D = x_prompt.shape
    Bs = x_sample.shape[0]
    depth = w_in.shape[0]
    n_mem = mem_prompt.shape[1]
    W, H, HD = W_BR, N_HEADS, HEAD_DIM
    d_ff = w_ffn_down.shape[1]
    alpha = (2 * depth) ** 0.25
    assert Bs % SAMPLE_BLOCK == 0 and (B * T) % ROW_TILE == 0

    cos_all, sin_all = _rope_tables(list(range(T)) + [PAST_LEN])
    cos_p, sin_p = cos_all[:T], sin_all[:T]
    cos_s, sin_s = cos_all[T:T + 1], sin_all[T:T + 1]

    row = lambda a: a.reshape(1, -1)
    state_spec = ((H, HD, HD), f32)
    state_scratch = pltpu.VMEM((H, HD, HD), f32)

    xp = x_prompt
    xs = x_sample.reshape(Bs, D)
    mk_all, mv_all, mkb_all, mvb_all = _memkv(mem_prompt.reshape(B * n_mem, D), w_mem_k, w_mem_v)
    outs_p = [[] for _ in range(6)]
    outs_s = [[] for _ in range(3)]
    sample_states = ()

    for l in range(depth):
        wl = w_in[l]
        o = 0
        pieces = []
        for width in (W, 3 * W, W, H, H, W, W, W, W, W, W, W, W, W):
            pieces.append(wl[:, o:o + width])
            o += width
        (w_xa, w_qkv, w_z, w_gb, w_ga, w_hq, w_hf, w_hi, w_hg, w_rq, w_rk, w_rv, w_rg, w_xq) = pieces
        w_ba = jnp.concatenate([w_gb, w_ga, jnp.zeros((D, HD - 2 * H), f32)], 1).astype(bf16)
        w_xa, w_qkv, w_z, w_xq = (a.astype(bf16) for a in (w_xa, w_qkv, w_z, w_xq))
        w_hgrn = jnp.concatenate([w_hq, w_hf, w_hi, w_hg], 1).astype(bf16)
        w_ret = jnp.concatenate([w_rq, w_rk, w_rv, w_rg], 1).astype(bf16)

        lru_params = (lru_conv_w[l], row(lru_conv_b[l]), lru_wa[l].astype(bf16), row(lru_ba[l]),
                      lru_wi[l].astype(bf16), row(lru_bi[l]), row(lru_lambda[l]))
        pad_lanes = lambda a, start: jnp.zeros((1, HD), f32).at[0, start:start + H].set(a)
        alog_row, dtb_row = pad_lanes(gdn_a_log[l], H), pad_lanes(gdn_dt_bias[l], H)
        gdn_params = (gdn_conv_w[l], alog_row, dtb_row, row(gdn_norm_w[l]))
        hgrn_params = (hgrn_lb_raw, row(hgrn_norm_w[l]))
        ret_params = (row(ret_gn_w[l]), row(ret_gn_b[l]))
        merge_consts = (w_merge_gate[l].astype(bf16), b_merge_gate[l].reshape(N_BR, 1, D), w_branch[l].astype(bf16),
                        w_out[l].astype(bf16), row(ln1_g[l]), row(ln1_b[l]))
        ffn_consts = (w_ffn_up[l][:, :d_ff].astype(bf16), w_ffn_up[l][:, d_ff:].astype(bf16),
                      w_ffn_down[l].astype(bf16), row(ln2_g[l]), row(ln2_b[l]))
        merge_k = functools.partial(_merge_kernel, alpha)
        ffn_k = functools.partial(_ffn_kernel, alpha, 2)

        ya, h_last, lconv = _prompt_call(
            _lru_prompt_kernel, "lru_prompt", xp, T_TILE_WIDE, (w_xa,) + lru_params, (), (),
            [(W, bf16)], [((1, W), f32), ((CONV_W - 1, W), f32)],
            [pltpu.VMEM((CARRY_ROWS + T_TILE_WIDE, W), f32), pltpu.VMEM((T_TILE_WIDE, W), f32),
             pltpu.VMEM((T_TILE_WIDE, W), f32), pltpu.VMEM((8, W), f32)])
        yb, gconv, s_gdn = _prompt_call(
            _gdn_prompt_kernel, "gdn_prompt", xp, T_TILE, (w_qkv, w_z, w_ba) + gdn_params, (), (),
            [(W, bf16)], [((CONV_W - 1, 3 * W), f32), state_spec],
            [pltpu.VMEM((GDN_SEQS, CARRY_ROWS + T_TILE, 3 * W), f32), pltpu.VMEM((GDN_SEQS, HD, W), f32)],
            seqs=GDN_SEQS)
        yc, s_hgrn = _prompt_call(
            functools.partial(_hgrn_prompt_kernel, l), "hgrn_prompt", xp, T_TILE, (w_hgrn,) + hgrn_params, (), (),
            [(W, bf16)], [state_spec], [pltpu.VMEM((HGRN_SEQS, H, HD, HD), f32)], seqs=HGRN_SEQS)
        yd, s_ret = _prompt_call(
            _ret_prompt_kernel, "ret_prompt", xp, T_TILE, (w_ret,) + ret_params, (), (cos_p, sin_p),
            [(W, bf16)], [state_spec], [pltpu.VMEM((RET_SEQS, H, HD, HD), f32)], seqs=RET_SEQS)
        (ye,) = _prompt_call(
            _xattn_prompt_kernel, "xattn_prompt", xp, T_TILE_WIDE, (w_xq,),
            (mkb_all[l].reshape(B, n_mem, W), mvb_all[l].reshape(B, n_mem, W)), (), [(W, bf16)], [], [],
            seqs=XATTN_SEQS)
        x2d = xp.reshape(B * T, D)
        ys = [y.reshape(B * T, W) for y in (ya, yb, yc, yd, ye)]
        x2d = _row_call(merge_k, "merge_prompt", [x2d] + ys, merge_consts, D)
        x2d = _row_call(ffn_k, "ffn_prompt", [x2d], ffn_consts, D)
        xp = x2d.reshape(B, T, D)
        for lst, val in zip(outs_p, (h_last.reshape(B, W), lconv, gconv, s_gdn, s_hgrn, s_ret)):
            lst.append(val)

        u_s = _sample_proj(xs, (w_xa, w_qkv, w_z, w_hgrn, w_ret, w_xq, w_ba))
        sample_params = lru_params + gdn_params + hgrn_params + ret_params + (cos_s, sin_s)
        res = _sample_branches(
            l, u_s, state_lru_h[l], state_lru_conv[l].transpose(1, 0, 2), state_gdn_conv[l].transpose(1, 0, 2),
            state_gdn_s, state_hgrn_s, state_ret_s, sample_params, sample_states)
        ys_s, (h1, lconv_s, gconv_s), sample_states = res[:4], res[4:7], tuple(res[7:])
        xq_off = sum(w.shape[1] for w in (w_xa, w_qkv, w_z, w_hgrn, w_ret))
        q_s = u_s[:, xq_off:xq_off + W].reshape(Bs, H, HD)
        ye_s = _sample_xattn(l, q_s, cache_mem_k, cache_mem_v).reshape(Bs, W)
        xs = _row_call(merge_k, "merge_sample", [xs] + list(ys_s) + [ye_s], merge_consts, D)
        xs = _row_call(ffn_k, "ffn_sample", [xs], ffn_consts, D)
        for lst, val in zip(outs_s, (h1, lconv_s.transpose(1, 0, 2), gconv_s.transpose(1, 0, 2))):
            lst.append(val)

    mem_shape = (depth, B, n_mem, H, HD)
    return (xp, xs.reshape(Bs, 1, D), *(jnp.stack(v) for v in outs_p), mk_all.reshape(mem_shape),
            mv_all.reshape(mem_shape), *(jnp.stack(v) for v in outs_s), *sample_states)
```

```python
import functools
import math

import numpy as np
import jax
import jax.numpy as jnp
from jax import lax
from jax.experimental import pallas as pl
from jax.experimental.pallas import tpu as pltpu

f32 = jnp.float32
bf16 = jnp.bfloat16

N_HEADS = 4
HEAD_DIM = 128
W_BR = N_HEADS * HEAD_DIM
CONV_W = 4
N_BR = 5
LRU_C = 8.0
GDN_CHUNK = 64
HGRN_CHUNK = 16
ROPE_BASE = 10000.0
LN_EPS = 1e-5
NORM_EPS = 1e-6
PAST_LEN = 16384

T_TILE = 256
T_TILE_WIDE = 512
GDN_SEQS = 2
RET_SEQS = 2
XATTN_SEQS = 2
HGRN_SEQS = 2
HGRN_LOOKAHEAD = 2
ROW_TILE = 1024
ROW_SUBTILE = 256
SAMPLE_BLOCK = 8
CARRY_ROWS = 8
VMEM_LIMIT = 52 * 1024 * 1024


def _mm(a, b):
    return jnp.dot(a.astype(bf16), b.astype(bf16), preferred_element_type=f32)


def _mm_nt(a, b):
    return lax.dot_general(a.astype(bf16), b.astype(bf16), (((1,), (1,)), ((), ())), preferred_element_type=f32)


def _mm_tn(a, b):
    return lax.dot_general(a.astype(bf16), b.astype(bf16), (((0,), (0,)), ((), ())), preferred_element_type=f32)


def _mask_mm_exact(mask_bf16, x):
    hi = x.astype(bf16)
    r1 = x - hi.astype(f32)
    mid = r1.astype(bf16)
    lo = (r1 - mid.astype(f32)).astype(bf16)
    dot = functools.partial(jnp.dot, preferred_element_type=f32)
    return dot(mask_bf16, hi) + dot(mask_bf16, mid) + dot(mask_bf16, lo)


def _chunk_masks(n, chunk):
    shift = int(math.log2(chunk))
    row = lax.broadcasted_iota(jnp.int32, (n, n), 0)
    col = lax.broadcasted_iota(jnp.int32, (n, n), 1)
    same = (row >> shift) == (col >> shift)
    return jnp.logical_and(same, col <= row), same


def _as_bf16(mask):
    return jnp.where(mask, 1.0, 0.0).astype(bf16)


def _sigmoid(x):
    return jax.nn.sigmoid(x)


def _silu(x):
    return x * jax.nn.sigmoid(x)


def _layernorm(x, g, b):
    mu = jnp.mean(x, -1, keepdims=True)
    xc = x - mu
    var = jnp.mean(xc * xc, -1, keepdims=True)
    return xc * lax.rsqrt(var + LN_EPS) * g + b


def _head_slices(a):
    return [a[:, h * HEAD_DIM:(h + 1) * HEAD_DIM] for h in range(N_HEADS)]


def _causal_conv(buf_ref, x, cw):
    n = x.shape[0]
    buf_ref[CARRY_ROWS:CARRY_ROWS + n, :] = x
    out = cw[CONV_W - 1:CONV_W, :] * x
    for j in range(CONV_W - 1):
        off = CARRY_ROWS - (CONV_W - 1) + j
        out = out + cw[j:j + 1, :] * buf_ref[off:off + n, :]
    buf_ref[0:CARRY_ROWS, :] = buf_ref[n:n + CARRY_ROWS, :]
    return out


def _conv_state(buf_ref):
    return buf_ref[CARRY_ROWS - (CONV_W - 1):CARRY_ROWS, :]


def _lru_gates(xc, wa_ref, ba, wi_ref, bi, lam):
    xh = _head_slices(xc)
    r = _sigmoid(jnp.concatenate([_mm(xh[h], wa_ref[h]) for h in range(N_HEADS)], -1) + ba)
    ig = _sigmoid(jnp.concatenate([_mm(xh[h], wi_ref[h]) for h in range(N_HEADS)], -1) + bi)
    log_a = -LRU_C * r * jax.nn.softplus(-lam)
    a = jnp.exp(log_a)
    b = jnp.sqrt(jnp.tanh(-log_a) * (1.0 + a * a)) * (ig * xc)
    return a, b


def _gdn_gates(ba, alog_row, dtb_row):
    beta = _sigmoid(ba)
    g = -jnp.exp(alog_row) * jax.nn.softplus(ba + dtb_row)
    return beta, g


def _l2norm(a):
    return a * lax.rsqrt(jnp.sum(a * a, -1, keepdims=True) + NORM_EPS)


def _rmsnorm_heads(o_heads, w):
    parts = [o * lax.rsqrt(jnp.mean(o * o, -1, keepdims=True) + NORM_EPS) for o in o_heads]
    return jnp.concatenate(parts, -1) * w


def _groupnorm_heads(o_heads, w, b):
    parts = []
    for o in o_heads:
        mu = jnp.mean(o, -1, keepdims=True)
        oc = o - mu
        var = jnp.mean(oc * oc, -1, keepdims=True)
        parts.append(oc * lax.rsqrt(var + NORM_EPS))
    return jnp.concatenate(parts, -1) * w + b


def _hgrn_lower_bound(lbraw, layer):
    n = lbraw.shape[0]
    rows = [lbraw[i:i + 1, :] for i in range(n)]
    m = rows[0]
    for r in rows[1:]:
        m = jnp.maximum(m, r)
    es = [jnp.exp(r - m) for r in rows]
    tot = es[0]
    for e in es[1:]:
        tot = tot + e
    acc = jnp.zeros_like(m)
    for i in range(1, layer + 1):
        acc = acc + es[i] / tot
    return acc


def _hgrn_gates(fz, lb):
    log_f = jnp.logaddexp(jnp.log(lb), jnp.log1p(-lb) + jax.nn.log_sigmoid(fz))
    k = (1.0 - lb) * _sigmoid(-fz)
    return log_f, k


def _rotary(x, cos, sin_signed):
    return x * cos + pltpu.roll(x, HEAD_DIM // 2, 1) * sin_signed


def _ret_log_gamma(h):
    return math.log1p(-(2.0 ** (-5.0 - h)))


def _rope_kernel(ang_ref, cos_ref, sin_ref):
    ang = ang_ref[...]
    lane = lax.broadcasted_iota(jnp.int32, ang.shape, 1)
    s = jnp.sin(ang)
    cos_ref[...] = jnp.cos(ang)
    sin_ref[...] = jnp.where(lane < HEAD_DIM // 2, -s, s)


def _rope_tables(positions):
    half = HEAD_DIM // 2
    inv = ROPE_BASE ** (-np.arange(half, dtype=np.float64) / half)
    ang = np.mod(np.asarray(positions, np.float64)[:, None] * inv[None, :], 2.0 * np.pi)
    ang = np.concatenate([ang, ang], axis=1).astype(np.float32)
    n = ang.shape[0]
    pad = (-n) % 8
    ang = np.pad(ang, ((0, pad), (0, 0)))
    shp = jax.ShapeDtypeStruct(ang.shape, f32)
    cos, sin = pl.pallas_call(_rope_kernel, out_shape=(shp, shp), name="rope_tables")(jnp.asarray(ang))
    return cos[:n], sin[:n]


def _memkv_kernel(m_ref, wk_ref, wv_ref, k_ref, v_ref, kb_ref, vb_ref):
    m = m_ref[...]
    k = _mm(m, wk_ref[...])
    v = _mm(m, wv_ref[...])
    k_ref[...] = k
    v_ref[...] = v
    kb_ref[...] = k.astype(bf16)
    vb_ref[...] = v.astype(bf16)


def _memkv(mem2d, wk, wv):
    rows, d = mem2d.shape
    depth = wk.shape[0]
    tile = 256
    w_spec = pl.BlockSpec((None, d, W_BR), lambda l, i: (l, 0, 0))
    o_spec = pl.BlockSpec((None, tile, W_BR), lambda l, i: (l, i, 0))
    return pl.pallas_call(
        _memkv_kernel,
        grid=(depth, rows // tile),
        in_specs=[pl.BlockSpec((tile, d), lambda l, i: (i, 0)), w_spec, w_spec],
        out_specs=[o_spec] * 4,
        out_shape=[jax.ShapeDtypeStruct((depth, rows, W_BR), f32)] * 2
        + [jax.ShapeDtypeStruct((depth, rows, W_BR), bf16)] * 2,
        compiler_params=pltpu.CompilerParams(dimension_semantics=("arbitrary", "arbitrary"),
                                             vmem_limit_bytes=VMEM_LIMIT),
        name="mem_kv",
    )(mem2d, wk, wv)


class _OfLayer:
    def __init__(self, array, layer):
        self.array, self.layer = array, layer


def _const_specs(consts):
    specs, operands = [], []
    for a in consts:
        if isinstance(a, _OfLayer):
            arr, layer = a.array, a.layer
            specs.append(pl.BlockSpec((None,) + arr.shape[1:], lambda *_, n=arr.ndim, l=layer: (l,) + (0,) * (n - 1)))
        else:
            arr = a
            specs.append(pl.BlockSpec(arr.shape, lambda *_, n=arr.ndim: (0,) * n))
        operands.append(arr)
    return specs, operands


def _prompt_call(kernel, name, x, tile, consts, per_batch, per_time, outs_tile, outs_batch, scratch, seqs=1):
    B, T, D = x.shape
    assert T % tile == 0 and B % seqs == 0
    nt = T // tile
    in_specs = [pl.BlockSpec((seqs, tile, D), lambda b, t: (b, t, 0))]
    const_specs, consts = _const_specs(consts)
    in_specs += const_specs
    for a in per_batch:
        in_specs.append(pl.BlockSpec((seqs,) + a.shape[1:], lambda b, t, n=a.ndim: (b,) + (0,) * (n - 1)))
    for a in per_time:
        in_specs.append(pl.BlockSpec((tile,) + a.shape[1:], lambda b, t, n=a.ndim: (t,) + (0,) * (n - 1)))
    out_specs, out_shape = [], []
    for width, dt in outs_tile:
        out_specs.append(pl.BlockSpec((seqs, tile, width), lambda b, t: (b, t, 0)))
        out_shape.append(jax.ShapeDtypeStruct((B, T, width), dt))
    for shp, dt in outs_batch:
        out_specs.append(pl.BlockSpec((seqs,) + shp, lambda b, t, n=len(shp): (b,) + (0,) * n))
        out_shape.append(jax.ShapeDtypeStruct((B,) + shp, dt))
    return pl.pallas_call(
        kernel,
        grid=(B // seqs, nt),
        in_specs=in_specs,
        out_specs=out_specs,
        out_shape=out_shape,
        scratch_shapes=scratch,
        compiler_params=pltpu.CompilerParams(dimension_semantics=("arbitrary", "arbitrary"),
                                             vmem_limit_bytes=VMEM_LIMIT),
        name=name,
    )(x, *consts, *per_batch, *per_time)


def _lru_prompt_kernel(x_ref, w_ref, cw_ref, cb_ref, wa_ref, ba_ref, wi_ref, bi_ref, lam_ref,
                       y_ref, h_ref, conv_ref, buf_ref, a_ref, b_ref, hcar_ref):
    n = x_ref.shape[1]

    @pl.when(pl.program_id(1) == 0)
    def _():
        buf_ref[0:CARRY_ROWS, :] = jnp.zeros((CARRY_ROWS, W_BR), f32)
        hcar_ref[...] = jnp.zeros_like(hcar_ref)

    xa = _mm(x_ref[0], w_ref[...])
    xc = _causal_conv(buf_ref, xa, cw_ref[...]) + cb_ref[...]
    a, b = _lru_gates(xc, wa_ref, ba_ref[...], wi_ref, bi_ref[...], lam_ref[...])

    a = a.reshape(n // 8, 8, W_BR)
    b = b.reshape(n // 8, 8, W_BR)
    row8 = lax.broadcasted_iota(jnp.int32, (n // 8, 8, W_BR), 1)
    for d in (1, 2, 4):
        keep = row8 >= d
        a_prev = jnp.where(keep, pltpu.roll(a, d, 1), 1.0)
        b_prev = jnp.where(keep, pltpu.roll(b, d, 1), 0.0)
        b = b + a * b_prev
        a = a * a_prev
    a_ref[...] = a.reshape(n, W_BR)
    b_ref[...] = b.reshape(n, W_BR)

    def group(gi, h_prev):
        i = pl.multiple_of(gi * 8, 8)
        blk = b_ref[pl.ds(i, 8), :] + a_ref[pl.ds(i, 8), :] * h_prev
        b_ref[pl.ds(i, 8), :] = blk
        return jnp.broadcast_to(blk[7:8, :], (8, W_BR))

    h_last = lax.fori_loop(0, n // 8, group, hcar_ref[...])
    hcar_ref[...] = h_last
    y_ref[0] = b_ref[...].astype(y_ref.dtype)
    h_ref[0] = h_last[0:1, :]
    conv_ref[0] = _conv_state(buf_ref)


def _gdn_prompt_kernel(x_ref, wqkv_ref, wz_ref, wba_ref, cw_ref, alog_ref, dtb_ref, nw_ref,
                       y_ref, conv_ref, s_ref, buf_ref, st_ref):
    seqs, n_seq, C = x_ref.shape[0], x_ref.shape[1], GDN_CHUNK
    assert n_seq == N_HEADS * C
    n = seqs * n_seq

    @pl.when(pl.program_id(1) == 0)
    def _():
        buf_ref[:, 0:CARRY_ROWS, :] = jnp.zeros((seqs, CARRY_ROWS, 3 * W_BR), f32)
        st_ref[...] = jnp.zeros_like(st_ref)

    x_seq = [x_ref[s].astype(bf16) for s in range(seqs)]
    x = jnp.concatenate(x_seq, 0)
    qkv = [_mm(x_seq[s], wqkv_ref[...]) for s in range(seqs)]
    c = jnp.concatenate([_silu(_causal_conv(buf_ref.at[s], qkv[s], cw_ref[...])) for s in range(seqs)], 0)
    z = _mm(x, wz_ref[...])
    beta, g = _gdn_gates(_mm(x, wba_ref[...]), alog_ref[...], dtb_ref[...])

    incl_t = _as_bf16(_chunk_masks(n_seq, C)[0])
    G_seq = [_mask_mm_exact(incl_t, g[s * n_seq:(s + 1) * n_seq]) for s in range(seqs)]
    G = jnp.concatenate(G_seq, 0)
    GT = [a.T for a in G_seq]
    H, HD, W = N_HEADS, HEAD_DIM, W_BR
    nck = n // C
    nck_seq = n_seq // C
    heads = range(H)
    tile = lambda a, h: a[:, h * HD:(h + 1) * HD]

    q_all = jnp.concatenate([_l2norm(tile(c, h)) * HD ** -0.5 for h in heads], 1)
    k_all = jnp.concatenate([_l2norm(tile(c, H + h)) for h in heads], 1)
    v_all = c[:, 2 * W:]
    per_head = lambda a, lo: jnp.concatenate([jnp.broadcast_to(a[:, lo + h:lo + h + 1], (n, HD)) for h in heads], 1)
    beta_w = per_head(beta, 0)
    g_w = per_head(G, H)
    glast_w = jnp.concatenate([jnp.broadcast_to(g_w[(ck + 1) * C - 1:(ck + 1) * C, :], (C, W)) for ck in range(nck)], 0)
    eg_w = jnp.exp(g_w)
    kb_all = k_all * beta_w
    vb_all = v_all * beta_w
    kbe_all = kb_all * eg_w
    qe_all = q_all * eg_w
    kend_all = k_all * jnp.exp(glast_w - g_w)
    eglast_w = jnp.exp(glast_w)

    lane_blk = lax.broadcasted_iota(jnp.int32, (n, H * C), 1) >> int(math.log2(C))
    gcol_c = G[:, H + H - 1:H + H]
    for h in reversed(range(H - 1)):
        gcol_c = jnp.where(lane_blk == h, G[:, H + h:H + h + 1], gcol_c)
    ri = lax.broadcasted_iota(jnp.int32, (C, H * C), 0)
    lj = lax.broadcasted_iota(jnp.int32, (C, H * C), 1)
    incl, strict = (lj & (C - 1)) <= ri, (lj & (C - 1)) < ri
    row_blk = lax.broadcasted_iota(jnp.int32, (1, H * C), 1) >> int(math.log2(C))
    bd_mask = (lax.broadcasted_iota(jnp.int32, (H * C, H * C), 0) >> int(math.log2(C))) == (
        lax.broadcasted_iota(jnp.int32, (H * C, H * C), 1) >> int(math.log2(C)))

    def block_diag(parts, width):
        zero = jnp.zeros((parts[0].shape[0], width), parts[0].dtype)
        return jnp.concatenate([jnp.concatenate([parts[h] if h == hp else zero for h in heads], 1) for hp in heads], 0)

    def block_diag_small(m_b):
        return jnp.where(bd_mask, jnp.concatenate([m_b] * H, 0), jnp.zeros((), bf16))

    neg_a, a_qk, rhs_w, rhs_bd = [], [], [], []
    for ck in range(nck):
        sl = slice(ck * C, (ck + 1) * C)
        gt, cs = GT[ck // nck_seq], ck % nck_seq
        g_row = None
        for h in reversed(heads):
            r = gt[H + h:H + h + 1, :]
            r = pltpu.roll(r, ((h - cs) * C) % n_seq, 1) if (h - cs) % nck_seq else r
            g_row = r if g_row is None else jnp.where(row_blk == h, r, g_row)
        decay = jnp.where(incl, jnp.exp(gcol_c[sl] - g_row), 0.0)
        k_b = k_all[sl].astype(bf16)
        k_bd = block_diag([tile(k_b, h) for h in heads], HD)
        neg_a.append(jnp.where(strict, -(_mm_nt(kb_all[sl], k_bd) * decay), 0.0))
        a_qk.append(_mm_nt(q_all[sl], k_bd) * decay)
        parts = [jnp.concatenate([tile(vb_all[sl], h), tile(kbe_all[sl], h)], 1) for h in heads]
        rhs_w.append(jnp.concatenate(parts, 1))
        rhs_bd.append(block_diag([p.astype(bf16) for p in parts], 2 * HD))

    dot = functools.partial(jnp.dot, preferred_element_type=f32)
    m = list(neg_a)
    t_off = list(neg_a)
    m_bd = [block_diag_small(a.astype(bf16)) for a in m]
    for _ in range(int(math.log2(C)) - 1):
        m = [dot(m[ck].astype(bf16), m_bd[ck]) for ck in range(nck)]
        m_bd = [block_diag_small(a.astype(bf16)) for a in m]
        t_off = [t_off[ck] + m[ck] + dot(t_off[ck].astype(bf16), m_bd[ck]) for ck in range(nck)]
    sol = [rhs_w[ck] + dot(t_off[ck].astype(bf16), rhs_bd[ck]) for ck in range(nck)]

    S = [st_ref[s] for s in range(seqs)]
    zero_s = jnp.zeros((HD, HD), bf16)
    o_chunks = [None] * nck
    for cs in range(nck_seq):
        for s in range(seqs):
            ck = s * nck_seq + cs
            sl = slice(ck * C, (ck + 1) * C)
            S_b = S[s].astype(bf16)
            s_pairs = [jnp.concatenate([jnp.concatenate([tile(S_b, p), zero_s], 1),
                                        jnp.concatenate([zero_s, tile(S_b, p + 1)], 1)], 0) for p in (0, 2)]
            u = [sol[ck][:, 2 * h * HD:(2 * h + 1) * HD] for h in heads]
            w = [sol[ck][:, (2 * h + 1) * HD:(2 * h + 2) * HD].astype(bf16) for h in heads]
            w_s = jnp.concatenate([dot(jnp.concatenate([w[p], w[p + 1]], 1), s_pairs[p // 2]) for p in (0, 2)], 1)
            v_new = jnp.concatenate(u, 1) - w_s
            v_b = v_new.astype(bf16)
            v_bd = block_diag([tile(v_b, h) for h in heads], HD)
            qe_b = qe_all[sl].astype(bf16)
            q_s = jnp.concatenate([dot(qe_b[:, p * HD:(p + 2) * HD], s_pairs[p // 2]) for p in (0, 2)], 1)
            o_chunks[ck] = q_s + dot(a_qk[ck].astype(bf16), v_bd)
            kend_stack = jnp.concatenate([tile(kend_all[sl], h) for h in heads], 0)
            S[s] = eglast_w[ck * C:ck * C + 1, :] * S[s] + _mm_tn(kend_stack, v_bd)
    o_all = jnp.concatenate(o_chunks, 0)
    y = _rmsnorm_heads([tile(o_all, h) for h in heads], nw_ref[...]) * _silu(z)
    for s in range(seqs):
        st_ref[s] = S[s]
        for h in heads:
            s_ref[s, h] = tile(S[s], h)
        y_ref[s] = y[s * n_seq:(s + 1) * n_seq].astype(y_ref.dtype)
        conv_ref[s] = _conv_state(buf_ref.at[s])


def _hgrn_prompt_kernel(layer, x_ref, w_ref, lbraw_ref, nw_ref, y_ref, s_ref, st_ref):
    n, C = x_ref.shape[1], HGRN_CHUNK

    @pl.when(pl.program_id(1) == 0)
    def _():
        st_ref[...] = jnp.zeros_like(st_ref)

    seqs = x_ref.shape[0]
    rows, nck = seqs * n, n // C
    u = _mm(jnp.concatenate([x_ref[s] for s in range(seqs)], 0), w_ref[...])
    hq, fz, hi, hg = (u[:, i * W_BR:(i + 1) * W_BR] for i in range(4))
    lb = _hgrn_lower_bound(lbraw_ref[...], layer)
    log_f, k = _hgrn_gates(fz, lb)
    q = _silu(hq)

    incl, _ = _chunk_masks(n, C)
    incl_b = _as_bf16(incl)
    G = jnp.concatenate([_mask_mm_exact(incl_b, log_f[s * n:(s + 1) * n]) for s in range(seqs)], 0)
    g_last = [G[(ck + 1) * C - 1:(ck + 1) * C, :] for ck in range(rows // C)]
    Gtot = jnp.concatenate([jnp.broadcast_to(r, (C, W_BR)) for r in g_last], 0)
    q_g = q * jnp.exp(G)
    k_g = k * jnp.exp(-G)
    k_end = k * jnp.exp(Gtot - G)
    dT = jnp.exp(jnp.concatenate(g_last, 0)).T

    heads = range(N_HEADS)
    hsl = [slice(h * HEAD_DIM, (h + 1) * HEAD_DIM) for h in heads]
    csl = [slice(ck * C, (ck + 1) * C) for ck in range(rows // C)]
    ssl = [slice(s * n, (s + 1) * n) for s in range(seqs)]
    lanes = [(s, h) for s in range(seqs) for h in heads]
    a_qk = {(s, h): jnp.where(incl, _mm_nt(q_g[ssl[s], hsl[h]], k_g[ssl[s], hsl[h]]), 0.0) for s, h in lanes}
    o_intra = {(s, h): _mm(a_qk[s, h], hi[ssl[s], hsl[h]]) for s, h in lanes}
    kv_of = lambda s, h, cs: _mm_tn(k_end[csl[s * nck + cs], hsl[h]], hi[csl[s * nck + cs], hsl[h]])
    kv = {}
    for cs in range(min(HGRN_LOOKAHEAD, nck)):
        for s, h in lanes:
            kv[s, h, cs] = kv_of(s, h, cs)
    S = {(s, h): st_ref[s, h] for s, h in lanes}
    o_inter = {}
    for cs in range(nck):
        if cs + HGRN_LOOKAHEAD < nck:
            for s, h in lanes:
                kv[s, h, cs + HGRN_LOOKAHEAD] = kv_of(s, h, cs + HGRN_LOOKAHEAD)
        for s, h in lanes:
            o_inter[s, h, cs] = _mm(q_g[csl[s * nck + cs], hsl[h]], S[s, h])
        for s, h in lanes:
            S[s, h] = dT[hsl[h], s * nck + cs:s * nck + cs + 1] * S[s, h] + kv.pop((s, h, cs))
    for s, h in lanes:
        st_ref[s, h] = S[s, h]
        s_ref[s, h] = S[s, h]
    o_heads = [jnp.concatenate([o_intra[s, h] + jnp.concatenate([o_inter[s, h, cs] for cs in range(nck)], 0)
                                for s in range(seqs)], 0) for h in heads]

    y = _rmsnorm_heads(o_heads, nw_ref[...]) * _sigmoid(hg)
    for s in range(seqs):
        y_ref[s] = y[ssl[s]].astype(y_ref.dtype)


def _ret_prompt_kernel(x_ref, w_ref, gw_ref, gb_ref, cos_ref, sin_ref, y_ref, s_ref, st_ref):
    n = x_ref.shape[1]

    @pl.when(pl.program_id(1) == 0)
    def _():
        st_ref[...] = jnp.zeros_like(st_ref)

    cos, sin = cos_ref[...], sin_ref[...]
    ri = lax.broadcasted_iota(jnp.int32, (n, n), 0)
    rj = lax.broadcasted_iota(jnp.int32, (n, n), 1)
    dist = (ri - rj).astype(f32)
    pos = lax.broadcasted_iota(jnp.int32, (n, 1), 0).astype(f32)
    heads = range(N_HEADS)
    lg = [_ret_log_gamma(h) for h in heads]

    for s in range(x_ref.shape[0]):
        xb = x_ref[s].astype(bf16)
        proj = [[_mm(xb, w_ref[:, i * W_BR + 2 * p * HEAD_DIM:i * W_BR + (2 * p + 2) * HEAD_DIM]) for i in range(4)]
                for p in range(N_HEADS // 2)]
        piece = lambda i, h: proj[h // 2][i][:, (h % 2) * HEAD_DIM:(h % 2 + 1) * HEAD_DIM]
        q = [_rotary(piece(0, h), cos, sin) for h in heads]
        k = [_rotary(piece(1, h), cos, sin) * HEAD_DIM ** -0.5 for h in heads]
        v = [piece(2, h) for h in heads]
        a_qk = [_mm_nt(q[h], k[h]) * jnp.where(rj <= ri, jnp.exp(dist * lg[h]), 0.0) for h in heads]
        S = [st_ref[s, h] for h in heads]
        o = [_mm(a_qk[h], v[h]) + _mm(q[h] * jnp.exp((pos + 1.0) * lg[h]), S[h]) for h in heads]
        S = [math.exp(n * lg[h]) * S[h] + _mm_tn(k[h] * jnp.exp((n - 1.0 - pos) * lg[h]), v[h]) for h in heads]
        for h in heads:
            st_ref[s, h] = S[h]
            s_ref[s, h] = S[h]
        rg = jnp.concatenate([piece(3, h) for h in heads], 1)
        y = _groupnorm_heads(o, gw_ref[...], gb_ref[...]) * _silu(rg)
        y_ref[s] = y.astype(y_ref.dtype)


def _xattn_prompt_kernel(x_ref, w_ref, mk_ref, mv_ref, y_ref):
    hsl = [slice(h * HEAD_DIM, (h + 1) * HEAD_DIM) for h in range(N_HEADS)]
    for b in range(x_ref.shape[0]):
        q = _mm(x_ref[b], w_ref[...])
        mk, mv = mk_ref[b], mv_ref[b]
        s = [_mm_nt(q[:, hs], mk[:, hs]) * HEAD_DIM ** -0.5 for hs in hsl]
        e = [jnp.exp(a - jnp.max(a, -1, keepdims=True)) for a in s]
        p = [a / jnp.sum(a, -1, keepdims=True) for a in e]
        outs = [_mm(a, mv[:, hs]) for a, hs in zip(p, hsl)]
        y_ref[b] = jnp.concatenate(outs, -1).astype(y_ref.dtype)


def _row_blocks(n):
    sub = min(ROW_SUBTILE, n)
    return [slice(r, r + sub) for r in range(0, n, sub)]


def _merge_kernel(alpha, x_ref, ya_ref, yb_ref, yc_ref, yd_ref, ye_ref, wg_ref, bg_ref, wb_ref, wo_ref,
                  g_ref, b_ref, o_ref):
    for rs in _row_blocks(x_ref.shape[0]):
        x = x_ref[rs, :]
        xb = x.astype(bf16)
        merged = None
        for i, y_ref in enumerate((ya_ref, yb_ref, yc_ref, yd_ref, ye_ref)):
            gate = _sigmoid(_mm(xb, wg_ref[i]) + bg_ref[i])
            term = gate * _mm(y_ref[rs, :], wb_ref[i])
            merged = term if merged is None else merged + term
        mix = _mm(merged, wo_ref[...])
        o_ref[rs, :] = _layernorm(alpha * x + mix, g_ref[...], b_ref[...])


def _ffn_kernel(alpha, n_split, x_ref, wu_ref, wd_ref, g_ref, b_ref, o_ref):
    d_ff = wd_ref.shape[0]
    step = d_ff // n_split
    for rs in _row_blocks(x_ref.shape[0]):
        x = x_ref[rs, :]
        xb = x.astype(bf16)
        ffn = None
        for i in range(n_split):
            cs = slice(i * step, (i + 1) * step)
            vs = slice(d_ff + i * step, d_ff + (i + 1) * step)
            hidden = _silu(_mm(xb, wu_ref[:, cs])) * _mm(xb, wu_ref[:, vs])
            part = _mm(hidden, wd_ref[cs, :])
            ffn = part if ffn is None else ffn + part
        o_ref[rs, :] = _layernorm(alpha * x + ffn, g_ref[...], b_ref[...])


def _row_call(kernel, name, row_ops, consts, layer, d_out):
    rows = row_ops[0].shape[0]
    tile = min(ROW_TILE, rows)
    in_specs = [pl.BlockSpec((tile, a.shape[1]), lambda i: (i, 0)) for a in row_ops]
    in_specs += [pl.BlockSpec((None,) + a.shape[1:], lambda i, n=a.ndim: (layer,) + (0,) * (n - 1),
                              pipeline_mode=pl.Buffered(1)) for a in consts]
    return pl.pallas_call(
        kernel,
        grid=(rows // tile,),
        in_specs=in_specs,
        out_specs=pl.BlockSpec((tile, d_out), lambda i: (i, 0)),
        out_shape=jax.ShapeDtypeStruct((rows, d_out), f32),
        compiler_params=pltpu.CompilerParams(dimension_semantics=("arbitrary",), vmem_limit_bytes=VMEM_LIMIT),
        name=name,
    )(*row_ops, *consts)


def _proj_kernel(x_ref, *refs):
    w_refs, o_ref = refs[:-1], refs[-1]
    x = x_ref[...].astype(bf16)
    off = 0
    for w_ref in w_refs:
        width = w_ref.shape[1]
        o_ref[:, off:off + width] = _mm(x, w_ref[...])
        off += width


def _sample_proj(x, weights):
    rows = x.shape[0]
    w_specs, w_arrays = _const_specs(weights)
    n = sum(w.shape[-1] for w in w_arrays)
    return pl.pallas_call(
        _proj_kernel,
        grid=(1,),
        in_specs=[pl.BlockSpec(x.shape, lambda i: (0, 0))] + w_specs,
        out_specs=pl.BlockSpec((rows, n), lambda i: (0, 0)),
        out_shape=jax.ShapeDtypeStruct((rows, n), f32),
        compiler_params=pltpu.CompilerParams(dimension_semantics=("arbitrary",), vmem_limit_bytes=VMEM_LIMIT),
        name="sample_proj",
    )(x, *w_arrays)


def _columns(rows_list):
    nb = rows_list[0].shape[0]
    out = []
    for b in range(nb):
        tile = jnp.concatenate([r[b:b + 1, :] for r in rows_list]
                               + [jnp.zeros((8 - len(rows_list), HEAD_DIM), f32)], 0)
        out.append(tile.T)
    return out


N_SAMPLE_INPUTS = 24


def _rows_of(parts):
    nb = 1 + max(bi for bi, _ in parts)
    return [jnp.concatenate([parts[bi, h] for bi in range(nb)], 0) for h in range(N_HEADS)]


def _sample_kernel(layer, n_alias, *refs):
    (u_ref, h0_ref, lconv_ref, gconv_ref, sg_ref, sh_ref, sr_ref,
     lcw_ref, lcb_ref, lwa_ref, lba_ref, lwi_ref, lbi_ref, lam_ref,
     gcw_ref, alog_ref, dtb_ref, gnw_ref, lbraw_ref, hnw_ref, rgw_ref, rgb_ref, cos_ref, sin_ref) = refs[:N_SAMPLE_INPUTS]
    (ya_ref, yb_ref, yc_ref, yd_ref, h1_ref, lconv_o_ref, gconv_o_ref,
     sg_o_ref, sh_o_ref, sr_o_ref) = refs[N_SAMPLE_INPUTS + n_alias:]
    if n_alias == 0:
        for o_ref in (sg_o_ref, sh_o_ref, sr_o_ref):
            for other in range(o_ref.shape[0]):
                if other != layer:
                    o_ref[other] = jnp.zeros(o_ref.shape[1:], f32)
        sg_o_ref, sh_o_ref, sr_o_ref = (r.at[layer] for r in (sg_o_ref, sh_o_ref, sr_o_ref))
    nb = SAMPLE_BLOCK
    W = W_BR
    pairs = [(bi, h) for bi in range(nb) for h in range(N_HEADS)]
    u = u_ref[...]
    off = 0

    def take(width):
        nonlocal off
        part = u[:, off:off + width]
        off += width
        return part

    xa, qkv, z = take(W), take(3 * W), take(W)
    hq, fz, hi, hg = take(W), take(W), take(W), take(W)
    rq, rk, rv, rg = take(W), take(W), take(W), take(W)
    xq, ba = take(W), take(HEAD_DIM)

    lcw = lcw_ref[...]
    xc = lcw[CONV_W - 1:CONV_W, :] * xa + lcb_ref[...]
    for j in range(CONV_W - 1):
        xc = xc + lcw[j:j + 1, :] * lconv_ref[j]
    for j in range(CONV_W - 2):
        lconv_o_ref[j] = lconv_ref[j + 1]
    lconv_o_ref[CONV_W - 2] = xa
    a, b = _lru_gates(xc, lwa_ref, lba_ref[...], lwi_ref, lbi_ref[...], lam_ref[...])
    h1 = a * h0_ref[...] + b
    h1_ref[...] = h1
    ya_ref[...] = h1

    gcw = gcw_ref[...]
    c = gcw[CONV_W - 1:CONV_W, :] * qkv
    for j in range(CONV_W - 1):
        c = c + gcw[j:j + 1, :] * gconv_ref[j]
    for j in range(CONV_W - 2):
        gconv_o_ref[j] = gconv_ref[j + 1]
    gconv_o_ref[CONV_W - 2] = qkv
    c = _silu(c)
    beta, g = _gdn_gates(ba, alog_ref[...], dtb_ref[...])
    eg = jnp.exp(g)
    qn, kn, vv, kcol = [], [], [], {}
    for h in range(N_HEADS):
        qn.append(_l2norm(c[:, h * HEAD_DIM:(h + 1) * HEAD_DIM]) * HEAD_DIM ** -0.5)
        kn.append(_l2norm(c[:, W + h * HEAD_DIM:W + (h + 1) * HEAD_DIM]))
        vv.append(c[:, 2 * W + h * HEAD_DIM:2 * W + (h + 1) * HEAD_DIM])
        cols = _columns([kn[h]])
        for bi in range(nb):
            kcol[bi, h] = cols[bi][:, 0:1]
    kS = _rows_of({(bi, h): _mm(kn[h][bi:bi + 1, :], sg_ref[bi, h]) for bi, h in pairs})
    v_new = [beta[:, h:h + 1] * (vv[h] - eg[:, N_HEADS + h:N_HEADS + h + 1] * kS[h]) for h in range(N_HEADS)]
    S1 = {}
    for bi, h in pairs:
        S1[bi, h] = (eg[bi:bi + 1, N_HEADS + h:N_HEADS + h + 1] * sg_ref[bi, h]
                     + kcol[bi, h] * v_new[h][bi:bi + 1, :])
        sg_o_ref[bi, h] = S1[bi, h]
    o = _rows_of({(bi, h): _mm(qn[h][bi:bi + 1, :], S1[bi, h]) for bi, h in pairs})
    yb_ref[...] = _rmsnorm_heads(o, gnw_ref[...]) * _silu(z)

    lb = _hgrn_lower_bound(lbraw_ref[...], layer)
    log_f, kk = _hgrn_gates(fz, lb)
    f = jnp.exp(log_f)
    qs = _silu(hq)
    kcol, fcol = {}, {}
    for h in range(N_HEADS):
        hs = slice(h * HEAD_DIM, (h + 1) * HEAD_DIM)
        cols = _columns([kk[:, hs], f[:, hs]])
        for bi in range(nb):
            kcol[bi, h], fcol[bi, h] = cols[bi][:, 0:1], cols[bi][:, 1:2]
    S1 = {}
    for bi, h in pairs:
        hs = slice(h * HEAD_DIM, (h + 1) * HEAD_DIM)
        S1[bi, h] = fcol[bi, h] * sh_ref[bi, h] + kcol[bi, h] * hi[bi:bi + 1, hs]
        sh_o_ref[bi, h] = S1[bi, h]
    o = _rows_of({(bi, h): _mm(qs[bi:bi + 1, h * HEAD_DIM:(h + 1) * HEAD_DIM], S1[bi, h]) for bi, h in pairs})
    yc_ref[...] = _rmsnorm_heads(o, hnw_ref[...]) * _sigmoid(hg)

    cos, sin = cos_ref[...], sin_ref[...]
    qr, kcol = [], {}
    for h in range(N_HEADS):
        hs = slice(h * HEAD_DIM, (h + 1) * HEAD_DIM)
        qr.append(_rotary(rq[:, hs], cos, sin))
        cols = _columns([_rotary(rk[:, hs], cos, sin) * HEAD_DIM ** -0.5])
        for bi in range(nb):
            kcol[bi, h] = cols[bi][:, 0:1]
    S1 = {}
    for bi, h in pairs:
        hs = slice(h * HEAD_DIM, (h + 1) * HEAD_DIM)
        S1[bi, h] = math.exp(_ret_log_gamma(h)) * sr_ref[bi, h] + kcol[bi, h] * rv[bi:bi + 1, hs]
        sr_o_ref[bi, h] = S1[bi, h]
    o = _rows_of({(bi, h): _mm(qr[h][bi:bi + 1, :], S1[bi, h]) for bi, h in pairs})
    yd_ref[...] = _groupnorm_heads(o, rgw_ref[...], rgb_ref[...]) * _silu(rg)


def _sample_xattn_kernel(q_ref, ck_ref, cv_ref, o_ref):
    for bi in range(q_ref.shape[0]):
        q = q_ref[bi] * HEAD_DIM ** -0.5
        s = jnp.sum(ck_ref[bi] * q[None, :, :], axis=-1, keepdims=True)
        e = jnp.exp(s - jnp.max(s, axis=0, keepdims=True))
        o_ref[bi] = jnp.sum(e * cv_ref[bi], axis=0) / jnp.sum(e, axis=0)


def _sample_xattn(layer, q, ck, cv):
    nb = SAMPLE_BLOCK
    cache_spec = pl.BlockSpec((None, nb) + ck.shape[2:], lambda i: (layer, i, 0, 0, 0))
    q_spec = pl.BlockSpec((nb,) + q.shape[1:], lambda i: (i, 0, 0))
    return pl.pallas_call(
        _sample_xattn_kernel,
        grid=(q.shape[0] // nb,),
        in_specs=[q_spec, cache_spec, cache_spec],
        out_specs=q_spec,
        out_shape=jax.ShapeDtypeStruct(q.shape, f32),
        compiler_params=pltpu.CompilerParams(dimension_semantics=("arbitrary",), vmem_limit_bytes=VMEM_LIMIT),
        name="sample_xattn",
    )(q, ck, cv)


def _sample_branches(layer, u, h0, lconv_t, gconv_t, sg, sh, sr, params, prev_states):
    nb = SAMPLE_BLOCK
    rows = u.shape[0]
    blk2 = lambda a: pl.BlockSpec((nb, a.shape[1]), lambda i: (i, 0))
    blk_t = lambda a: pl.BlockSpec((a.shape[0], nb, a.shape[2]), lambda i: (0, i, 0))
    blk_layer = lambda a: pl.BlockSpec((None, nb) + a.shape[2:],
                                       lambda i, n=a.ndim: (layer, i) + (0,) * (n - 2))
    param_specs, params = _const_specs(params)
    sds = lambda a: jax.ShapeDtypeStruct(a.shape, f32)
    y_shape = jax.ShapeDtypeStruct((rows, W_BR), f32)
    y_spec = pl.BlockSpec((nb, W_BR), lambda i: (i, 0))
    n_in = N_SAMPLE_INPUTS
    assert 7 + len(params) == n_in
    n_small_outs = 7
    blk_state_out = blk_layer if prev_states else (
        lambda a: pl.BlockSpec((a.shape[0], nb) + a.shape[2:], lambda i, n=a.ndim: (0, i) + (0,) * (n - 2)))
    return pl.pallas_call(
        functools.partial(_sample_kernel, layer, len(prev_states)),
        grid=(rows // nb,),
        in_specs=[blk2(u), blk2(h0), blk_t(lconv_t), blk_t(gconv_t), blk_layer(sg), blk_layer(sh), blk_layer(sr)]
        + param_specs + [pl.BlockSpec(memory_space=pl.ANY)] * len(prev_states),
        out_specs=[y_spec] * 4 + [blk2(h0), blk_t(lconv_t), blk_t(gconv_t),
                                  blk_state_out(sg), blk_state_out(sh), blk_state_out(sr)],
        out_shape=[y_shape] * 4 + [sds(h0), sds(lconv_t), sds(gconv_t), sds(sg), sds(sh), sds(sr)],
        input_output_aliases={n_in + j: n_small_outs + j for j in range(len(prev_states))},
        compiler_params=pltpu.CompilerParams(dimension_semantics=("arbitrary",), vmem_limit_bytes=VMEM_LIMIT),
        name="sample_branches",
    )(u, h0, lconv_t, gconv_t, sg, sh, sr, *params, *prev_states)


def kernel(x_prompt, x_sample, mem_prompt, state_lru_h, state_lru_conv, state_gdn_conv, state_gdn_s, state_hgrn_s, state_ret_s, cache_mem_k, cache_mem_v, w_in, lru_conv_w, lru_conv_b, lru_wa, lru_ba, lru_wi, lru_bi, lru_lambda, gdn_conv_w, gdn_a_log, gdn_dt_bias, gdn_norm_w, hgrn_lb_raw, hgrn_norm_w, ret_gn_w, ret_gn_b, w_mem_k, w_mem_v, w_merge_gate, b_merge_gate, w_branch, w_out, ln1_g, ln1_b, w_ffn_up, w_ffn_down, ln2_g, ln2_b):
    B, T, D = x_prompt.shape
    Bs = x_sample.shape[0]
    depth = w_in.shape[0]
    n_mem = mem_prompt.shape[1]
    W, H, HD = W_BR, N_HEADS, HEAD_DIM
    alpha = (2 * depth) ** 0.25
    assert Bs % SAMPLE_BLOCK == 0 and (B * T) % ROW_TILE == 0

    cos_all, sin_all = _rope_tables(list(range(T)) + [PAST_LEN])
    cos_p, sin_p = cos_all[:T], sin_all[:T]
    cos_s, sin_s = cos_all[T:T + 1], sin_all[T:T + 1]

    state_spec = ((H, HD, HD), f32)

    xp = x_prompt
    xs = x_sample.reshape(Bs, D)
    mk_all, mv_all, mkb_all, mvb_all = _memkv(mem_prompt.reshape(B * n_mem, D), w_mem_k, w_mem_v)
    outs_p = [[] for _ in range(6)]
    outs_s = [[] for _ in range(3)]
    sample_states = ()
    layer_rows = lambda a: a.reshape(depth, 1, -1)
    merge_consts = (w_merge_gate.astype(bf16), b_merge_gate.reshape(depth, N_BR, 1, D), w_branch.astype(bf16),
                    w_out.astype(bf16), layer_rows(ln1_g), layer_rows(ln1_b))
    ffn_consts = (w_ffn_up.astype(bf16), w_ffn_down.astype(bf16), layer_rows(ln2_g), layer_rows(ln2_b))
    merge_k = functools.partial(_merge_kernel, alpha)
    ffn_k = functools.partial(_ffn_kernel, alpha, 2)

    o = 0
    pieces = []
    for width in (W, 3 * W, W, H, H, W, W, W, W, W, W, W, W, W):
        pieces.append(w_in[:, :, o:o + width])
        o += width
    (w_xa, w_qkv, w_z, w_gb, w_ga, w_hq, w_hf, w_hi, w_hg, w_rq, w_rk, w_rv, w_rg, w_xq) = pieces
    w_ba = jnp.concatenate([w_gb, w_ga, jnp.zeros((depth, D, HD - 2 * H), f32)], 2).astype(bf16)
    w_xa, w_qkv, w_z, w_xq = (a.astype(bf16) for a in (w_xa, w_qkv, w_z, w_xq))
    w_hgrn = jnp.concatenate([w_hq, w_hf, w_hi, w_hg], 2).astype(bf16)
    w_ret = jnp.concatenate([w_rq, w_rk, w_rv, w_rg], 2).astype(bf16)
    pad_lanes = lambda a: jnp.zeros((depth, 1, HD), f32).at[:, 0, H:2 * H].set(a)
    lru_all = (lru_conv_w, layer_rows(lru_conv_b), lru_wa.astype(bf16), layer_rows(lru_ba), lru_wi.astype(bf16),
               layer_rows(lru_bi), layer_rows(lru_lambda))
    gdn_all = (gdn_conv_w, pad_lanes(gdn_a_log), pad_lanes(gdn_dt_bias), layer_rows(gdn_norm_w))
    hgrn_norm_all = layer_rows(hgrn_norm_w)
    ret_all = (layer_rows(ret_gn_w), layer_rows(ret_gn_b))

    for l in range(depth):
        of_layer = lambda a: _OfLayer(a, l)
        w_xa_l, w_qkv_l, w_z_l, w_ba_l, w_hgrn_l, w_ret_l, w_xq_l = (
            of_layer(a) for a in (w_xa, w_qkv, w_z, w_ba, w_hgrn, w_ret, w_xq))
        lru_params = tuple(of_layer(a) for a in lru_all)
        gdn_params = tuple(of_layer(a) for a in gdn_all)
        hgrn_params = (hgrn_lb_raw, of_layer(hgrn_norm_all))
        ret_params = tuple(of_layer(a) for a in ret_all)

        ya, h_last, lconv = _prompt_call(
            _lru_prompt_kernel, "lru_prompt", xp, T_TILE_WIDE, (w_xa_l,) + lru_params, (), (),
            [(W, bf16)], [((1, W), f32), ((CONV_W - 1, W), f32)],
            [pltpu.VMEM((CARRY_ROWS + T_TILE_WIDE, W), f32), pltpu.VMEM((T_TILE_WIDE, W), f32),
             pltpu.VMEM((T_TILE_WIDE, W), f32), pltpu.VMEM((8, W), f32)])
        yb, gconv, s_gdn = _prompt_call(
            _gdn_prompt_kernel, "gdn_prompt", xp, T_TILE, (w_qkv_l, w_z_l, w_ba_l) + gdn_params, (), (),
            [(W, bf16)], [((CONV_W - 1, 3 * W), f32), state_spec],
            [pltpu.VMEM((GDN_SEQS, CARRY_ROWS + T_TILE, 3 * W), f32), pltpu.VMEM((GDN_SEQS, HD, W), f32)],
            seqs=GDN_SEQS)
        yc, s_hgrn = _prompt_call(
            functools.partial(_hgrn_prompt_kernel, l), "hgrn_prompt", xp, T_TILE, (w_hgrn_l,) + hgrn_params, (), (),
            [(W, bf16)], [state_spec], [pltpu.VMEM((HGRN_SEQS, H, HD, HD), f32)], seqs=HGRN_SEQS)
        yd, s_ret = _prompt_call(
            _ret_prompt_kernel, "ret_prompt", xp, T_TILE, (w_ret_l,) + ret_params, (), (cos_p, sin_p),
            [(W, bf16)], [state_spec], [pltpu.VMEM((RET_SEQS, H, HD, HD), f32)], seqs=RET_SEQS)
        (ye,) = _prompt_call(
            _xattn_prompt_kernel, "xattn_prompt", xp, T_TILE_WIDE, (w_xq_l,),
            (mkb_all[l].reshape(B, n_mem, W), mvb_all[l].reshape(B, n_mem, W)), (), [(W, bf16)], [], [],
            seqs=XATTN_SEQS)
        x2d = xp.reshape(B * T, D)
        ys = [y.reshape(B * T, W) for y in (ya, yb, yc, yd, ye)]
        x2d = _row_call(merge_k, "merge_prompt", [x2d] + ys, merge_consts, l, D)
        x2d = _row_call(ffn_k, "ffn_prompt", [x2d], ffn_consts, l, D)
        xp = x2d.reshape(B, T, D)
        for lst, val in zip(outs_p, (h_last.reshape(B, W), lconv, gconv, s_gdn, s_hgrn, s_ret)):
            lst.append(val)

        u_s = _sample_proj(xs, (w_xa_l, w_qkv_l, w_z_l, w_hgrn_l, w_ret_l, w_xq_l, w_ba_l))
        sample_params = lru_params + gdn_params + hgrn_params + ret_params + (cos_s, sin_s)
        res = _sample_branches(
            l, u_s, state_lru_h[l], state_lru_conv[l].transpose(1, 0, 2), state_gdn_conv[l].transpose(1, 0, 2),
            state_gdn_s, state_hgrn_s, state_ret_s, sample_params, sample_states)
        ys_s, (h1, lconv_s, gconv_s), sample_states = res[:4], res[4:7], tuple(res[7:])
        xq_off = sum(w.shape[2] for w in (w_xa, w_qkv, w_z, w_hgrn, w_ret))
        q_s = u_s[:, xq_off:xq_off + W].reshape(Bs, H, HD)
        ye_s = _sample_xattn(l, q_s, cache_mem_k, cache_mem_v).reshape(Bs, W)
        xs = _row_call(merge_k, "merge_sample", [xs] + list(ys_s) + [ye_s], merge_consts, l, D)
        xs = _row_call(ffn_k, "ffn_sample", [xs], ffn_consts, l, D)
        for lst, val in zip(outs_s, (h1, lconv_s.transpose(1, 0, 2), gconv_s.transpose(1, 0, 2))):
            lst.append(val)

    mem_shape = (depth, B, n_mem, H, HD)
    return (xp, xs.reshape(Bs, 1, D), *(jnp.stack(v) for v in outs_p), mk_all.reshape(mem_shape),
            mv_all.reshape(mem_shape), *(jnp.stack(v) for v in outs_s), *sample_states)
```

```python
import functools
import math

import numpy as np
import jax
import jax.numpy as jnp
from jax import lax
from jax.experimental import pallas as pl
from jax.experimental.pallas import tpu as pltpu

f32 = jnp.float32
bf16 = jnp.bfloat16

N_HEADS = 4
HEAD_DIM = 128
W_BR = N_HEADS * HEAD_DIM
CONV_W = 4
N_BR = 5
LRU_C = 8.0
GDN_CHUNK = 64
HGRN_CHUNK = 16
ROPE_BASE = 10000.0
LN_EPS = 1e-5
NORM_EPS = 1e-6
PAST_LEN = 16384

T_TILE = 256
T_TILE_WIDE = 512
GDN_SEQS = 2
RET_SEQS = 2
XATTN_SEQS = 2
HGRN_SEQS = 2
HGRN_LOOKAHEAD = 2
ROW_TILE = 1024
ROW_SUBTILE = 256
SAMPLE_BLOCK = 8
CARRY_ROWS = 8
VMEM_LIMIT = 52 * 1024 * 1024


def _mm(a, b):
    return jnp.dot(a.astype(bf16), b.astype(bf16), preferred_element_type=f32)


def _mm_nt(a, b):
    return lax.dot_general(a.astype(bf16), b.astype(bf16), (((1,), (1,)), ((), ())), preferred_element_type=f32)


def _mm_tn(a, b):
    return lax.dot_general(a.astype(bf16), b.astype(bf16), (((0,), (0,)), ((), ())), preferred_element_type=f32)


def _mask_mm_exact(mask_bf16, x):
    hi = x.astype(bf16)
    r1 = x - hi.astype(f32)
    mid = r1.astype(bf16)
    lo = (r1 - mid.astype(f32)).astype(bf16)
    dot = functools.partial(jnp.dot, preferred_element_type=f32)
    return dot(mask_bf16, hi) + dot(mask_bf16, mid) + dot(mask_bf16, lo)


def _chunk_masks(n, chunk):
    shift = int(math.log2(chunk))
    row = lax.broadcasted_iota(jnp.int32, (n, n), 0)
    col = lax.broadcasted_iota(jnp.int32, (n, n), 1)
    same = (row >> shift) == (col >> shift)
    return jnp.logical_and(same, col <= row), same


def _as_bf16(mask):
    return jnp.where(mask, 1.0, 0.0).astype(bf16)


def _sigmoid(x):
    return jax.nn.sigmoid(x)


def _silu(x):
    return x * jax.nn.sigmoid(x)


def _layernorm(x, g, b):
    mu = jnp.mean(x, -1, keepdims=True)
    xc = x - mu
    var = jnp.mean(xc * xc, -1, keepdims=True)
    return xc * lax.rsqrt(var + LN_EPS) * g + b


def _head_slices(a):
    return [a[:, h * HEAD_DIM:(h + 1) * HEAD_DIM] for h in range(N_HEADS)]


def _causal_conv(buf_ref, x, cw):
    n = x.shape[0]
    buf_ref[CARRY_ROWS:CARRY_ROWS + n, :] = x
    out = cw[CONV_W - 1:CONV_W, :] * x
    for j in range(CONV_W - 1):
        off = CARRY_ROWS - (CONV_W - 1) + j
        out = out + cw[j:j + 1, :] * buf_ref[off:off + n, :]
    buf_ref[0:CARRY_ROWS, :] = buf_ref[n:n + CARRY_ROWS, :]
    return out


def _conv_state(buf_ref):
    return buf_ref[CARRY_ROWS - (CONV_W - 1):CARRY_ROWS, :]


def _lru_gates(xc, wa_ref, ba, wi_ref, bi, lam):
    xh = _head_slices(xc)
    r = _sigmoid(jnp.concatenate([_mm(xh[h], wa_ref[h]) for h in range(N_HEADS)], -1) + ba)
    ig = _sigmoid(jnp.concatenate([_mm(xh[h], wi_ref[h]) for h in range(N_HEADS)], -1) + bi)
    log_a = -LRU_C * r * jax.nn.softplus(-lam)
    a = jnp.exp(log_a)
    b = jnp.sqrt(jnp.tanh(-log_a) * (1.0 + a * a)) * (ig * xc)
    return a, b


def _gdn_gates(ba, alog_row, dtb_row):
    beta = _sigmoid(ba)
    g = -jnp.exp(alog_row) * jax.nn.softplus(ba + dtb_row)
    return beta, g


def _l2norm(a):
    return a * lax.rsqrt(jnp.sum(a * a, -1, keepdims=True) + NORM_EPS)


def _rmsnorm_heads(o_heads, w):
    parts = [o * lax.rsqrt(jnp.mean(o * o, -1, keepdims=True) + NORM_EPS) for o in o_heads]
    return jnp.concatenate(parts, -1) * w


def _groupnorm_heads(o_heads, w, b):
    parts = []
    for o in o_heads:
        mu = jnp.mean(o, -1, keepdims=True)
        oc = o - mu
        var = jnp.mean(oc * oc, -1, keepdims=True)
        parts.append(oc * lax.rsqrt(var + NORM_EPS))
    return jnp.concatenate(parts, -1) * w + b


def _hgrn_lower_bound(lbraw, layer):
    n = lbraw.shape[0]
    rows = [lbraw[i:i + 1, :] for i in range(n)]
    m = rows[0]
    for r in rows[1:]:
        m = jnp.maximum(m, r)
    es = [jnp.exp(r - m) for r in rows]
    tot = es[0]
    for e in es[1:]:
        tot = tot + e
    acc = jnp.zeros_like(m)
    for i in range(1, layer + 1):
        acc = acc + es[i] / tot
    return acc


def _hgrn_gates(fz, lb):
    log_f = jnp.logaddexp(jnp.log(lb), jnp.log1p(-lb) + jax.nn.log_sigmoid(fz))
    k = (1.0 - lb) * _sigmoid(-fz)
    return log_f, k


def _rotary(x, cos, sin_signed):
    return x * cos + pltpu.roll(x, HEAD_DIM // 2, 1) * sin_signed


def _ret_log_gamma(h):
    return math.log1p(-(2.0 ** (-5.0 - h)))


def _rope_kernel(ang_ref, cos_ref, sin_ref):
    ang = ang_ref[...]
    lane = lax.broadcasted_iota(jnp.int32, ang.shape, 1)
    s = jnp.sin(ang)
    cos_ref[...] = jnp.cos(ang)
    sin_ref[...] = jnp.where(lane < HEAD_DIM // 2, -s, s)


def _rope_tables(positions):
    half = HEAD_DIM // 2
    inv = ROPE_BASE ** (-np.arange(half, dtype=np.float64) / half)
    ang = np.mod(np.asarray(positions, np.float64)[:, None] * inv[None, :], 2.0 * np.pi)
    ang = np.concatenate([ang, ang], axis=1).astype(np.float32)
    n = ang.shape[0]
    pad = (-n) % 8
    ang = np.pad(ang, ((0, pad), (0, 0)))
    shp = jax.ShapeDtypeStruct(ang.shape, f32)
    cos, sin = pl.pallas_call(_rope_kernel, out_shape=(shp, shp), name="rope_tables")(jnp.asarray(ang))
    return cos[:n], sin[:n]


def _memkv_kernel(m_ref, wk_ref, wv_ref, k_ref, v_ref, kb_ref, vb_ref):
    m = m_ref[...]
    k = _mm(m, wk_ref[...])
    v = _mm(m, wv_ref[...])
    k_ref[...] = k
    v_ref[...] = v
    kb_ref[...] = k.astype(bf16)
    vb_ref[...] = v.astype(bf16)


def _memkv(mem2d, wk, wv):
    rows, d = mem2d.shape
    depth = wk.shape[0]
    tile = min(ROW_TILE, rows)
    w_spec = pl.BlockSpec((None, d, W_BR), lambda l, i: (l, 0, 0))
    o_spec = pl.BlockSpec((None, tile, W_BR), lambda l, i: (l, i, 0))
    return pl.pallas_call(
        _memkv_kernel,
        grid=(depth, rows // tile),
        in_specs=[pl.BlockSpec((tile, d), lambda l, i: (i, 0)), w_spec, w_spec],
        out_specs=[o_spec] * 4,
        out_shape=[jax.ShapeDtypeStruct((depth, rows, W_BR), f32)] * 2
        + [jax.ShapeDtypeStruct((depth, rows, W_BR), bf16)] * 2,
        compiler_params=pltpu.CompilerParams(dimension_semantics=("arbitrary", "arbitrary"),
                                             vmem_limit_bytes=VMEM_LIMIT),
        name="mem_kv",
    )(mem2d, wk, wv)


class _OfLayer:
    def __init__(self, array, layer):
        self.array, self.layer = array, layer


def _const_specs(consts):
    specs, operands = [], []
    for a in consts:
        if isinstance(a, _OfLayer):
            arr, layer = a.array, a.layer
            specs.append(pl.BlockSpec((None,) + arr.shape[1:], lambda *_, n=arr.ndim, l=layer: (l,) + (0,) * (n - 1)))
        else:
            arr = a
            specs.append(pl.BlockSpec(arr.shape, lambda *_, n=arr.ndim: (0,) * n))
        operands.append(arr)
    return specs, operands


def _prompt_call(kernel, name, x, tile, consts, per_batch, per_time, outs_tile, outs_batch, scratch, seqs=1):
    B, T, D = x.shape
    assert T % tile == 0 and B % seqs == 0
    nt = T // tile
    in_specs = [pl.BlockSpec((seqs, tile, D), lambda b, t: (b, t, 0))]
    const_specs, consts = _const_specs(consts)
    in_specs += const_specs
    for a in per_batch:
        in_specs.append(pl.BlockSpec((seqs,) + a.shape[1:], lambda b, t, n=a.ndim: (b,) + (0,) * (n - 1)))
    for a in per_time:
        in_specs.append(pl.BlockSpec((tile,) + a.shape[1:], lambda b, t, n=a.ndim: (t,) + (0,) * (n - 1)))
    out_specs, out_shape = [], []
    for width, dt in outs_tile:
        out_specs.append(pl.BlockSpec((seqs, tile, width), lambda b, t: (b, t, 0)))
        out_shape.append(jax.ShapeDtypeStruct((B, T, width), dt))
    for shp, dt in outs_batch:
        out_specs.append(pl.BlockSpec((seqs,) + shp, lambda b, t, n=len(shp): (b,) + (0,) * n))
        out_shape.append(jax.ShapeDtypeStruct((B,) + shp, dt))
    return pl.pallas_call(
        kernel,
        grid=(B // seqs, nt),
        in_specs=in_specs,
        out_specs=out_specs,
        out_shape=out_shape,
        scratch_shapes=scratch,
        compiler_params=pltpu.CompilerParams(dimension_semantics=("arbitrary", "arbitrary"),
                                             vmem_limit_bytes=VMEM_LIMIT),
        name=name,
    )(x, *consts, *per_batch, *per_time)


def _lru_prompt_kernel(x_ref, w_ref, cw_ref, cb_ref, wa_ref, ba_ref, wi_ref, bi_ref, lam_ref,
                       y_ref, h_ref, conv_ref, buf_ref, a_ref, b_ref, hcar_ref):
    n = x_ref.shape[1]

    @pl.when(pl.program_id(1) == 0)
    def _():
        buf_ref[0:CARRY_ROWS, :] = jnp.zeros((CARRY_ROWS, W_BR), f32)
        hcar_ref[...] = jnp.zeros_like(hcar_ref)

    xa = _mm(x_ref[0], w_ref[...])
    xc = _causal_conv(buf_ref, xa, cw_ref[...]) + cb_ref[...]
    a, b = _lru_gates(xc, wa_ref, ba_ref[...], wi_ref, bi_ref[...], lam_ref[...])

    a = a.reshape(n // 8, 8, W_BR)
    b = b.reshape(n // 8, 8, W_BR)
    row8 = lax.broadcasted_iota(jnp.int32, (n // 8, 8, W_BR), 1)
    for d in (1, 2, 4):
        keep = row8 >= d
        a_prev = jnp.where(keep, pltpu.roll(a, d, 1), 1.0)
        b_prev = jnp.where(keep, pltpu.roll(b, d, 1), 0.0)
        b = b + a * b_prev
        a = a * a_prev
    a_ref[...] = a.reshape(n, W_BR)
    b_ref[...] = b.reshape(n, W_BR)

    def group(gi, h_prev):
        i = pl.multiple_of(gi * 8, 8)
        blk = b_ref[pl.ds(i, 8), :] + a_ref[pl.ds(i, 8), :] * h_prev
        b_ref[pl.ds(i, 8), :] = blk
        return jnp.broadcast_to(blk[7:8, :], (8, W_BR))

    h_last = lax.fori_loop(0, n // 8, group, hcar_ref[...])
    hcar_ref[...] = h_last
    y_ref[0] = b_ref[...].astype(y_ref.dtype)
    h_ref[0] = h_last[0:1, :]
    conv_ref[0] = _conv_state(buf_ref)


def _gdn_prompt_kernel(x_ref, wqkv_ref, wz_ref, wba_ref, cw_ref, alog_ref, dtb_ref, nw_ref,
                       y_ref, conv_ref, s_ref, buf_ref, st_ref):
    seqs, n_seq, C = x_ref.shape[0], x_ref.shape[1], GDN_CHUNK
    assert n_seq == N_HEADS * C
    n = seqs * n_seq

    @pl.when(pl.program_id(1) == 0)
    def _():
        buf_ref[:, 0:CARRY_ROWS, :] = jnp.zeros((seqs, CARRY_ROWS, 3 * W_BR), f32)
        st_ref[...] = jnp.zeros_like(st_ref)

    x_seq = [x_ref[s].astype(bf16) for s in range(seqs)]
    x = jnp.concatenate(x_seq, 0)
    qkv = [_mm(x_seq[s], wqkv_ref[...]) for s in range(seqs)]
    c = jnp.concatenate([_silu(_causal_conv(buf_ref.at[s], qkv[s], cw_ref[...])) for s in range(seqs)], 0)
    z = _mm(x, wz_ref[...])
    beta, g = _gdn_gates(_mm(x, wba_ref[...]), alog_ref[...], dtb_ref[...])

    incl_t = _as_bf16(_chunk_masks(n_seq, C)[0])
    G_seq = [_mask_mm_exact(incl_t, g[s * n_seq:(s + 1) * n_seq]) for s in range(seqs)]
    G = jnp.concatenate(G_seq, 0)
    GT = [a.T for a in G_seq]
    H, HD, W = N_HEADS, HEAD_DIM, W_BR
    nck = n // C
    nck_seq = n_seq // C
    heads = range(H)
    tile = lambda a, h: a[:, h * HD:(h + 1) * HD]

    q_all = jnp.concatenate([_l2norm(tile(c, h)) * HD ** -0.5 for h in heads], 1)
    k_all = jnp.concatenate([_l2norm(tile(c, H + h)) for h in heads], 1)
    v_all = c[:, 2 * W:]
    per_head = lambda a, lo: jnp.concatenate([jnp.broadcast_to(a[:, lo + h:lo + h + 1], (n, HD)) for h in heads], 1)
    beta_w = per_head(beta, 0)
    g_w = per_head(G, H)
    glast_w = jnp.concatenate([jnp.broadcast_to(g_w[(ck + 1) * C - 1:(ck + 1) * C, :], (C, W)) for ck in range(nck)], 0)
    eg_w = jnp.exp(g_w)
    kb_all = k_all * beta_w
    vb_all = v_all * beta_w
    kbe_all = kb_all * eg_w
    qe_all = q_all * eg_w
    kend_all = k_all * jnp.exp(glast_w - g_w)
    eglast_w = jnp.exp(glast_w)

    lane_blk = lax.broadcasted_iota(jnp.int32, (n, H * C), 1) >> int(math.log2(C))
    gcol_c = G[:, H + H - 1:H + H]
    for h in reversed(range(H - 1)):
        gcol_c = jnp.where(lane_blk == h, G[:, H + h:H + h + 1], gcol_c)
    ri = lax.broadcasted_iota(jnp.int32, (C, H * C), 0)
    lj = lax.broadcasted_iota(jnp.int32, (C, H * C), 1)
    incl, strict = (lj & (C - 1)) <= ri, (lj & (C - 1)) < ri
    row_blk = lax.broadcasted_iota(jnp.int32, (1, H * C), 1) >> int(math.log2(C))
    bd_mask = (lax.broadcasted_iota(jnp.int32, (H * C, H * C), 0) >> int(math.log2(C))) == (
        lax.broadcasted_iota(jnp.int32, (H * C, H * C), 1) >> int(math.log2(C)))

    def block_diag(parts, width):
        zero = jnp.zeros((parts[0].shape[0], width), parts[0].dtype)
        return jnp.concatenate([jnp.concatenate([parts[h] if h == hp else zero for h in heads], 1) for hp in heads], 0)

    def block_diag_small(m_b):
        return jnp.where(bd_mask, jnp.concatenate([m_b] * H, 0), jnp.zeros((), bf16))

    neg_a, a_qk, rhs_w, rhs_bd = [], [], [], []
    for ck in range(nck):
        sl = slice(ck * C, (ck + 1) * C)
        gt, cs = GT[ck // nck_seq], ck % nck_seq
        g_row = None
        for h in reversed(heads):
            r = gt[H + h:H + h + 1, :]
            r = pltpu.roll(r, ((h - cs) * C) % n_seq, 1) if (h - cs) % nck_seq else r
            g_row = r if g_row is None else jnp.where(row_blk == h, r, g_row)
        decay = jnp.where(incl, jnp.exp(gcol_c[sl] - g_row), 0.0)
        k_b = k_all[sl].astype(bf16)
        k_bd = block_diag([tile(k_b, h) for h in heads], HD)
        neg_a.append(jnp.where(strict, -(_mm_nt(kb_all[sl], k_bd) * decay), 0.0))
        a_qk.append(_mm_nt(q_all[sl], k_bd) * decay)
        parts = [jnp.concatenate([tile(vb_all[sl], h), tile(kbe_all[sl], h)], 1) for h in heads]
        rhs_w.append(jnp.concatenate(parts, 1))
        rhs_bd.append(block_diag([p.astype(bf16) for p in parts], 2 * HD))

    dot = functools.partial(jnp.dot, preferred_element_type=f32)
    m = list(neg_a)
    t_off = list(neg_a)
    m_bd = [block_diag_small(a.astype(bf16)) for a in m]
    for _ in range(int(math.log2(C)) - 1):
        m = [dot(m[ck].astype(bf16), m_bd[ck]) for ck in range(nck)]
        m_bd = [block_diag_small(a.astype(bf16)) for a in m]
        t_off = [t_off[ck] + m[ck] + dot(t_off[ck].astype(bf16), m_bd[ck]) for ck in range(nck)]
    sol = [rhs_w[ck] + dot(t_off[ck].astype(bf16), rhs_bd[ck]) for ck in range(nck)]

    S = [st_ref[s] for s in range(seqs)]
    zero_s = jnp.zeros((HD, HD), bf16)
    o_chunks = [None] * nck
    for cs in range(nck_seq):
        for s in range(seqs):
            ck = s * nck_seq + cs
            sl = slice(ck * C, (ck + 1) * C)
            S_b = S[s].astype(bf16)
            s_pairs = [jnp.concatenate([jnp.concatenate([tile(S_b, p), zero_s], 1),
                                        jnp.concatenate([zero_s, tile(S_b, p + 1)], 1)], 0) for p in (0, 2)]
            u = [sol[ck][:, 2 * h * HD:(2 * h + 1) * HD] for h in heads]
            w = [sol[ck][:, (2 * h + 1) * HD:(2 * h + 2) * HD].astype(bf16) for h in heads]
            w_s = jnp.concatenate([dot(jnp.concatenate([w[p], w[p + 1]], 1), s_pairs[p // 2]) for p in (0, 2)], 1)
            v_new = jnp.concatenate(u, 1) - w_s
            v_b = v_new.astype(bf16)
            v_bd = block_diag([tile(v_b, h) for h in heads], HD)
            qe_b = qe_all[sl].astype(bf16)
            q_s = jnp.concatenate([dot(qe_b[:, p * HD:(p + 2) * HD], s_pairs[p // 2]) for p in (0, 2)], 1)
            o_chunks[ck] = q_s + dot(a_qk[ck].astype(bf16), v_bd)
            kend_stack = jnp.concatenate([tile(kend_all[sl], h) for h in heads], 0)
            S[s] = eglast_w[ck * C:ck * C + 1, :] * S[s] + _mm_tn(kend_stack, v_bd)
    o_all = jnp.concatenate(o_chunks, 0)
    y = _rmsnorm_heads([tile(o_all, h) for h in heads], nw_ref[...]) * _silu(z)
    for s in range(seqs):
        st_ref[s] = S[s]
        for h in heads:
            s_ref[s, h] = tile(S[s], h)
        y_ref[s] = y[s * n_seq:(s + 1) * n_seq].astype(y_ref.dtype)
        conv_ref[s] = _conv_state(buf_ref.at[s])


def _hgrn_prompt_kernel(layer, x_ref, w_ref, lbraw_ref, nw_ref, y_ref, s_ref, st_ref):
    n, C = x_ref.shape[1], HGRN_CHUNK

    @pl.when(pl.program_id(1) == 0)
    def _():
        st_ref[...] = jnp.zeros_like(st_ref)

    seqs = x_ref.shape[0]
    rows, nck = seqs * n, n // C
    u = _mm(jnp.concatenate([x_ref[s] for s in range(seqs)], 0), w_ref[...])
    hq, fz, hi, hg = (u[:, i * W_BR:(i + 1) * W_BR] for i in range(4))
    lb = _hgrn_lower_bound(lbraw_ref[...], layer)
    log_f, k = _hgrn_gates(fz, lb)
    q = _silu(hq)

    incl, _ = _chunk_masks(n, C)
    incl_b = _as_bf16(incl)
    G = jnp.concatenate([_mask_mm_exact(incl_b, log_f[s * n:(s + 1) * n]) for s in range(seqs)], 0)
    g_last = [G[(ck + 1) * C - 1:(ck + 1) * C, :] for ck in range(rows // C)]
    Gtot = jnp.concatenate([jnp.broadcast_to(r, (C, W_BR)) for r in g_last], 0)
    q_g = q * jnp.exp(G)
    k_g = k * jnp.exp(-G)
    k_end = k * jnp.exp(Gtot - G)
    dT = jnp.exp(jnp.concatenate(g_last, 0)).T

    heads = range(N_HEADS)
    hsl = [slice(h * HEAD_DIM, (h + 1) * HEAD_DIM) for h in heads]
    csl = [slice(ck * C, (ck + 1) * C) for ck in range(rows // C)]
    ssl = [slice(s * n, (s + 1) * n) for s in range(seqs)]
    lanes = [(s, h) for s in range(seqs) for h in heads]
    a_qk = {(s, h): jnp.where(incl, _mm_nt(q_g[ssl[s], hsl[h]], k_g[ssl[s], hsl[h]]), 0.0) for s, h in lanes}
    o_intra = {(s, h): _mm(a_qk[s, h], hi[ssl[s], hsl[h]]) for s, h in lanes}
    kv_of = lambda s, h, cs: _mm_tn(k_end[csl[s * nck + cs], hsl[h]], hi[csl[s * nck + cs], hsl[h]])
    kv = {}
    for cs in range(min(HGRN_LOOKAHEAD, nck)):
        for s, h in lanes:
            kv[s, h, cs] = kv_of(s, h, cs)
    S = {(s, h): st_ref[s, h] for s, h in lanes}
    o_inter = {}
    for cs in range(nck):
        if cs + HGRN_LOOKAHEAD < nck:
            for s, h in lanes:
                kv[s, h, cs + HGRN_LOOKAHEAD] = kv_of(s, h, cs + HGRN_LOOKAHEAD)
        for s, h in lanes:
            o_inter[s, h, cs] = _mm(q_g[csl[s * nck + cs], hsl[h]], S[s, h])
        for s, h in lanes:
            S[s, h] = dT[hsl[h], s * nck + cs:s * nck + cs + 1] * S[s, h] + kv.pop((s, h, cs))
    for s, h in lanes:
        st_ref[s, h] = S[s, h]
        s_ref[s, h] = S[s, h]
    o_heads = [jnp.concatenate([o_intra[s, h] + jnp.concatenate([o_inter[s, h, cs] for cs in range(nck)], 0)
                                for s in range(seqs)], 0) for h in heads]

    y = _rmsnorm_heads(o_heads, nw_ref[...]) * _sigmoid(hg)
    for s in range(seqs):
        y_ref[s] = y[ssl[s]].astype(y_ref.dtype)


def _ret_prompt_kernel(x_ref, w_ref, gw_ref, gb_ref, cos_ref, sin_ref, y_ref, s_ref, st_ref):
    n = x_ref.shape[1]

    @pl.when(pl.program_id(1) == 0)
    def _():
        st_ref[...] = jnp.zeros_like(st_ref)

    cos, sin = cos_ref[...], sin_ref[...]
    ri = lax.broadcasted_iota(jnp.int32, (n, n), 0)
    rj = lax.broadcasted_iota(jnp.int32, (n, n), 1)
    dist = (ri - rj).astype(f32)
    pos = lax.broadcasted_iota(jnp.int32, (n, 1), 0).astype(f32)
    heads = range(N_HEADS)
    lg = [_ret_log_gamma(h) for h in heads]

    for s in range(x_ref.shape[0]):
        xb = x_ref[s].astype(bf16)
        proj = [[_mm(xb, w_ref[:, i * W_BR + 2 * p * HEAD_DIM:i * W_BR + (2 * p + 2) * HEAD_DIM]) for i in range(4)]
                for p in range(N_HEADS // 2)]
        piece = lambda i, h: proj[h // 2][i][:, (h % 2) * HEAD_DIM:(h % 2 + 1) * HEAD_DIM]
        q = [_rotary(piece(0, h), cos, sin) for h in heads]
        k = [_rotary(piece(1, h), cos, sin) * HEAD_DIM ** -0.5 for h in heads]
        v = [piece(2, h) for h in heads]
        a_qk = [_mm_nt(q[h], k[h]) * jnp.where(rj <= ri, jnp.exp(dist * lg[h]), 0.0) for h in heads]
        S = [st_ref[s, h] for h in heads]
        o = [_mm(a_qk[h], v[h]) + _mm(q[h] * jnp.exp((pos + 1.0) * lg[h]), S[h]) for h in heads]
        S = [math.exp(n * lg[h]) * S[h] + _mm_tn(k[h] * jnp.exp((n - 1.0 - pos) * lg[h]), v[h]) for h in heads]
        for h in heads:
            st_ref[s, h] = S[h]
            s_ref[s, h] = S[h]
        rg = jnp.concatenate([piece(3, h) for h in heads], 1)
        y = _groupnorm_heads(o, gw_ref[...], gb_ref[...]) * _silu(rg)
        y_ref[s] = y.astype(y_ref.dtype)


def _xattn_prompt_kernel(x_ref, w_ref, mk_ref, mv_ref, y_ref):
    hsl = [slice(h * HEAD_DIM, (h + 1) * HEAD_DIM) for h in range(N_HEADS)]
    for b in range(x_ref.shape[0]):
        q = _mm(x_ref[b], w_ref[...])
        mk, mv = mk_ref[b], mv_ref[b]
        s = [_mm_nt(q[:, hs], mk[:, hs]) * HEAD_DIM ** -0.5 for hs in hsl]
        e = [jnp.exp(a - jnp.max(a, -1, keepdims=True)) for a in s]
        p = [a / jnp.sum(a, -1, keepdims=True) for a in e]
        outs = [_mm(a, mv[:, hs]) for a, hs in zip(p, hsl)]
        y_ref[b] = jnp.concatenate(outs, -1).astype(y_ref.dtype)


def _row_blocks(n):
    sub = min(ROW_SUBTILE, n)
    return [slice(r, r + sub) for r in range(0, n, sub)]


def _merge_kernel(alpha, x_ref, ya_ref, yb_ref, yc_ref, yd_ref, ye_ref, wg_ref, bg_ref, wb_ref, wo_ref,
                  g_ref, b_ref, o_ref):
    for rs in _row_blocks(x_ref.shape[0]):
        x = x_ref[rs, :]
        xb = x.astype(bf16)
        merged = None
        for i, y_ref in enumerate((ya_ref, yb_ref, yc_ref, yd_ref, ye_ref)):
            gate = _sigmoid(_mm(xb, wg_ref[i]) + bg_ref[i])
            term = gate * _mm(y_ref[rs, :], wb_ref[i])
            merged = term if merged is None else merged + term
        mix = _mm(merged, wo_ref[...])
        o_ref[rs, :] = _layernorm(alpha * x + mix, g_ref[...], b_ref[...])


def _ffn_kernel(alpha, n_split, x_ref, wu_ref, wd_ref, g_ref, b_ref, o_ref):
    d_ff = wd_ref.shape[0]
    step = d_ff // n_split
    for rs in _row_blocks(x_ref.shape[0]):
        x = x_ref[rs, :]
        xb = x.astype(bf16)
        ffn = None
        for i in range(n_split):
            cs = slice(i * step, (i + 1) * step)
            vs = slice(d_ff + i * step, d_ff + (i + 1) * step)
            hidden = _silu(_mm(xb, wu_ref[:, cs])) * _mm(xb, wu_ref[:, vs])
            part = _mm(hidden, wd_ref[cs, :])
            ffn = part if ffn is None else ffn + part
        o_ref[rs, :] = _layernorm(alpha * x + ffn, g_ref[...], b_ref[...])


def _row_call(kernel, name, row_ops, consts, layer, d_out):
    rows = row_ops[0].shape[0]
    tile = min(ROW_TILE, rows)
    in_specs = [pl.BlockSpec((tile, a.shape[1]), lambda i: (i, 0)) for a in row_ops]
    in_specs += [pl.BlockSpec((None,) + a.shape[1:], lambda i, n=a.ndim: (layer,) + (0,) * (n - 1),
                              pipeline_mode=pl.Buffered(1)) for a in consts]
    return pl.pallas_call(
        kernel,
        grid=(rows // tile,),
        in_specs=in_specs,
        out_specs=pl.BlockSpec((tile, d_out), lambda i: (i, 0)),
        out_shape=jax.ShapeDtypeStruct((rows, d_out), f32),
        compiler_params=pltpu.CompilerParams(dimension_semantics=("arbitrary",), vmem_limit_bytes=VMEM_LIMIT),
        name=name,
    )(*row_ops, *consts)


def _proj_kernel(x_ref, *refs):
    w_refs, o_ref = refs[:-1], refs[-1]
    x = x_ref[...].astype(bf16)
    off = 0
    for w_ref in w_refs:
        width = w_ref.shape[1]
        o_ref[:, off:off + width] = _mm(x, w_ref[...])
        off += width


def _sample_proj(x, weights):
    rows = x.shape[0]
    w_specs, w_arrays = _const_specs(weights)
    n = sum(w.shape[-1] for w in w_arrays)
    return pl.pallas_call(
        _proj_kernel,
        grid=(1,),
        in_specs=[pl.BlockSpec(x.shape, lambda i: (0, 0))] + w_specs,
        out_specs=pl.BlockSpec((rows, n), lambda i: (0, 0)),
        out_shape=jax.ShapeDtypeStruct((rows, n), f32),
        compiler_params=pltpu.CompilerParams(dimension_semantics=("arbitrary",), vmem_limit_bytes=VMEM_LIMIT),
        name="sample_proj",
    )(x, *w_arrays)


def _columns(rows_list):
    nb = rows_list[0].shape[0]
    out = []
    for b in range(nb):
        tile = jnp.concatenate([r[b:b + 1, :] for r in rows_list]
                               + [jnp.zeros((8 - len(rows_list), HEAD_DIM), f32)], 0)
        out.append(tile.T)
    return out


N_SAMPLE_INPUTS = 24


def _rows_of(parts):
    nb = 1 + max(bi for bi, _ in parts)
    return [jnp.concatenate([parts[bi, h] for bi in range(nb)], 0) for h in range(N_HEADS)]


def _sample_kernel(layer, n_alias, *refs):
    (u_ref, h0_ref, lconv_ref, gconv_ref, sg_ref, sh_ref, sr_ref,
     lcw_ref, lcb_ref, lwa_ref, lba_ref, lwi_ref, lbi_ref, lam_ref,
     gcw_ref, alog_ref, dtb_ref, gnw_ref, lbraw_ref, hnw_ref, rgw_ref, rgb_ref, cos_ref, sin_ref) = refs[:N_SAMPLE_INPUTS]
    (ya_ref, yb_ref, yc_ref, yd_ref, h1_ref, lconv_o_ref, gconv_o_ref,
     sg_o_ref, sh_o_ref, sr_o_ref) = refs[N_SAMPLE_INPUTS + n_alias:]
    if n_alias == 0:
        for o_ref in (sg_o_ref, sh_o_ref, sr_o_ref):
            for other in range(o_ref.shape[0]):
                if other != layer:
                    o_ref[other] = jnp.zeros(o_ref.shape[1:], f32)
        sg_o_ref, sh_o_ref, sr_o_ref = (r.at[layer] for r in (sg_o_ref, sh_o_ref, sr_o_ref))
    nb = SAMPLE_BLOCK
    W = W_BR
    pairs = [(bi, h) for bi in range(nb) for h in range(N_HEADS)]
    u = u_ref[...]
    off = 0

    def take(width):
        nonlocal off
        part = u[:, off:off + width]
        off += width
        return part

    xa, qkv, z = take(W), take(3 * W), take(W)
    hq, fz, hi, hg = take(W), take(W), take(W), take(W)
    rq, rk, rv, rg = take(W), take(W), take(W), take(W)
    xq, ba = take(W), take(HEAD_DIM)

    lcw = lcw_ref[...]
    xc = lcw[CONV_W - 1:CONV_W, :] * xa + lcb_ref[...]
    for j in range(CONV_W - 1):
        xc = xc + lcw[j:j + 1, :] * lconv_ref[j]
    for j in range(CONV_W - 2):
        lconv_o_ref[j] = lconv_ref[j + 1]
    lconv_o_ref[CONV_W - 2] = xa
    a, b = _lru_gates(xc, lwa_ref, lba_ref[...], lwi_ref, lbi_ref[...], lam_ref[...])
    h1 = a * h0_ref[...] + b
    h1_ref[...] = h1
    ya_ref[...] = h1

    gcw = gcw_ref[...]
    c = gcw[CONV_W - 1:CONV_W, :] * qkv
    for j in range(CONV_W - 1):
        c = c + gcw[j:j + 1, :] * gconv_ref[j]
    for j in range(CONV_W - 2):
        gconv_o_ref[j] = gconv_ref[j + 1]
    gconv_o_ref[CONV_W - 2] = qkv
    c = _silu(c)
    beta, g = _gdn_gates(ba, alog_ref[...], dtb_ref[...])
    eg = jnp.exp(g)
    qn, kn, vv, kcol = [], [], [], {}
    for h in range(N_HEADS):
        qn.append(_l2norm(c[:, h * HEAD_DIM:(h + 1) * HEAD_DIM]) * HEAD_DIM ** -0.5)
        kn.append(_l2norm(c[:, W + h * HEAD_DIM:W + (h + 1) * HEAD_DIM]))
        vv.append(c[:, 2 * W + h * HEAD_DIM:2 * W + (h + 1) * HEAD_DIM])
        cols = _columns([kn[h]])
        for bi in range(nb):
            kcol[bi, h] = cols[bi][:, 0:1]
    kS = _rows_of({(bi, h): _mm(kn[h][bi:bi + 1, :], sg_ref[bi, h]) for bi, h in pairs})
    v_new = [beta[:, h:h + 1] * (vv[h] - eg[:, N_HEADS + h:N_HEADS + h + 1] * kS[h]) for h in range(N_HEADS)]
    S1 = {}
    for bi, h in pairs:
        S1[bi, h] = (eg[bi:bi + 1, N_HEADS + h:N_HEADS + h + 1] * sg_ref[bi, h]
                     + kcol[bi, h] * v_new[h][bi:bi + 1, :])
        sg_o_ref[bi, h] = S1[bi, h]
    o = _rows_of({(bi, h): _mm(qn[h][bi:bi + 1, :], S1[bi, h]) for bi, h in pairs})
    yb_ref[...] = _rmsnorm_heads(o, gnw_ref[...]) * _silu(z)

    lb = _hgrn_lower_bound(lbraw_ref[...], layer)
    log_f, kk = _hgrn_gates(fz, lb)
    f = jnp.exp(log_f)
    qs = _silu(hq)
    kcol, fcol = {}, {}
    for h in range(N_HEADS):
        hs = slice(h * HEAD_DIM, (h + 1) * HEAD_DIM)
        cols = _columns([kk[:, hs], f[:, hs]])
        for bi in range(nb):
            kcol[bi, h], fcol[bi, h] = cols[bi][:, 0:1], cols[bi][:, 1:2]
    S1 = {}
    for bi, h in pairs:
        hs = slice(h * HEAD_DIM, (h + 1) * HEAD_DIM)
        S1[bi, h] = fcol[bi, h] * sh_ref[bi, h] + kcol[bi, h] * hi[bi:bi + 1, hs]
        sh_o_ref[bi, h] = S1[bi, h]
    o = _rows_of({(bi, h): _mm(qs[bi:bi + 1, h * HEAD_DIM:(h + 1) * HEAD_DIM], S1[bi, h]) for bi, h in pairs})
    yc_ref[...] = _rmsnorm_heads(o, hnw_ref[...]) * _sigmoid(hg)

    cos, sin = cos_ref[...], sin_ref[...]
    qr, kcol = [], {}
    for h in range(N_HEADS):
        hs = slice(h * HEAD_DIM, (h + 1) * HEAD_DIM)
        qr.append(_rotary(rq[:, hs], cos, sin))
        cols = _columns([_rotary(rk[:, hs], cos, sin) * HEAD_DIM ** -0.5])
        for bi in range(nb):
            kcol[bi, h] = cols[bi][:, 0:1]
    S1 = {}
    for bi, h in pairs:
        hs = slice(h * HEAD_DIM, (h + 1) * HEAD_DIM)
        S1[bi, h] = math.exp(_ret_log_gamma(h)) * sr_ref[bi, h] + kcol[bi, h] * rv[bi:bi + 1, hs]
        sr_o_ref[bi, h] = S1[bi, h]
    o = _rows_of({(bi, h): _mm(qr[h][bi:bi + 1, :], S1[bi, h]) for bi, h in pairs})
    yd_ref[...] = _groupnorm_heads(o, rgw_ref[...], rgb_ref[...]) * _silu(rg)


def _sample_xattn_kernel(q_ref, ck_ref, cv_ref, o_ref):
    for bi in range(q_ref.shape[0]):
        q = q_ref[bi] * (HEAD_DIM ** -0.5 * math.log2(math.e))
        s = jnp.sum(ck_ref[bi] * q[None, :, :], axis=-1, keepdims=True)
        e = jnp.exp2(s - jnp.max(s, axis=0, keepdims=True))
        o_ref[bi] = jnp.sum(e * cv_ref[bi], axis=0) / jnp.sum(e, axis=0)


def _sample_xattn(layer, q, ck, cv):
    nb = SAMPLE_BLOCK
    cache_spec = pl.BlockSpec((None, nb) + ck.shape[2:], lambda i: (layer, i, 0, 0, 0))
    q_spec = pl.BlockSpec((nb,) + q.shape[1:], lambda i: (i, 0, 0))
    return pl.pallas_call(
        _sample_xattn_kernel,
        grid=(q.shape[0] // nb,),
        in_specs=[q_spec, cache_spec, cache_spec],
        out_specs=q_spec,
        out_shape=jax.ShapeDtypeStruct(q.shape, f32),
        compiler_params=pltpu.CompilerParams(dimension_semantics=("arbitrary",), vmem_limit_bytes=VMEM_LIMIT),
        name="sample_xattn",
    )(q, ck, cv)


def _sample_branches(layer, u, h0, lconv_t, gconv_t, sg, sh, sr, params, prev_states):
    nb = SAMPLE_BLOCK
    rows = u.shape[0]
    blk2 = lambda a: pl.BlockSpec((nb, a.shape[1]), lambda i: (i, 0))
    blk_t = lambda a: pl.BlockSpec((a.shape[0], nb, a.shape[2]), lambda i: (0, i, 0))
    blk_layer = lambda a: pl.BlockSpec((None, nb) + a.shape[2:],
                                       lambda i, n=a.ndim: (layer, i) + (0,) * (n - 2))
    param_specs, params = _const_specs(params)
    sds = lambda a: jax.ShapeDtypeStruct(a.shape, f32)
    y_shape = jax.ShapeDtypeStruct((rows, W_BR), f32)
    y_spec = pl.BlockSpec((nb, W_BR), lambda i: (i, 0))
    n_in = N_SAMPLE_INPUTS
    assert 7 + len(params) == n_in
    n_small_outs = 7
    blk_state_out = blk_layer if prev_states else (
        lambda a: pl.BlockSpec((a.shape[0], nb) + a.shape[2:], lambda i, n=a.ndim: (0, i) + (0,) * (n - 2)))
    return pl.pallas_call(
        functools.partial(_sample_kernel, layer, len(prev_states)),
        grid=(rows // nb,),
        in_specs=[blk2(u), blk2(h0), blk_t(lconv_t), blk_t(gconv_t), blk_layer(sg), blk_layer(sh), blk_layer(sr)]
        + param_specs + [pl.BlockSpec(memory_space=pl.ANY)] * len(prev_states),
        out_specs=[y_spec] * 4 + [blk2(h0), blk_t(lconv_t), blk_t(gconv_t),
                                  blk_state_out(sg), blk_state_out(sh), blk_state_out(sr)],
        out_shape=[y_shape] * 4 + [sds(h0), sds(lconv_t), sds(gconv_t), sds(sg), sds(sh), sds(sr)],
        input_output_aliases={n_in + j: n_small_outs + j for j in range(len(prev_states))},
        compiler_params=pltpu.CompilerParams(dimension_semantics=("arbitrary",), vmem_limit_bytes=VMEM_LIMIT),
        name="sample_branches",
    )(u, h0, lconv_t, gconv_t, sg, sh, sr, *params, *prev_states)


def kernel(x_prompt, x_sample, mem_prompt, state_lru_h, state_lru_conv, state_gdn_conv, state_gdn_s, state_hgrn_s, state_ret_s, cache_mem_k, cache_mem_v, w_in, lru_conv_w, lru_conv_b, lru_wa, lru_ba, lru_wi, lru_bi, lru_lambda, gdn_conv_w, gdn_a_log, gdn_dt_bias, gdn_norm_w, hgrn_lb_raw, hgrn_norm_w, ret_gn_w, ret_gn_b, w_mem_k, w_mem_v, w_merge_gate, b_merge_gate, w_branch, w_out, ln1_g, ln1_b, w_ffn_up, w_ffn_down, ln2_g, ln2_b):
    B, T, D = x_prompt.shape
    Bs = x_sample.shape[0]
    depth = w_in.shape[0]
    n_mem = mem_prompt.shape[1]
    W, H, HD = W_BR, N_HEADS, HEAD_DIM
    alpha = (2 * depth) ** 0.25
    assert Bs % SAMPLE_BLOCK == 0 and (B * T) % ROW_TILE == 0

    cos_all, sin_all = _rope_tables(list(range(T)) + [PAST_LEN])
    cos_p, sin_p = cos_all[:T], sin_all[:T]
    cos_s, sin_s = cos_all[T:T + 1], sin_all[T:T + 1]

    state_spec = ((H, HD, HD), f32)

    xp = x_prompt
    xs = x_sample.reshape(Bs, D)
    mk_all, mv_all, mkb_all, mvb_all = _memkv(mem_prompt.reshape(B * n_mem, D), w_mem_k, w_mem_v)
    outs_p = [[] for _ in range(6)]
    outs_s = [[] for _ in range(3)]
    sample_states = ()
    layer_rows = lambda a: a.reshape(depth, 1, -1)
    merge_consts = (w_merge_gate.astype(bf16), b_merge_gate.reshape(depth, N_BR, 1, D), w_branch.astype(bf16),
                    w_out.astype(bf16), layer_rows(ln1_g), layer_rows(ln1_b))
    ffn_consts = (w_ffn_up.astype(bf16), w_ffn_down.astype(bf16), layer_rows(ln2_g), layer_rows(ln2_b))
    merge_k = functools.partial(_merge_kernel, alpha)
    ffn_k = functools.partial(_ffn_kernel, alpha, 2)

    o = 0
    pieces = []
    for width in (W, 3 * W, W, H, H, W, W, W, W, W, W, W, W, W):
        pieces.append(w_in[:, :, o:o + width])
        o += width
    (w_xa, w_qkv, w_z, w_gb, w_ga, w_hq, w_hf, w_hi, w_hg, w_rq, w_rk, w_rv, w_rg, w_xq) = pieces
    w_ba = jnp.concatenate([w_gb, w_ga, jnp.zeros((depth, D, HD - 2 * H), f32)], 2).astype(bf16)
    w_xa, w_qkv, w_z, w_xq = (a.astype(bf16) for a in (w_xa, w_qkv, w_z, w_xq))
    w_hgrn = jnp.concatenate([w_hq, w_hf, w_hi, w_hg], 2).astype(bf16)
    w_ret = jnp.concatenate([w_rq, w_rk, w_rv, w_rg], 2).astype(bf16)
    pad_lanes = lambda a: jnp.zeros((depth, 1, HD), f32).at[:, 0, H:2 * H].set(a)
    lru_all = (lru_conv_w, layer_rows(lru_conv_b), lru_wa.astype(bf16), layer_rows(lru_ba), lru_wi.astype(bf16),
               layer_rows(lru_bi), layer_rows(lru_lambda))
    gdn_all = (gdn_conv_w, pad_lanes(gdn_a_log), pad_lanes(gdn_dt_bias), layer_rows(gdn_norm_w))
    hgrn_norm_all = layer_rows(hgrn_norm_w)
    ret_all = (layer_rows(ret_gn_w), layer_rows(ret_gn_b))

    for l in range(depth):
        of_layer = lambda a: _OfLayer(a, l)
        w_xa_l, w_qkv_l, w_z_l, w_ba_l, w_hgrn_l, w_ret_l, w_xq_l = (
            of_layer(a) for a in (w_xa, w_qkv, w_z, w_ba, w_hgrn, w_ret, w_xq))
        lru_params = tuple(of_layer(a) for a in lru_all)
        gdn_params = tuple(of_layer(a) for a in gdn_all)
        hgrn_params = (hgrn_lb_raw, of_layer(hgrn_norm_all))
        ret_params = tuple(of_layer(a) for a in ret_all)

        ya, h_last, lconv = _prompt_call(
            _lru_prompt_kernel, "lru_prompt", xp, T_TILE_WIDE, (w_xa_l,) + lru_params, (), (),
            [(W, bf16)], [((1, W), f32), ((CONV_W - 1, W), f32)],
            [pltpu.VMEM((CARRY_ROWS + T_TILE_WIDE, W), f32), pltpu.VMEM((T_TILE_WIDE, W), f32),
             pltpu.VMEM((T_TILE_WIDE, W), f32), pltpu.VMEM((8, W), f32)])
        yb, gconv, s_gdn = _prompt_call(
            _gdn_prompt_kernel, "gdn_prompt", xp, T_TILE, (w_qkv_l, w_z_l, w_ba_l) + gdn_params, (), (),
            [(W, bf16)], [((CONV_W - 1, 3 * W), f32), state_spec],
            [pltpu.VMEM((GDN_SEQS, CARRY_ROWS + T_TILE, 3 * W), f32), pltpu.VMEM((GDN_SEQS, HD, W), f32)],
            seqs=GDN_SEQS)
        yc, s_hgrn = _prompt_call(
            functools.partial(_hgrn_prompt_kernel, l), "hgrn_prompt", xp, T_TILE, (w_hgrn_l,) + hgrn_params, (), (),
            [(W, bf16)], [state_spec], [pltpu.VMEM((HGRN_SEQS, H, HD, HD), f32)], seqs=HGRN_SEQS)
        yd, s_ret = _prompt_call(
            _ret_prompt_kernel, "ret_prompt", xp, T_TILE, (w_ret_l,) + ret_params, (), (cos_p, sin_p),
            [(W, bf16)], [state_spec], [pltpu.VMEM((RET_SEQS, H, HD, HD), f32)], seqs=RET_SEQS)
        (ye,) = _prompt_call(
            _xattn_prompt_kernel, "xattn_prompt", xp, T_TILE_WIDE, (w_xq_l,),
            (mkb_all[l].reshape(B, n_mem, W), mvb_all[l].reshape(B, n_mem, W)), (), [(W, bf16)], [], [],
            seqs=XATTN_SEQS)
        x2d = xp.reshape(B * T, D)
        ys = [y.reshape(B * T, W) for y in (ya, yb, yc, yd, ye)]
        x2d = _row_call(merge_k, "merge_prompt", [x2d] + ys, merge_consts, l, D)
        x2d = _row_call(ffn_k, "ffn_prompt", [x2d], ffn_consts, l, D)
        xp = x2d.reshape(B, T, D)
        for lst, val in zip(outs_p, (h_last.reshape(B, W), lconv, gconv, s_gdn, s_hgrn, s_ret)):
            lst.append(val)

        u_s = _sample_proj(xs, (w_xa_l, w_qkv_l, w_z_l, w_hgrn_l, w_ret_l, w_xq_l, w_ba_l))
        sample_params = lru_params + gdn_params + hgrn_params + ret_params + (cos_s, sin_s)
        res = _sample_branches(
            l, u_s, state_lru_h[l], state_lru_conv[l].transpose(1, 0, 2), state_gdn_conv[l].transpose(1, 0, 2),
            state_gdn_s, state_hgrn_s, state_ret_s, sample_params, sample_states)
        ys_s, (h1, lconv_s, gconv_s), sample_states = res[:4], res[4:7], tuple(res[7:])
        xq_off = sum(w.shape[2] for w in (w_xa, w_qkv, w_z, w_hgrn, w_ret))
        q_s = u_s[:, xq_off:xq_off + W].reshape(Bs, H, HD)
        ye_s = _sample_xattn(l, q_s, cache_mem_k, cache_mem_v).reshape(Bs, W)
        xs = _row_call(merge_k, "merge_sample", [xs] + list(ys_s) + [ye_s], merge_consts, l, D)
        xs = _row_call(ffn_k, "ffn_sample", [xs], ffn_consts, l, D)
        for lst, val in zip(outs_s, (h1, lconv_s.transpose(1, 0, 2), gconv_s.transpose(1, 0, 2))):
            lst.append(val)

    mem_shape = (depth, B, n_mem, H, HD)
    return (xp, xs.reshape(Bs, 1, D), *(jnp.stack(v) for v in outs_p), mk_all.reshape(mem_shape),
            mv_all.reshape(mem_shape), *(jnp.stack(v) for v in outs_s), *sample_states)
```

```python
import functools
import math

import numpy as np
import jax
import jax.numpy as jnp
from jax import lax
from jax.experimental import pallas as pl
from jax.experimental.pallas import tpu as pltpu

f32 = jnp.float32
bf16 = jnp.bfloat16

N_HEADS = 4
HEAD_DIM = 128
W_BR = N_HEADS * HEAD_DIM
CONV_W = 4
N_BR = 5
LRU_C = 8.0
GDN_CHUNK = 64
HGRN_CHUNK = 16
ROPE_BASE = 10000.0
LN_EPS = 1e-5
NORM_EPS = 1e-6
PAST_LEN = 16384

T_TILE = 256
T_TILE_WIDE = 512
GDN_SEQS = 2
RET_SEQS = 4
XATTN_SEQS = 4
HGRN_SEQS = 2
HGRN_LOOKAHEAD = 2
ROW_TILE = 1024
ROW_SUBTILE = 256
SAMPLE_BLOCK = 8
CARRY_ROWS = 8
VMEM_LIMIT = 52 * 1024 * 1024


def _mm(a, b):
    return jnp.dot(a.astype(bf16), b.astype(bf16), preferred_element_type=f32)


def _mm_nt(a, b):
    return lax.dot_general(a.astype(bf16), b.astype(bf16), (((1,), (1,)), ((), ())), preferred_element_type=f32)


def _mm_tn(a, b):
    return lax.dot_general(a.astype(bf16), b.astype(bf16), (((0,), (0,)), ((), ())), preferred_element_type=f32)


def _mask_mm_exact(mask_bf16, x):
    hi = x.astype(bf16)
    r1 = x - hi.astype(f32)
    mid = r1.astype(bf16)
    lo = (r1 - mid.astype(f32)).astype(bf16)
    dot = functools.partial(jnp.dot, preferred_element_type=f32)
    return dot(mask_bf16, hi) + dot(mask_bf16, mid) + dot(mask_bf16, lo)


def _chunk_masks(n, chunk):
    shift = int(math.log2(chunk))
    row = lax.broadcasted_iota(jnp.int32, (n, n), 0)
    col = lax.broadcasted_iota(jnp.int32, (n, n), 1)
    same = (row >> shift) == (col >> shift)
    return jnp.logical_and(same, col <= row), same


def _as_bf16(mask):
    return jnp.where(mask, 1.0, 0.0).astype(bf16)


def _sigmoid(x):
    return jax.nn.sigmoid(x)


def _silu(x):
    return x * jax.nn.sigmoid(x)


def _layernorm(x, g, b):
    mu = jnp.mean(x, -1, keepdims=True)
    xc = x - mu
    var = jnp.mean(xc * xc, -1, keepdims=True)
    return xc * lax.rsqrt(var + LN_EPS) * g + b


def _head_slices(a):
    return [a[:, h * HEAD_DIM:(h + 1) * HEAD_DIM] for h in range(N_HEADS)]


def _causal_conv(buf_ref, x, cw):
    n = x.shape[0]
    buf_ref[CARRY_ROWS:CARRY_ROWS + n, :] = x
    out = cw[CONV_W - 1:CONV_W, :] * x
    for j in range(CONV_W - 1):
        off = CARRY_ROWS - (CONV_W - 1) + j
        out = out + cw[j:j + 1, :] * buf_ref[off:off + n, :]
    buf_ref[0:CARRY_ROWS, :] = buf_ref[n:n + CARRY_ROWS, :]
    return out


def _conv_state(buf_ref):
    return buf_ref[CARRY_ROWS - (CONV_W - 1):CARRY_ROWS, :]


def _lru_gates(xc, wa_ref, ba, wi_ref, bi, lam):
    xh = _head_slices(xc)
    r = _sigmoid(jnp.concatenate([_mm(xh[h], wa_ref[h]) for h in range(N_HEADS)], -1) + ba)
    ig = _sigmoid(jnp.concatenate([_mm(xh[h], wi_ref[h]) for h in range(N_HEADS)], -1) + bi)
    log_a = -LRU_C * r * jax.nn.softplus(-lam)
    a = jnp.exp(log_a)
    b = jnp.sqrt(jnp.tanh(-log_a) * (1.0 + a * a)) * (ig * xc)
    return a, b


def _gdn_gates(ba, alog_row, dtb_row):
    beta = _sigmoid(ba)
    g = -jnp.exp(alog_row) * jax.nn.softplus(ba + dtb_row)
    return beta, g


def _l2norm(a):
    return a * lax.rsqrt(jnp.sum(a * a, -1, keepdims=True) + NORM_EPS)


def _rmsnorm_heads(o_heads, w):
    parts = [o * lax.rsqrt(jnp.mean(o * o, -1, keepdims=True) + NORM_EPS) for o in o_heads]
    return jnp.concatenate(parts, -1) * w


def _groupnorm_heads(o_heads, w, b):
    parts = []
    for o in o_heads:
        mu = jnp.mean(o, -1, keepdims=True)
        oc = o - mu
        var = jnp.mean(oc * oc, -1, keepdims=True)
        parts.append(oc * lax.rsqrt(var + NORM_EPS))
    return jnp.concatenate(parts, -1) * w + b


def _hgrn_lower_bound(lbraw, layer):
    n = lbraw.shape[0]
    rows = [lbraw[i:i + 1, :] for i in range(n)]
    m = rows[0]
    for r in rows[1:]:
        m = jnp.maximum(m, r)
    es = [jnp.exp(r - m) for r in rows]
    tot = es[0]
    for e in es[1:]:
        tot = tot + e
    acc = jnp.zeros_like(m)
    for i in range(1, layer + 1):
        acc = acc + es[i] / tot
    return acc


def _hgrn_gates(fz, lb):
    log_f = jnp.logaddexp(jnp.log(lb), jnp.log1p(-lb) + jax.nn.log_sigmoid(fz))
    k = (1.0 - lb) * _sigmoid(-fz)
    return log_f, k


def _rotary(x, cos, sin_signed):
    return x * cos + pltpu.roll(x, HEAD_DIM // 2, 1) * sin_signed


def _ret_log_gamma(h):
    return math.log1p(-(2.0 ** (-5.0 - h)))


def _rope_kernel(ang_ref, cos_ref, sin_ref):
    ang = ang_ref[...]
    lane = lax.broadcasted_iota(jnp.int32, ang.shape, 1)
    s = jnp.sin(ang)
    cos_ref[...] = jnp.cos(ang)
    sin_ref[...] = jnp.where(lane < HEAD_DIM // 2, -s, s)


def _rope_tables(positions):
    half = HEAD_DIM // 2
    inv = ROPE_BASE ** (-np.arange(half, dtype=np.float64) / half)
    ang = np.mod(np.asarray(positions, np.float64)[:, None] * inv[None, :], 2.0 * np.pi)
    ang = np.concatenate([ang, ang], axis=1).astype(np.float32)
    n = ang.shape[0]
    pad = (-n) % 8
    ang = np.pad(ang, ((0, pad), (0, 0)))
    shp = jax.ShapeDtypeStruct(ang.shape, f32)
    cos, sin = pl.pallas_call(_rope_kernel, out_shape=(shp, shp), name="rope_tables")(jnp.asarray(ang))
    return cos[:n], sin[:n]


def _memkv_kernel(m_ref, wk_ref, wv_ref, k_ref, v_ref, kb_ref, vb_ref):
    m = m_ref[...]
    k = _mm(m, wk_ref[...])
    v = _mm(m, wv_ref[...])
    k_ref[...] = k
    v_ref[...] = v
    kb_ref[...] = k.astype(bf16)
    vb_ref[...] = v.astype(bf16)


def _memkv(mem2d, wk, wv):
    rows, d = mem2d.shape
    depth = wk.shape[0]
    tile = min(ROW_TILE, rows)
    w_spec = pl.BlockSpec((None, d, W_BR), lambda l, i: (l, 0, 0))
    o_spec = pl.BlockSpec((None, tile, W_BR), lambda l, i: (l, i, 0))
    return pl.pallas_call(
        _memkv_kernel,
        grid=(depth, rows // tile),
        in_specs=[pl.BlockSpec((tile, d), lambda l, i: (i, 0)), w_spec, w_spec],
        out_specs=[o_spec] * 4,
        out_shape=[jax.ShapeDtypeStruct((depth, rows, W_BR), f32)] * 2
        + [jax.ShapeDtypeStruct((depth, rows, W_BR), bf16)] * 2,
        compiler_params=pltpu.CompilerParams(dimension_semantics=("arbitrary", "arbitrary"),
                                             vmem_limit_bytes=VMEM_LIMIT),
        name="mem_kv",
    )(mem2d, wk, wv)


class _OfLayer:
    def __init__(self, array, layer):
        self.array, self.layer = array, layer


def _const_specs(consts):
    specs, operands = [], []
    for a in consts:
        if isinstance(a, _OfLayer):
            arr, layer = a.array, a.layer
            specs.append(pl.BlockSpec((None,) + arr.shape[1:], lambda *_, n=arr.ndim, l=layer: (l,) + (0,) * (n - 1)))
        else:
            arr = a
            specs.append(pl.BlockSpec(arr.shape, lambda *_, n=arr.ndim: (0,) * n))
        operands.append(arr)
    return specs, operands


def _prompt_call(kernel, name, x, tile, consts, per_batch, per_time, outs_tile, outs_batch, scratch, seqs=1):
    B, T, D = x.shape
    assert T % tile == 0 and B % seqs == 0
    nt = T // tile
    in_specs = [pl.BlockSpec((seqs, tile, D), lambda b, t: (b, t, 0))]
    const_specs, consts = _const_specs(consts)
    in_specs += const_specs
    for a in per_batch:
        in_specs.append(pl.BlockSpec((seqs,) + a.shape[1:], lambda b, t, n=a.ndim: (b,) + (0,) * (n - 1)))
    for a in per_time:
        in_specs.append(pl.BlockSpec((tile,) + a.shape[1:], lambda b, t, n=a.ndim: (t,) + (0,) * (n - 1)))
    out_specs, out_shape = [], []
    for width, dt in outs_tile:
        out_specs.append(pl.BlockSpec((seqs, tile, width), lambda b, t: (b, t, 0)))
        out_shape.append(jax.ShapeDtypeStruct((B, T, width), dt))
    for shp, dt in outs_batch:
        out_specs.append(pl.BlockSpec((seqs,) + shp, lambda b, t, n=len(shp): (b,) + (0,) * n))
        out_shape.append(jax.ShapeDtypeStruct((B,) + shp, dt))
    return pl.pallas_call(
        kernel,
        grid=(B // seqs, nt),
        in_specs=in_specs,
        out_specs=out_specs,
        out_shape=out_shape,
        scratch_shapes=scratch,
        compiler_params=pltpu.CompilerParams(dimension_semantics=("arbitrary", "arbitrary"),
                                             vmem_limit_bytes=VMEM_LIMIT),
        name=name,
    )(x, *consts, *per_batch, *per_time)


def _lru_prompt_kernel(x_ref, w_ref, cw_ref, cb_ref, wa_ref, ba_ref, wi_ref, bi_ref, lam_ref,
                       y_ref, h_ref, conv_ref, buf_ref, a_ref, b_ref, hcar_ref):
    n = x_ref.shape[1]

    @pl.when(pl.program_id(1) == 0)
    def _():
        buf_ref[0:CARRY_ROWS, :] = jnp.zeros((CARRY_ROWS, W_BR), f32)
        hcar_ref[...] = jnp.zeros_like(hcar_ref)

    xa = _mm(x_ref[0], w_ref[...])
    xc = _causal_conv(buf_ref, xa, cw_ref[...]) + cb_ref[...]
    a, b = _lru_gates(xc, wa_ref, ba_ref[...], wi_ref, bi_ref[...], lam_ref[...])

    a = a.reshape(n // 8, 8, W_BR)
    b = b.reshape(n // 8, 8, W_BR)
    row8 = lax.broadcasted_iota(jnp.int32, (n // 8, 8, W_BR), 1)
    for d in (1, 2, 4):
        keep = row8 >= d
        a_prev = jnp.where(keep, pltpu.roll(a, d, 1), 1.0)
        b_prev = jnp.where(keep, pltpu.roll(b, d, 1), 0.0)
        b = b + a * b_prev
        a = a * a_prev
    a_ref[...] = a.reshape(n, W_BR)
    b_ref[...] = b.reshape(n, W_BR)

    def group(gi, h_prev):
        i = pl.multiple_of(gi * 8, 8)
        blk = b_ref[pl.ds(i, 8), :] + a_ref[pl.ds(i, 8), :] * h_prev
        b_ref[pl.ds(i, 8), :] = blk
        return jnp.broadcast_to(blk[7:8, :], (8, W_BR))

    h_last = lax.fori_loop(0, n // 8, group, hcar_ref[...])
    hcar_ref[...] = h_last
    y_ref[0] = b_ref[...].astype(y_ref.dtype)
    h_ref[0] = h_last[0:1, :]
    conv_ref[0] = _conv_state(buf_ref)


def _gdn_prompt_kernel(x_ref, wqkv_ref, wz_ref, wba_ref, cw_ref, alog_ref, dtb_ref, nw_ref,
                       y_ref, conv_ref, s_ref, buf_ref, st_ref):
    seqs, n_seq, C = x_ref.shape[0], x_ref.shape[1], GDN_CHUNK
    assert n_seq == N_HEADS * C
    n = seqs * n_seq

    @pl.when(pl.program_id(1) == 0)
    def _():
        buf_ref[:, 0:CARRY_ROWS, :] = jnp.zeros((seqs, CARRY_ROWS, 3 * W_BR), f32)
        st_ref[...] = jnp.zeros_like(st_ref)

    x_seq = [x_ref[s].astype(bf16) for s in range(seqs)]
    x = jnp.concatenate(x_seq, 0)
    qkv = [_mm(x_seq[s], wqkv_ref[...]) for s in range(seqs)]
    c = jnp.concatenate([_silu(_causal_conv(buf_ref.at[s], qkv[s], cw_ref[...])) for s in range(seqs)], 0)
    z = _mm(x, wz_ref[...])
    beta, g = _gdn_gates(_mm(x, wba_ref[...]), alog_ref[...], dtb_ref[...])

    incl_t = _as_bf16(_chunk_masks(n_seq, C)[0])
    G_seq = [_mask_mm_exact(incl_t, g[s * n_seq:(s + 1) * n_seq]) for s in range(seqs)]
    G = jnp.concatenate(G_seq, 0)
    GT = [a.T for a in G_seq]
    H, HD, W = N_HEADS, HEAD_DIM, W_BR
    nck = n // C
    nck_seq = n_seq // C
    heads = range(H)
    tile = lambda a, h: a[:, h * HD:(h + 1) * HD]

    q_all = jnp.concatenate([_l2norm(tile(c, h)) * HD ** -0.5 for h in heads], 1)
    k_all = jnp.concatenate([_l2norm(tile(c, H + h)) for h in heads], 1)
    v_all = c[:, 2 * W:]
    per_head = lambda a, lo: jnp.concatenate([jnp.broadcast_to(a[:, lo + h:lo + h + 1], (n, HD)) for h in heads], 1)
    beta_w = per_head(beta, 0)
    g_w = per_head(G, H)
    glast_w = jnp.concatenate([jnp.broadcast_to(g_w[(ck + 1) * C - 1:(ck + 1) * C, :], (C, W)) for ck in range(nck)], 0)
    eg_w = jnp.exp(g_w)
    kb_all = k_all * beta_w
    vb_all = v_all * beta_w
    kbe_all = kb_all * eg_w
    qe_all = q_all * eg_w
    kend_all = k_all * jnp.exp(glast_w - g_w)
    eglast_w = jnp.exp(glast_w)

    lane_blk = lax.broadcasted_iota(jnp.int32, (n, H * C), 1) >> int(math.log2(C))
    gcol_c = G[:, H + H - 1:H + H]
    for h in reversed(range(H - 1)):
        gcol_c = jnp.where(lane_blk == h, G[:, H + h:H + h + 1], gcol_c)
    ri = lax.broadcasted_iota(jnp.int32, (C, H * C), 0)
    lj = lax.broadcasted_iota(jnp.int32, (C, H * C), 1)
    incl, strict = (lj & (C - 1)) <= ri, (lj & (C - 1)) < ri
    row_blk = lax.broadcasted_iota(jnp.int32, (1, H * C), 1) >> int(math.log2(C))
    bd_mask = (lax.broadcasted_iota(jnp.int32, (H * C, H * C), 0) >> int(math.log2(C))) == (
        lax.broadcasted_iota(jnp.int32, (H * C, H * C), 1) >> int(math.log2(C)))

    def block_diag(parts, width):
        zero = jnp.zeros((parts[0].shape[0], width), parts[0].dtype)
        return jnp.concatenate([jnp.concatenate([parts[h] if h == hp else zero for h in heads], 1) for hp in heads], 0)

    def block_diag_small(m_b):
        return jnp.where(bd_mask, jnp.concatenate([m_b] * H, 0), jnp.zeros((), bf16))

    neg_a, a_qk, rhs_w, rhs_bd = [], [], [], []
    for ck in range(nck):
        sl = slice(ck * C, (ck + 1) * C)
        gt, cs = GT[ck // nck_seq], ck % nck_seq
        g_row = None
        for h in reversed(heads):
            r = gt[H + h:H + h + 1, :]
            r = pltpu.roll(r, ((h - cs) * C) % n_seq, 1) if (h - cs) % nck_seq else r
            g_row = r if g_row is None else jnp.where(row_blk == h, r, g_row)
        decay = jnp.where(incl, jnp.exp(gcol_c[sl] - g_row), 0.0)
        k_b = k_all[sl].astype(bf16)
        k_bd = block_diag([tile(k_b, h) for h in heads], HD)
        neg_a.append(jnp.where(strict, -(_mm_nt(kb_all[sl], k_bd) * decay), 0.0))
        a_qk.append(_mm_nt(q_all[sl], k_bd) * decay)
        parts = [jnp.concatenate([tile(vb_all[sl], h), tile(kbe_all[sl], h)], 1) for h in heads]
        rhs_w.append(jnp.concatenate(parts, 1))
        rhs_bd.append(block_diag([p.astype(bf16) for p in parts], 2 * HD))

    dot = functools.partial(jnp.dot, preferred_element_type=f32)
    m = list(neg_a)
    t_off = list(neg_a)
    m_bd = [block_diag_small(a.astype(bf16)) for a in m]
    for _ in range(int(math.log2(C)) - 1):
        m = [dot(m[ck].astype(bf16), m_bd[ck]) for ck in range(nck)]
        m_bd = [block_diag_small(a.astype(bf16)) for a in m]
        t_off = [t_off[ck] + m[ck] + dot(t_off[ck].astype(bf16), m_bd[ck]) for ck in range(nck)]
    sol = [rhs_w[ck] + dot(t_off[ck].astype(bf16), rhs_bd[ck]) for ck in range(nck)]

    S = [st_ref[s] for s in range(seqs)]
    zero_s = jnp.zeros((HD, HD), bf16)
    o_chunks = [None] * nck
    for cs in range(nck_seq):
        for s in range(seqs):
            ck = s * nck_seq + cs
            sl = slice(ck * C, (ck + 1) * C)
            S_b = S[s].astype(bf16)
            s_pairs = [jnp.concatenate([jnp.concatenate([tile(S_b, p), zero_s], 1),
                                        jnp.concatenate([zero_s, tile(S_b, p + 1)], 1)], 0) for p in (0, 2)]
            u = [sol[ck][:, 2 * h * HD:(2 * h + 1) * HD] for h in heads]
            w = [sol[ck][:, (2 * h + 1) * HD:(2 * h + 2) * HD].astype(bf16) for h in heads]
            w_s = jnp.concatenate([dot(jnp.concatenate([w[p], w[p + 1]], 1), s_pairs[p // 2]) for p in (0, 2)], 1)
            v_new = jnp.concatenate(u, 1) - w_s
            v_b = v_new.astype(bf16)
            v_bd = block_diag([tile(v_b, h) for h in heads], HD)
            qe_b = qe_all[sl].astype(bf16)
            q_s = jnp.concatenate([dot(qe_b[:, p * HD:(p + 2) * HD], s_pairs[p // 2]) for p in (0, 2)], 1)
            o_chunks[ck] = q_s + dot(a_qk[ck].astype(bf16), v_bd)
            kend_stack = jnp.concatenate([tile(kend_all[sl], h) for h in heads], 0)
            S[s] = eglast_w[ck * C:ck * C + 1, :] * S[s] + _mm_tn(kend_stack, v_bd)
    o_all = jnp.concatenate(o_chunks, 0)
    y = _rmsnorm_heads([tile(o_all, h) for h in heads], nw_ref[...]) * _silu(z)
    for s in range(seqs):
        st_ref[s] = S[s]
        for h in heads:
            s_ref[s, h] = tile(S[s], h)
        y_ref[s] = y[s * n_seq:(s + 1) * n_seq].astype(y_ref.dtype)
        conv_ref[s] = _conv_state(buf_ref.at[s])


def _hgrn_prompt_kernel(layer, x_ref, w_ref, lbraw_ref, nw_ref, y_ref, s_ref, st_ref):
    n, C = x_ref.shape[1], HGRN_CHUNK

    @pl.when(pl.program_id(1) == 0)
    def _():
        st_ref[...] = jnp.zeros_like(st_ref)

    seqs = x_ref.shape[0]
    rows, nck = seqs * n, n // C
    u = _mm(jnp.concatenate([x_ref[s] for s in range(seqs)], 0), w_ref[...])
    hq, fz, hi, hg = (u[:, i * W_BR:(i + 1) * W_BR] for i in range(4))
    lb = _hgrn_lower_bound(lbraw_ref[...], layer)
    log_f, k = _hgrn_gates(fz, lb)
    q = _silu(hq)

    incl, _ = _chunk_masks(n, C)
    incl_b = _as_bf16(incl)
    G = jnp.concatenate([_mask_mm_exact(incl_b, log_f[s * n:(s + 1) * n]) for s in range(seqs)], 0)
    g_last = [G[(ck + 1) * C - 1:(ck + 1) * C, :] for ck in range(rows // C)]
    Gtot = jnp.concatenate([jnp.broadcast_to(r, (C, W_BR)) for r in g_last], 0)
    q_g = q * jnp.exp(G)
    k_g = k * jnp.exp(-G)
    k_end = k * jnp.exp(Gtot - G)
    dT = jnp.exp(jnp.concatenate(g_last, 0)).T

    heads = range(N_HEADS)
    hsl = [slice(h * HEAD_DIM, (h + 1) * HEAD_DIM) for h in heads]
    csl = [slice(ck * C, (ck + 1) * C) for ck in range(rows // C)]
    ssl = [slice(s * n, (s + 1) * n) for s in range(seqs)]
    lanes = [(s, h) for s in range(seqs) for h in heads]
    a_qk = {(s, h): jnp.where(incl, _mm_nt(q_g[ssl[s], hsl[h]], k_g[ssl[s], hsl[h]]), 0.0) for s, h in lanes}
    o_intra = {(s, h): _mm(a_qk[s, h], hi[ssl[s], hsl[h]]) for s, h in lanes}
    kv_of = lambda s, h, cs: _mm_tn(k_end[csl[s * nck + cs], hsl[h]], hi[csl[s * nck + cs], hsl[h]])
    kv = {}
    for cs in range(min(HGRN_LOOKAHEAD, nck)):
        for s, h in lanes:
            kv[s, h, cs] = kv_of(s, h, cs)
    S = {(s, h): st_ref[s, h] for s, h in lanes}
    o_inter = {}
    for cs in range(nck):
        if cs + HGRN_LOOKAHEAD < nck:
            for s, h in lanes:
                kv[s, h, cs + HGRN_LOOKAHEAD] = kv_of(s, h, cs + HGRN_LOOKAHEAD)
        for s, h in lanes:
            o_inter[s, h, cs] = _mm(q_g[csl[s * nck + cs], hsl[h]], S[s, h])
        for s, h in lanes:
            S[s, h] = dT[hsl[h], s * nck + cs:s * nck + cs + 1] * S[s, h] + kv.pop((s, h, cs))
    for s, h in lanes:
        st_ref[s, h] = S[s, h]
        s_ref[s, h] = S[s, h]
    o_heads = [jnp.concatenate([o_intra[s, h] + jnp.concatenate([o_inter[s, h, cs] for cs in range(nck)], 0)
                                for s in range(seqs)], 0) for h in heads]

    y = _rmsnorm_heads(o_heads, nw_ref[...]) * _sigmoid(hg)
    for s in range(seqs):
        y_ref[s] = y[ssl[s]].astype(y_ref.dtype)


def _ret_prompt_kernel(x_ref, w_ref, gw_ref, gb_ref, cos_ref, sin_ref, y_ref, s_ref, st_ref):
    n = x_ref.shape[1]

    @pl.when(pl.program_id(1) == 0)
    def _():
        st_ref[...] = jnp.zeros_like(st_ref)

    cos, sin = cos_ref[...], sin_ref[...]
    ri = lax.broadcasted_iota(jnp.int32, (n, n), 0)
    rj = lax.broadcasted_iota(jnp.int32, (n, n), 1)
    dist = (ri - rj).astype(f32)
    pos = lax.broadcasted_iota(jnp.int32, (n, 1), 0).astype(f32)
    heads = range(N_HEADS)
    lg = [_ret_log_gamma(h) for h in heads]

    for s in range(x_ref.shape[0]):
        xb = x_ref[s].astype(bf16)
        proj = [[_mm(xb, w_ref[:, i * W_BR + 2 * p * HEAD_DIM:i * W_BR + (2 * p + 2) * HEAD_DIM]) for i in range(4)]
                for p in range(N_HEADS // 2)]
        piece = lambda i, h: proj[h // 2][i][:, (h % 2) * HEAD_DIM:(h % 2 + 1) * HEAD_DIM]
        q = [_rotary(piece(0, h), cos, sin) for h in heads]
        k = [_rotary(piece(1, h), cos, sin) * HEAD_DIM ** -0.5 for h in heads]
        v = [piece(2, h) for h in heads]
        a_qk = [_mm_nt(q[h], k[h]) * jnp.where(rj <= ri, jnp.exp(dist * lg[h]), 0.0) for h in heads]
        S = [st_ref[s, h] for h in heads]
        o = [_mm(a_qk[h], v[h]) + _mm(q[h] * jnp.exp((pos + 1.0) * lg[h]), S[h]) for h in heads]
        S = [math.exp(n * lg[h]) * S[h] + _mm_tn(k[h] * jnp.exp((n - 1.0 - pos) * lg[h]), v[h]) for h in heads]
        for h in heads:
            st_ref[s, h] = S[h]
            s_ref[s, h] = S[h]
        rg = jnp.concatenate([piece(3, h) for h in heads], 1)
        y = _groupnorm_heads(o, gw_ref[...], gb_ref[...]) * _silu(rg)
        y_ref[s] = y.astype(y_ref.dtype)


def _xattn_prompt_kernel(x_ref, w_ref, mk_ref, mv_ref, y_ref):
    hsl = [slice(h * HEAD_DIM, (h + 1) * HEAD_DIM) for h in range(N_HEADS)]
    for b in range(x_ref.shape[0]):
        q = _mm(x_ref[b], w_ref[...])
        mk, mv = mk_ref[b], mv_ref[b]
        s = [_mm_nt(q[:, hs], mk[:, hs]) * HEAD_DIM ** -0.5 for hs in hsl]
        e = [jnp.exp(a - jnp.max(a, -1, keepdims=True)) for a in s]
        p = [a / jnp.sum(a, -1, keepdims=True) for a in e]
        outs = [_mm(a, mv[:, hs]) for a, hs in zip(p, hsl)]
        y_ref[b] = jnp.concatenate(outs, -1).astype(y_ref.dtype)


def _row_blocks(n):
    sub = min(ROW_SUBTILE, n)
    return [slice(r, r + sub) for r in range(0, n, sub)]


def _merge_kernel(alpha, x_ref, ya_ref, yb_ref, yc_ref, yd_ref, ye_ref, wg_ref, bg_ref, wb_ref, wo_ref,
                  g_ref, b_ref, o_ref):
    for rs in _row_blocks(x_ref.shape[0]):
        x = x_ref[rs, :]
        xb = x.astype(bf16)
        merged = None
        for i, y_ref in enumerate((ya_ref, yb_ref, yc_ref, yd_ref, ye_ref)):
            gate = _sigmoid(_mm(xb, wg_ref[i]) + bg_ref[i])
            term = gate * _mm(y_ref[rs, :], wb_ref[i])
            merged = term if merged is None else merged + term
        mix = _mm(merged, wo_ref[...])
        o_ref[rs, :] = _layernorm(alpha * x + mix, g_ref[...], b_ref[...])


def _ffn_kernel(alpha, n_split, x_ref, wu_ref, wd_ref, g_ref, b_ref, o_ref):
    d_ff = wd_ref.shape[0]
    step = d_ff // n_split
    for rs in _row_blocks(x_ref.shape[0]):
        x = x_ref[rs, :]
        xb = x.astype(bf16)
        ffn = None
        for i in range(n_split):
            cs = slice(i * step, (i + 1) * step)
            vs = slice(d_ff + i * step, d_ff + (i + 1) * step)
            hidden = _silu(_mm(xb, wu_ref[:, cs])) * _mm(xb, wu_ref[:, vs])
            part = _mm(hidden, wd_ref[cs, :])
            ffn = part if ffn is None else ffn + part
        o_ref[rs, :] = _layernorm(alpha * x + ffn, g_ref[...], b_ref[...])


def _row_call(kernel, name, row_ops, consts, layer, d_out):
    rows = row_ops[0].shape[0]
    tile = min(ROW_TILE, rows)
    in_specs = [pl.BlockSpec((tile, a.shape[1]), lambda i: (i, 0)) for a in row_ops]
    in_specs += [pl.BlockSpec((None,) + a.shape[1:], lambda i, n=a.ndim: (layer,) + (0,) * (n - 1),
                              pipeline_mode=pl.Buffered(1)) for a in consts]
    return pl.pallas_call(
        kernel,
        grid=(rows // tile,),
        in_specs=in_specs,
        out_specs=pl.BlockSpec((tile, d_out), lambda i: (i, 0)),
        out_shape=jax.ShapeDtypeStruct((rows, d_out), f32),
        compiler_params=pltpu.CompilerParams(dimension_semantics=("arbitrary",), vmem_limit_bytes=VMEM_LIMIT),
        name=name,
    )(*row_ops, *consts)


def _proj_kernel(x_ref, *refs):
    w_refs, o_ref = refs[:-1], refs[-1]
    x = x_ref[...].astype(bf16)
    off = 0
    for w_ref in w_refs:
        width = w_ref.shape[1]
        o_ref[:, off:off + width] = _mm(x, w_ref[...])
        off += width


def _sample_proj(x, weights):
    rows = x.shape[0]
    w_specs, w_arrays = _const_specs(weights)
    n = sum(w.shape[-1] for w in w_arrays)
    return pl.pallas_call(
        _proj_kernel,
        grid=(1,),
        in_specs=[pl.BlockSpec(x.shape, lambda i: (0, 0))] + w_specs,
        out_specs=pl.BlockSpec((rows, n), lambda i: (0, 0)),
        out_shape=jax.ShapeDtypeStruct((rows, n), f32),
        compiler_params=pltpu.CompilerParams(dimension_semantics=("arbitrary",), vmem_limit_bytes=VMEM_LIMIT),
        name="sample_proj",
    )(x, *w_arrays)


def _columns(rows_list):
    nb = rows_list[0].shape[0]
    out = []
    for b in range(nb):
        tile = jnp.concatenate([r[b:b + 1, :] for r in rows_list]
                               + [jnp.zeros((8 - len(rows_list), HEAD_DIM), f32)], 0)
        out.append(tile.T)
    return out


N_SAMPLE_INPUTS = 24


def _rows_of(parts):
    nb = 1 + max(bi for bi, _ in parts)
    return [jnp.concatenate([parts[bi, h] for bi in range(nb)], 0) for h in range(N_HEADS)]


def _sample_kernel(layer, n_alias, *refs):
    (u_ref, h0_ref, lconv_ref, gconv_ref, sg_ref, sh_ref, sr_ref,
     lcw_ref, lcb_ref, lwa_ref, lba_ref, lwi_ref, lbi_ref, lam_ref,
     gcw_ref, alog_ref, dtb_ref, gnw_ref, lbraw_ref, hnw_ref, rgw_ref, rgb_ref, cos_ref, sin_ref) = refs[:N_SAMPLE_INPUTS]
    (ya_ref, yb_ref, yc_ref, yd_ref, h1_ref, lconv_o_ref, gconv_o_ref,
     sg_o_ref, sh_o_ref, sr_o_ref) = refs[N_SAMPLE_INPUTS + n_alias:]
    if n_alias == 0:
        for o_ref in (sg_o_ref, sh_o_ref, sr_o_ref):
            for other in range(o_ref.shape[0]):
                if other != layer:
                    o_ref[other] = jnp.zeros(o_ref.shape[1:], f32)
        sg_o_ref, sh_o_ref, sr_o_ref = (r.at[layer] for r in (sg_o_ref, sh_o_ref, sr_o_ref))
    nb = SAMPLE_BLOCK
    W = W_BR
    pairs = [(bi, h) for bi in range(nb) for h in range(N_HEADS)]
    u = u_ref[...]
    off = 0

    def take(width):
        nonlocal off
        part = u[:, off:off + width]
        off += width
        return part

    xa, qkv, z = take(W), take(3 * W), take(W)
    hq, fz, hi, hg = take(W), take(W), take(W), take(W)
    rq, rk, rv, rg = take(W), take(W), take(W), take(W)
    xq, ba = take(W), take(HEAD_DIM)

    lcw = lcw_ref[...]
    xc = lcw[CONV_W - 1:CONV_W, :] * xa + lcb_ref[...]
    for j in range(CONV_W - 1):
        xc = xc + lcw[j:j + 1, :] * lconv_ref[j]
    for j in range(CONV_W - 2):
        lconv_o_ref[j] = lconv_ref[j + 1]
    lconv_o_ref[CONV_W - 2] = xa
    a, b = _lru_gates(xc, lwa_ref, lba_ref[...], lwi_ref, lbi_ref[...], lam_ref[...])
    h1 = a * h0_ref[...] + b
    h1_ref[...] = h1
    ya_ref[...] = h1

    gcw = gcw_ref[...]
    c = gcw[CONV_W - 1:CONV_W, :] * qkv
    for j in range(CONV_W - 1):
        c = c + gcw[j:j + 1, :] * gconv_ref[j]
    for j in range(CONV_W - 2):
        gconv_o_ref[j] = gconv_ref[j + 1]
    gconv_o_ref[CONV_W - 2] = qkv
    c = _silu(c)
    beta, g = _gdn_gates(ba, alog_ref[...], dtb_ref[...])
    eg = jnp.exp(g)
    qn, kn, vv, kcol = [], [], [], {}
    for h in range(N_HEADS):
        qn.append(_l2norm(c[:, h * HEAD_DIM:(h + 1) * HEAD_DIM]) * HEAD_DIM ** -0.5)
        kn.append(_l2norm(c[:, W + h * HEAD_DIM:W + (h + 1) * HEAD_DIM]))
        vv.append(c[:, 2 * W + h * HEAD_DIM:2 * W + (h + 1) * HEAD_DIM])
        cols = _columns([kn[h]])
        for bi in range(nb):
            kcol[bi, h] = cols[bi][:, 0:1]
    kS = _rows_of({(bi, h): _mm(kn[h][bi:bi + 1, :], sg_ref[bi, h]) for bi, h in pairs})
    v_new = [beta[:, h:h + 1] * (vv[h] - eg[:, N_HEADS + h:N_HEADS + h + 1] * kS[h]) for h in range(N_HEADS)]
    S1 = {}
    for bi, h in pairs:
        S1[bi, h] = (eg[bi:bi + 1, N_HEADS + h:N_HEADS + h + 1] * sg_ref[bi, h]
                     + kcol[bi, h] * v_new[h][bi:bi + 1, :])
        sg_o_ref[bi, h] = S1[bi, h]
    o = _rows_of({(bi, h): _mm(qn[h][bi:bi + 1, :], S1[bi, h]) for bi, h in pairs})
    yb_ref[...] = _rmsnorm_heads(o, gnw_ref[...]) * _silu(z)

    lb = _hgrn_lower_bound(lbraw_ref[...], layer)
    log_f, kk = _hgrn_gates(fz, lb)
    f = jnp.exp(log_f)
    qs = _silu(hq)
    kcol, fcol = {}, {}
    for h in range(N_HEADS):
        hs = slice(h * HEAD_DIM, (h + 1) * HEAD_DIM)
        cols = _columns([kk[:, hs], f[:, hs]])
        for bi in range(nb):
            kcol[bi, h], fcol[bi, h] = cols[bi][:, 0:1], cols[bi][:, 1:2]
    S1 = {}
    for bi, h in pairs:
        hs = slice(h * HEAD_DIM, (h + 1) * HEAD_DIM)
        S1[bi, h] = fcol[bi, h] * sh_ref[bi, h] + kcol[bi, h] * hi[bi:bi + 1, hs]
        sh_o_ref[bi, h] = S1[bi, h]
    o = _rows_of({(bi, h): _mm(qs[bi:bi + 1, h * HEAD_DIM:(h + 1) * HEAD_DIM], S1[bi, h]) for bi, h in pairs})
    yc_ref[...] = _rmsnorm_heads(o, hnw_ref[...]) * _sigmoid(hg)

    cos, sin = cos_ref[...], sin_ref[...]
    qr, kcol = [], {}
    for h in range(N_HEADS):
        hs = slice(h * HEAD_DIM, (h + 1) * HEAD_DIM)
        qr.append(_rotary(rq[:, hs], cos, sin))
        cols = _columns([_rotary(rk[:, hs], cos, sin) * HEAD_DIM ** -0.5])
        for bi in range(nb):
            kcol[bi, h] = cols[bi][:, 0:1]
    S1 = {}
    for bi, h in pairs:
        hs = slice(h * HEAD_DIM, (h + 1) * HEAD_DIM)
        S1[bi, h] = math.exp(_ret_log_gamma(h)) * sr_ref[bi, h] + kcol[bi, h] * rv[bi:bi + 1, hs]
        sr_o_ref[bi, h] = S1[bi, h]
    o = _rows_of({(bi, h): _mm(qr[h][bi:bi + 1, :], S1[bi, h]) for bi, h in pairs})
    yd_ref[...] = _groupnorm_heads(o, rgw_ref[...], rgb_ref[...]) * _silu(rg)


def _sample_xattn_kernel(q_ref, ck_ref, cv_ref, o_ref):
    for bi in range(q_ref.shape[0]):
        q = q_ref[bi] * (HEAD_DIM ** -0.5 * math.log2(math.e))
        s = jnp.sum(ck_ref[bi] * q[None, :, :], axis=-1, keepdims=True)
        e = jnp.exp2(s - jnp.max(s, axis=0, keepdims=True))
        o_ref[bi] = jnp.sum(e * cv_ref[bi], axis=0) / jnp.sum(e, axis=0)


def _sample_xattn(layer, q, ck, cv):
    nb = SAMPLE_BLOCK
    cache_spec = pl.BlockSpec((None, nb) + ck.shape[2:], lambda i: (layer, i, 0, 0, 0))
    q_spec = pl.BlockSpec((nb,) + q.shape[1:], lambda i: (i, 0, 0))
    return pl.pallas_call(
        _sample_xattn_kernel,
        grid=(q.shape[0] // nb,),
        in_specs=[q_spec, cache_spec, cache_spec],
        out_specs=q_spec,
        out_shape=jax.ShapeDtypeStruct(q.shape, f32),
        compiler_params=pltpu.CompilerParams(dimension_semantics=("arbitrary",), vmem_limit_bytes=VMEM_LIMIT),
        name="sample_xattn",
    )(q, ck, cv)


def _sample_branches(layer, u, h0, lconv_t, gconv_t, sg, sh, sr, params, prev_states):
    nb = SAMPLE_BLOCK
    rows = u.shape[0]
    blk2 = lambda a: pl.BlockSpec((nb, a.shape[1]), lambda i: (i, 0))
    blk_t = lambda a: pl.BlockSpec((a.shape[0], nb, a.shape[2]), lambda i: (0, i, 0))
    blk_layer = lambda a: pl.BlockSpec((None, nb) + a.shape[2:],
                                       lambda i, n=a.ndim: (layer, i) + (0,) * (n - 2))
    param_specs, params = _const_specs(params)
    sds = lambda a: jax.ShapeDtypeStruct(a.shape, f32)
    y_shape = jax.ShapeDtypeStruct((rows, W_BR), f32)
    y_spec = pl.BlockSpec((nb, W_BR), lambda i: (i, 0))
    n_in = N_SAMPLE_INPUTS
    assert 7 + len(params) == n_in
    n_small_outs = 7
    blk_state_out = blk_layer if prev_states else (
        lambda a: pl.BlockSpec((a.shape[0], nb) + a.shape[2:], lambda i, n=a.ndim: (0, i) + (0,) * (n - 2)))
    return pl.pallas_call(
        functools.partial(_sample_kernel, layer, len(prev_states)),
        grid=(rows // nb,),
        in_specs=[blk2(u), blk2(h0), blk_t(lconv_t), blk_t(gconv_t), blk_layer(sg), blk_layer(sh), blk_layer(sr)]
        + param_specs + [pl.BlockSpec(memory_space=pl.ANY)] * len(prev_states),
        out_specs=[y_spec] * 4 + [blk2(h0), blk_t(lconv_t), blk_t(gconv_t),
                                  blk_state_out(sg), blk_state_out(sh), blk_state_out(sr)],
        out_shape=[y_shape] * 4 + [sds(h0), sds(lconv_t), sds(gconv_t), sds(sg), sds(sh), sds(sr)],
        input_output_aliases={n_in + j: n_small_outs + j for j in range(len(prev_states))},
        compiler_params=pltpu.CompilerParams(dimension_semantics=("arbitrary",), vmem_limit_bytes=VMEM_LIMIT),
        name="sample_branches",
    )(u, h0, lconv_t, gconv_t, sg, sh, sr, *params, *prev_states)


def kernel(x_prompt, x_sample, mem_prompt, state_lru_h, state_lru_conv, state_gdn_conv, state_gdn_s, state_hgrn_s, state_ret_s, cache_mem_k, cache_mem_v, w_in, lru_conv_w, lru_conv_b, lru_wa, lru_ba, lru_wi, lru_bi, lru_lambda, gdn_conv_w, gdn_a_log, gdn_dt_bias, gdn_norm_w, hgrn_lb_raw, hgrn_norm_w, ret_gn_w, ret_gn_b, w_mem_k, w_mem_v, w_merge_gate, b_merge_gate, w_branch, w_out, ln1_g, ln1_b, w_ffn_up, w_ffn_down, ln2_g, ln2_b):
    B, T, D = x_prompt.shape
    Bs = x_sample.shape[0]
    depth = w_in.shape[0]
    n_mem = mem_prompt.shape[1]
    W, H, HD = W_BR, N_HEADS, HEAD_DIM
    alpha = (2 * depth) ** 0.25
    assert Bs % SAMPLE_BLOCK == 0 and (B * T) % ROW_TILE == 0

    cos_all, sin_all = _rope_tables(list(range(T)) + [PAST_LEN])
    cos_p, sin_p = cos_all[:T], sin_all[:T]
    cos_s, sin_s = cos_all[T:T + 1], sin_all[T:T + 1]

    state_spec = ((H, HD, HD), f32)

    xp = x_prompt
    xs = x_sample.reshape(Bs, D)
    mk_all, mv_all, mkb_all, mvb_all = _memkv(mem_prompt.reshape(B * n_mem, D), w_mem_k, w_mem_v)
    outs_p = [[] for _ in range(6)]
    outs_s = [[] for _ in range(3)]
    sample_states = ()
    layer_rows = lambda a: a.reshape(depth, 1, -1)
    merge_consts = (w_merge_gate.astype(bf16), b_merge_gate.reshape(depth, N_BR, 1, D), w_branch.astype(bf16),
                    w_out.astype(bf16), layer_rows(ln1_g), layer_rows(ln1_b))
    ffn_consts = (w_ffn_up.astype(bf16), w_ffn_down.astype(bf16), layer_rows(ln2_g), layer_rows(ln2_b))
    merge_k = functools.partial(_merge_kernel, alpha)
    ffn_k = functools.partial(_ffn_kernel, alpha, 2)

    o = 0
    pieces = []
    for width in (W, 3 * W, W, H, H, W, W, W, W, W, W, W, W, W):
        pieces.append(w_in[:, :, o:o + width])
        o += width
    (w_xa, w_qkv, w_z, w_gb, w_ga, w_hq, w_hf, w_hi, w_hg, w_rq, w_rk, w_rv, w_rg, w_xq) = pieces
    w_ba = jnp.concatenate([w_gb, w_ga, jnp.zeros((depth, D, HD - 2 * H), f32)], 2).astype(bf16)
    w_xa, w_qkv, w_z, w_xq = (a.astype(bf16) for a in (w_xa, w_qkv, w_z, w_xq))
    w_hgrn = jnp.concatenate([w_hq, w_hf, w_hi, w_hg], 2).astype(bf16)
    w_ret = jnp.concatenate([w_rq, w_rk, w_rv, w_rg], 2).astype(bf16)
    pad_lanes = lambda a: jnp.zeros((depth, 1, HD), f32).at[:, 0, H:2 * H].set(a)
    lru_all = (lru_conv_w, layer_rows(lru_conv_b), lru_wa.astype(bf16), layer_rows(lru_ba), lru_wi.astype(bf16),
               layer_rows(lru_bi), layer_rows(lru_lambda))
    gdn_all = (gdn_conv_w, pad_lanes(gdn_a_log), pad_lanes(gdn_dt_bias), layer_rows(gdn_norm_w))
    hgrn_norm_all = layer_rows(hgrn_norm_w)
    ret_all = (layer_rows(ret_gn_w), layer_rows(ret_gn_b))

    for l in range(depth):
        of_layer = lambda a: _OfLayer(a, l)
        w_xa_l, w_qkv_l, w_z_l, w_ba_l, w_hgrn_l, w_ret_l, w_xq_l = (
            of_layer(a) for a in (w_xa, w_qkv, w_z, w_ba, w_hgrn, w_ret, w_xq))
        lru_params = tuple(of_layer(a) for a in lru_all)
        gdn_params = tuple(of_layer(a) for a in gdn_all)
        hgrn_params = (hgrn_lb_raw, of_layer(hgrn_norm_all))
        ret_params = tuple(of_layer(a) for a in ret_all)

        ya, h_last, lconv = _prompt_call(
            _lru_prompt_kernel, "lru_prompt", xp, T_TILE_WIDE, (w_xa_l,) + lru_params, (), (),
            [(W, bf16)], [((1, W), f32), ((CONV_W - 1, W), f32)],
            [pltpu.VMEM((CARRY_ROWS + T_TILE_WIDE, W), f32), pltpu.VMEM((T_TILE_WIDE, W), f32),
             pltpu.VMEM((T_TILE_WIDE, W), f32), pltpu.VMEM((8, W), f32)])
        yb, gconv, s_gdn = _prompt_call(
            _gdn_prompt_kernel, "gdn_prompt", xp, T_TILE, (w_qkv_l, w_z_l, w_ba_l) + gdn_params, (), (),
            [(W, bf16)], [((CONV_W - 1, 3 * W), f32), state_spec],
            [pltpu.VMEM((GDN_SEQS, CARRY_ROWS + T_TILE, 3 * W), f32), pltpu.VMEM((GDN_SEQS, HD, W), f32)],
            seqs=GDN_SEQS)
        yc, s_hgrn = _prompt_call(
            functools.partial(_hgrn_prompt_kernel, l), "hgrn_prompt", xp, T_TILE, (w_hgrn_l,) + hgrn_params, (), (),
            [(W, bf16)], [state_spec], [pltpu.VMEM((HGRN_SEQS, H, HD, HD), f32)], seqs=HGRN_SEQS)
        yd, s_ret = _prompt_call(
            _ret_prompt_kernel, "ret_prompt", xp, T_TILE, (w_ret_l,) + ret_params, (), (cos_p, sin_p),
            [(W, bf16)], [state_spec], [pltpu.VMEM((RET_SEQS, H, HD, HD), f32)], seqs=RET_SEQS)
        (ye,) = _prompt_call(
            _xattn_prompt_kernel, "xattn_prompt", xp, T_TILE_WIDE, (w_xq_l,),
            (mkb_all[l].reshape(B, n_mem, W), mvb_all[l].reshape(B, n_mem, W)), (), [(W, bf16)], [], [],
            seqs=XATTN_SEQS)
        x2d = xp.reshape(B * T, D)
        ys = [y.reshape(B * T, W) for y in (ya, yb, yc, yd, ye)]
        x2d = _row_call(merge_k, "merge_prompt", [x2d] + ys, merge_consts, l, D)
        x2d = _row_call(ffn_k, "ffn_prompt", [x2d], ffn_consts, l, D)
        xp = x2d.reshape(B, T, D)
        for lst, val in zip(outs_p, (h_last.reshape(B, W), lconv, gconv, s_gdn, s_hgrn, s_ret)):
            lst.append(val)

        u_s = _sample_proj(xs, (w_xa_l, w_qkv_l, w_z_l, w_hgrn_l, w_ret_l, w_xq_l, w_ba_l))
        sample_params = lru_params + gdn_params + hgrn_params + ret_params + (cos_s, sin_s)
        res = _sample_branches(
            l, u_s, state_lru_h[l], state_lru_conv[l].transpose(1, 0, 2), state_gdn_conv[l].transpose(1, 0, 2),
            state_gdn_s, state_hgrn_s, state_ret_s, sample_params, sample_states)
        ys_s, (h1, lconv_s, gconv_s), sample_states = res[:4], res[4:7], tuple(res[7:])
        xq_off = sum(w.shape[2] for w in (w_xa, w_qkv, w_z, w_hgrn, w_ret))
        q_s = u_s[:, xq_off:xq_off + W].reshape(Bs, H, HD)
        ye_s = _sample_xattn(l, q_s, cache_mem_k, cache_mem_v).reshape(Bs, W)
        xs = _row_call(merge_k, "merge_sample", [xs] + list(ys_s) + [ye_s], merge_consts, l, D)
        xs = _row_call(ffn_k, "ffn_sample", [xs], ffn_consts, l, D)
        for lst, val in zip(outs_s, (h1, lconv_s.transpose(1, 0, 2), gconv_s.transpose(1, 0, 2))):
            lst.append(val)

    mem_shape = (depth, B, n_mem, H, HD)
    return (xp, xs.reshape(Bs, 1, D), *(jnp.stack(v) for v in outs_p), mk_all.reshape(mem_shape),
            mv_all.reshape(mem_shape), *(jnp.stack(v) for v in outs_s), *sample_states)
```
